```python
import jax, jax.numpy as jnp
from jax import lax
import numpy as np

D_MODEL = 1024
BATCH = 8
SEQ = 2048
DEPTH = 1

CHUNK = 64
D_FF = 2816
NORM_EPS = 1e-6

RWKV_HEADS = 8
RWKV_HEAD_DIM = 64
D_RWKV = RWKV_HEADS * RWKV_HEAD_DIM
DECAY_LORA = 64
ICLR_LORA = 64
GATE_LORA = 128
RWKV_GN_EPS = 64e-5

SSM_HEADS = 8
SSM_HEAD_DIM = 64
D_SSM = SSM_HEADS * SSM_HEAD_DIM
SSM_GROUPS = 2
SSM_STATE = 128
CONV_WIDTH = 4
D_CONV = D_SSM + 2 * SSM_GROUPS * SSM_STATE
SSM_NORM_EPS = 1e-5

D_MIX = D_RWKV + D_SSM
D_IN_RWKV = 3 * D_RWKV + DECAY_LORA + ICLR_LORA + GATE_LORA
D_IN_SSM = D_SSM + D_CONV + SSM_HEADS
D_IN = D_IN_RWKV + D_IN_SSM

kernel_name = "hybrid_rwkv7_mamba2_macaron_block"


def rms_norm(x, g, eps=NORM_EPS):
    x32 = x.astype(jnp.float32)
    y = x32 * lax.rsqrt(jnp.mean(x32 * x32, axis=-1, keepdims=True) + eps)
    return y.astype(x.dtype) * g


def swiglu_ffn(h, w_gate, w_up, w_down):
    return (jax.nn.silu(h @ w_gate) * (h @ w_up)) @ w_down


def token_shift(t):
    return jnp.pad(t, ((0, 0), (1, 0), (0, 0)))[:, :-1]


def rwkv7_mix(p, mu, w0, w2, a0, a2, g2, k_k, k_a, r_k, gn_g, gn_b):
    b, L, _ = p.shape
    H, N = RWKV_HEADS, RWKV_HEAD_DIM
    p = p + (token_shift(p) - p) * mu
    r, k, v, wd, ad, gd = jnp.split(
        p, [D_RWKV, 2 * D_RWKV, 3 * D_RWKV, 3 * D_RWKV + DECAY_LORA,
            3 * D_RWKV + DECAY_LORA + ICLR_LORA], axis=-1)
    w_log = -jax.nn.softplus(-(w0 + jnp.tanh(wd) @ w2)) - 0.5
    decay = jnp.exp(-jnp.exp(w_log))
    a = jax.nn.sigmoid(a0 + ad @ a2)
    g = jax.nn.sigmoid(gd) @ g2
    heads = lambda t: t.reshape(b, L, H, N)
    kk = heads(k * k_k)
    kk32 = kk.astype(jnp.float32)
    kk = (kk32 / jnp.maximum(jnp.sqrt(jnp.sum(kk32 * kk32, -1, keepdims=True)), 1e-12)).astype(k.dtype)
    k = k * (1.0 + (a - 1.0) * k_a)
    r_h, w_h, k_h, v_h, a_h = heads(r), heads(decay), heads(k), heads(v), heads(a)

    def step(S, inp):
        rt, wt, kt, vt, kkt, at = inp
        sa = jnp.einsum('bhij,bhj->bhi', S, -kkt)
        S = S * wt[:, :, None, :] + sa[..., None] * (kkt * at)[:, :, None, :] + vt[..., None] * kt[:, :, None, :]
        return S, jnp.einsum('bhij,bhj->bhi', S, rt)

    xs = tuple(jnp.moveaxis(t, 1, 0) for t in (r_h, w_h, k_h, v_h, kk, a_h))
    S0 = jnp.zeros((b, H, N, N), r.dtype)
    _, ys = lax.scan(step, S0, xs)
    y = jnp.moveaxis(ys, 0, 1)
    y32 = y.astype(jnp.float32)
    mean = jnp.mean(y32, -1, keepdims=True)
    var = jnp.mean(jnp.square(y32 - mean), -1, keepdims=True)
    y = ((y32 - mean) * lax.rsqrt(var + RWKV_GN_EPS)).astype(y.dtype).reshape(b, L, D_RWKV) * gn_g + gn_b
    bonus = (jnp.sum(r_h * k_h * r_k, -1, keepdims=True) * v_h).reshape(b, L, D_RWKV)
    return (y + bonus) * g


def causal_depthwise_conv(t, w):
    C = t.shape[-1]
    return lax.conv_general_dilated(
        t, w[:, None, :], window_strides=(1,), padding=[(CONV_WIDTH - 1, 0)],
        dimension_numbers=('NWC', 'WIO', 'NWC'), feature_group_count=C)


def ssd_scan(x, a, Bm, Cm):
    b, L, H, P = x.shape
    G, N = Bm.shape[2], Bm.shape[3]
    R = H // G
    nc = L // CHUNK
    x = x.reshape(b, nc, CHUNK, G, R, P)
    a = a.reshape(b, nc, CHUNK, G, R).transpose(0, 3, 4, 1, 2)
    Bc = Bm.reshape(b, nc, CHUNK, G, N)
    Cc = Cm.reshape(b, nc, CHUNK, G, N)
    a_cs = jnp.cumsum(a, axis=-1)
    causal = jnp.tril(jnp.ones((CHUNK, CHUNK), dtype=bool))
    seg = a_cs[..., :, None] - a_cs[..., None, :]
    L_mat = jnp.exp(jnp.where(causal, seg, -jnp.inf))
    scores = jnp.einsum('bclgn,bcsgn->bgcls', Cc, Bc)
    y_diag = jnp.einsum('bgcls,bgrcls,bcsgrp->bclgrp', scores, L_mat, x)
    decay_to_end = jnp.exp(a_cs[..., -1:] - a_cs)
    chunk_states = jnp.einsum('bclgn,bgrcl,bclgrp->bcgrpn', Bc, decay_to_end, x)
    chunk_decay = jnp.exp(a_cs[..., -1])

    def carry_fn(state, inp):
        st, dec = inp
        return state * dec[..., None, None] + st, state

    init = jnp.zeros((b, G, R, P, N), x.dtype)
    _, prev_states = lax.scan(carry_fn, init,
                              (jnp.moveaxis(chunk_states, 1, 0), jnp.moveaxis(chunk_decay, 3, 0)))
    y_off = jnp.einsum('bclgn,bgrcl,cbgrpn->bclgrp', Cc, jnp.exp(a_cs), prev_states)
    return (y_diag + y_off).reshape(b, L, H, P)


def mamba2_mix(p, conv_w, conv_b, dt_bias, a_log, d_skip, norm_g):
    b, L, _ = p.shape
    z, xbc, dt = jnp.split(p, [D_SSM, D_SSM + D_CONV], axis=-1)
    xbc = jax.nn.silu(causal_depthwise_conv(xbc, conv_w) + conv_b)
    xs, Bm, Cm = jnp.split(xbc, [D_SSM, D_SSM + SSM_GROUPS * SSM_STATE], axis=-1)
    dt = jax.nn.softplus(dt + dt_bias)
    A = -jnp.exp(a_log)
    xh = xs.reshape(b, L, SSM_HEADS, SSM_HEAD_DIM)
    y = ssd_scan(xh * dt[..., None], dt * A,
                 Bm.reshape(b, L, SSM_GROUPS, SSM_STATE), Cm.reshape(b, L, SSM_GROUPS, SSM_STATE))
    y = (y + d_skip[:, None] * xh).reshape(b, L, D_SSM)
    return rms_norm(y * jax.nn.silu(z), norm_g, SSM_NORM_EPS)


def setup_inputs(seed: int = 0) -> dict:
    key = jax.random.key(seed)
    ks = jax.random.split(key, 32)
    f32 = jnp.float32
    nrm = lambda k, shape, s: jax.random.normal(k, shape, f32) * s
    gain = lambda k, shape: 1.0 + 0.02 * jax.random.normal(k, shape, f32)
    Dd = DEPTH
    dt0 = jnp.exp(jax.random.uniform(ks[20], (Dd, SSM_HEADS), f32, np.log(1e-3), np.log(1e-1)))
    return {
        "x": nrm(ks[0], (BATCH, SEQ, D_MODEL), 1.0),
        "norm_ffn1": gain(ks[1], (Dd, D_MODEL)),
        "ffn1_w_gate": nrm(ks[2], (Dd, D_MODEL, D_FF), D_MODEL ** -0.5),
        "ffn1_w_up": nrm(ks[3], (Dd, D_MODEL, D_FF), D_MODEL ** -0.5),
        "ffn1_w_down": nrm(ks[4], (Dd, D_FF, D_MODEL), D_FF ** -0.5),
        "norm_mix": gain(ks[5], (Dd, D_MODEL)),
        "w_in": nrm(ks[6], (Dd, D_MODEL, D_IN), D_MODEL ** -0.5),
        "rwkv_mu": jax.random.uniform(ks[7], (Dd, D_IN_RWKV), f32),
        "rwkv_w0": jax.random.uniform(ks[8], (Dd, D_RWKV), f32, -6.0, 1.0),
        "rwkv_w2": nrm(ks[9], (Dd, DECAY_LORA, D_RWKV), 0.5 * DECAY_LORA ** -0.5),
        "rwkv_a0": nrm(ks[10], (Dd, D_RWKV), 0.1),
        "rwkv_a2": nrm(ks[11], (Dd, ICLR_LORA, D_RWKV), ICLR_LORA ** -0.5),
        "rwkv_g2": nrm(ks[12], (Dd, GATE_LORA, D_RWKV), GATE_LORA ** -0.5),
        "rwkv_k_k": 0.85 + 0.02 * jax.random.normal(ks[13], (Dd, D_RWKV), f32),
        "rwkv_k_a": gain(ks[14], (Dd, D_RWKV)),
        "rwkv_r_k": nrm(ks[15], (Dd, RWKV_HEADS, RWKV_HEAD_DIM), 0.1),
        "rwkv_gn_g": gain(ks[16], (Dd, D_RWKV)),
        "rwkv_gn_b": nrm(ks[17], (Dd, D_RWKV), 0.01),
        "ssm_conv_w": nrm(ks[18], (Dd, CONV_WIDTH, D_CONV), CONV_WIDTH ** -0.5),
        "ssm_conv_b": nrm(ks[19], (Dd, D_CONV), 0.01),
        "ssm_dt_bias": dt0 + jnp.log(-jnp.expm1(-dt0)),
        "ssm_a_log": jnp.log(jax.random.uniform(ks[21], (Dd, SSM_HEADS), f32, 1.0, 16.0)),
        "ssm_d": gain(ks[22], (Dd, SSM_HEADS)),
        "ssm_norm": gain(ks[23], (Dd, D_SSM)),
        "w_out": nrm(ks[24], (Dd, D_MIX, D_MODEL), D_MIX ** -0.5),
        "norm_ffn2": gain(ks[25], (Dd, D_MODEL)),
        "ffn2_w_gate": nrm(ks[26], (Dd, D_MODEL, D_FF), D_MODEL ** -0.5),
        "ffn2_w_up": nrm(ks[27], (Dd, D_MODEL, D_FF), D_MODEL ** -0.5),
        "ffn2_w_down": nrm(ks[28], (Dd, D_FF, D_MODEL), D_FF ** -0.5),
        "norm_final": gain(ks[29], (D_MODEL,)),
    }


def reference(x, norm_ffn1, ffn1_w_gate, ffn1_w_up, ffn1_w_down, norm_mix, w_in,
              rwkv_mu, rwkv_w0, rwkv_w2, rwkv_a0, rwkv_a2, rwkv_g2, rwkv_k_k, rwkv_k_a,
              rwkv_r_k, rwkv_gn_g, rwkv_gn_b, ssm_conv_w, ssm_conv_b, ssm_dt_bias,
              ssm_a_log, ssm_d, ssm_norm, w_out, norm_ffn2, ffn2_w_gate, ffn2_w_up,
              ffn2_w_down, norm_final):
    for i in range(DEPTH):
        h = rms_norm(x, norm_ffn1[i])
        x = x + 0.5 * swiglu_ffn(h, ffn1_w_gate[i], ffn1_w_up[i], ffn1_w_down[i])
        h = rms_norm(x, norm_mix[i])
        p = h @ w_in[i]
        y_rwkv = rwkv7_mix(p[..., :D_IN_RWKV], rwkv_mu[i], rwkv_w0[i], rwkv_w2[i], rwkv_a0[i],
                           rwkv_a2[i], rwkv_g2[i], rwkv_k_k[i], rwkv_k_a[i], rwkv_r_k[i],
                           rwkv_gn_g[i], rwkv_gn_b[i])
        y_ssm = mamba2_mix(p[..., D_IN_RWKV:], ssm_conv_w[i], ssm_conv_b[i], ssm_dt_bias[i],
                           ssm_a_log[i], ssm_d[i], ssm_norm[i])
        x = x + jnp.concatenate([y_rwkv, y_ssm], axis=-1) @ w_out[i]
        h = rms_norm(x, norm_ffn2[i])
        x = x + 0.5 * swiglu_ffn(h, ffn2_w_gate[i], ffn2_w_up[i], ffn2_w_down[i])
    return rms_norm(x, norm_final)
```

```python
import functools

import jax
import jax.numpy as jnp
from jax import lax
from jax.experimental import pallas as pl
from jax.experimental.pallas import tpu as pltpu

F32 = jnp.float32
BF16 = jnp.bfloat16

NORM_EPS = 1e-6
RWKV_GN_EPS = 64e-5
SSM_NORM_EPS = 1e-5

HEAD_DIM = 64
CHUNK = 64
BLOCK = 256
LANE = 128

NT_DIMS = (((1,), (1,)), ((), ()))
TN_DIMS = (((0,), (0,)), ((), ()))

VMEM_LIMIT = 56 * 1024 * 1024


def _dot(a, b):
    return jnp.dot(a, b, preferred_element_type=F32)


def _dot_nt(a, b):
    return lax.dot_general(a, b, NT_DIMS, preferred_element_type=F32)


def _dot_tn(a, b):
    return lax.dot_general(a, b, TN_DIMS, preferred_element_type=F32)


def _rms(x, g, eps):
    return x * lax.rsqrt(jnp.mean(x * x, axis=-1, keepdims=True) + eps) * g


def _split3_rows(x):
    hi = x.astype(BF16)
    r1 = x - hi.astype(F32)
    mid = r1.astype(BF16)
    lo = (r1 - mid.astype(F32)).astype(BF16)
    return jnp.concatenate([hi, mid, lo, jnp.zeros_like(hi)], axis=0)


def _head_sum(x, e_ref):
    hi = x.astype(BF16)
    lo = (x - hi.astype(F32)).astype(BF16)
    e = e_ref[...]
    return _dot(hi, e) + _dot(lo, e)


def _block_diag(x, bdmask):
    return jnp.where(bdmask, jnp.concatenate([x, x, x, x], axis=0), 0.0).astype(BF16)


def _swiglu_accumulate(h_scr, wg_ref, wu_ref, wd_ref, acc_scr):
    h = h_scr[...]
    gate = _dot(h, wg_ref[...])
    up = _dot(h, wu_ref[...])
    act = (gate * jax.nn.sigmoid(gate) * up).astype(BF16)
    acc_scr[...] += _dot(act, wd_ref[...])


def _ffn1_kernel(x_ref, g_ref, wg_ref, wu_ref, wd_ref, o_ref, h_scr, acc_scr):
    j = pl.program_id(1)

    @pl.when(j == 0)
    def _():
        h_scr[...] = _rms(x_ref[...], g_ref[...], NORM_EPS).astype(BF16)
        acc_scr[...] = jnp.zeros_like(acc_scr)

    _swiglu_accumulate(h_scr, wg_ref, wu_ref, wd_ref, acc_scr)

    @pl.when(j == pl.num_programs(1) - 1)
    def _():
        o_ref[...] = x_ref[...] + 0.5 * acc_scr[...]


def _ffn2_kernel(x_ref, yr_ref, ys_ref, wor_ref, wos_ref, g_ref, wg_ref, wu_ref, wd_ref, gf_ref,
                 o_ref, h_scr, acc_scr, x_scr):
    j = pl.program_id(1)

    @pl.when(j == 0)
    def _():
        x2 = (x_ref[...] + _dot(yr_ref[...].astype(BF16), wor_ref[...])
              + _dot(ys_ref[...].astype(BF16), wos_ref[...]))
        x_scr[...] = x2
        h_scr[...] = _rms(x2, g_ref[...], NORM_EPS).astype(BF16)
        acc_scr[...] = jnp.zeros_like(acc_scr)

    _swiglu_accumulate(h_scr, wg_ref, wu_ref, wd_ref, acc_scr)

    @pl.when(j == pl.num_programs(1) - 1)
    def _():
        o_ref[...] = _rms(x_scr[...] + 0.5 * acc_scr[...], gf_ref[...], NORM_EPS)


def _ffn_tiles(t, d_ff):
    tm = 512
    tf = d_ff // 2
    assert t % tm == 0 and d_ff % 2 == 0 and tf % LANE == 0
    return tm, tf


def _ffn1(x, g, wg, wu, wd):
    t, d = x.shape
    d_ff = wg.shape[1]
    tm, tf = _ffn_tiles(t, d_ff)
    row = lambda i, j: (i, 0)
    return pl.pallas_call(
        _ffn1_kernel,
        grid=(t // tm, d_ff // tf),
        in_specs=[
            pl.BlockSpec((tm, d), row),
            pl.BlockSpec((1, d), lambda i, j: (0, 0)),
            pl.BlockSpec((d, tf), lambda i, j: (0, j)),
            pl.BlockSpec((d, tf), lambda i, j: (0, j)),
            pl.BlockSpec((tf, d), lambda i, j: (j, 0)),
        ],
        out_specs=pl.BlockSpec((tm, d), row),
        out_shape=jax.ShapeDtypeStruct((t, d), F32),
        scratch_shapes=[pltpu.VMEM((tm, d), BF16), pltpu.VMEM((tm, d), F32)],
        compiler_params=pltpu.CompilerParams(
            dimension_semantics=("arbitrary", "arbitrary"), vmem_limit_bytes=VMEM_LIMIT),
        name="ffn1",
    )(x, g, wg, wu, wd)


def _ffn2(x, yr, ys, wor, wos, g, wg, wu, wd, gf):
    t, d = x.shape
    d_ff = wg.shape[1]
    dm = yr.shape[1]
    tm, tf = _ffn_tiles(t, d_ff)
    row = lambda i, j: (i, 0)
    const = lambda i, j: (0, 0)
    return pl.pallas_call(
        _ffn2_kernel,
        grid=(t // tm, d_ff // tf),
        in_specs=[
            pl.BlockSpec((tm, d), row),
            pl.BlockSpec((tm, dm), row),
            pl.BlockSpec((tm, dm), row),
            pl.BlockSpec((dm, d), const),
            pl.BlockSpec((dm, d), const),
            pl.BlockSpec((1, d), const),
            pl.BlockSpec((d, tf), lambda i, j: (0, j)),
            pl.BlockSpec((d, tf), lambda i, j: (0, j)),
            pl.BlockSpec((tf, d), lambda i, j: (j, 0)),
            pl.BlockSpec((1, d), const),
        ],
        out_specs=pl.BlockSpec((tm, d), row),
        out_shape=jax.ShapeDtypeStruct((t, d), F32),
        scratch_shapes=[pltpu.VMEM((tm, d), BF16), pltpu.VMEM((tm, d), F32), pltpu.VMEM((tm, d), F32)],
        compiler_params=pltpu.CompilerParams(
            dimension_semantics=("arbitrary", "arbitrary"), vmem_limit_bytes=VMEM_LIMIT),
        name="ffn2",
    )(x, yr, ys, wor, wos, g, wg, wu, wd, gf)


def _inproj_kernel(x_ref, g_ref, wr_ref, ws_ref, pr_ref, ps_ref):
    h = _rms(x_ref[...], g_ref[...], NORM_EPS).astype(BF16)
    pr_ref[...] = _dot(h, wr_ref[...])
    ps_ref[...] = _dot(h, ws_ref[...])


def _inproj(x, g, wr, ws):
    t, d = x.shape
    tm = 512
    assert t % tm == 0
    const = lambda i: (0, 0)
    return pl.pallas_call(
        _inproj_kernel,
        grid=(t // tm,),
        in_specs=[
            pl.BlockSpec((tm, d), lambda i: (i, 0)),
            pl.BlockSpec((1, d), const),
            pl.BlockSpec(wr.shape, const),
            pl.BlockSpec(ws.shape, const),
        ],
        out_specs=[pl.BlockSpec((tm, wr.shape[1]), lambda i: (i, 0)),
                   pl.BlockSpec((tm, ws.shape[1]), lambda i: (i, 0))],
        out_shape=[jax.ShapeDtypeStruct((t, wr.shape[1]), F32),
                   jax.ShapeDtypeStruct((t, ws.shape[1]), F32)],
        compiler_params=pltpu.CompilerParams(
            dimension_semantics=("arbitrary",), vmem_limit_bytes=VMEM_LIMIT),
        name="in_proj",
    )(x, g, wr, ws)


def _rwkv_kernel(p_ref, mu_ref, w0_ref, w2_ref, a0_ref, a2_ref, g2_ref, kk_ref, ka_ref, rk_ref,
                 gng_ref, gnb_ref, e_ref, tri4_ref, onesq_ref, bdm_ref, trs_ref, tri_ref,
                 o_ref,
                 last_scr, z_scr, r_scr, k_scr, v_scr, a_scr, b_scr, ld_scr, y_scr, bonus_scr, gate_scr):
    tl, dr = o_ref.shape

    @pl.when(pl.program_id(1) == 0)
    def _():
        last_scr[...] = jnp.zeros_like(last_scr)
        z_scr[...] = jnp.zeros_like(z_scr)

    p = p_ref[...]
    ext = jnp.concatenate([last_scr[...], p], axis=0)
    shifted = pltpu.roll(ext, 1, axis=0)[8:, :]
    last_scr[...] = p[tl - 8:, :]
    ps = p + (shifted - p) * mu_ref[...]

    r = ps[:, 0:dr]
    k = ps[:, dr:2 * dr]
    v = ps[:, 2 * dr:3 * dr]
    wd = ps[:, 3 * dr:3 * dr + LANE]
    ad = ps[:, 3 * dr + LANE:3 * dr + 2 * LANE]
    gd = ps[:, 3 * dr + 2 * LANE:3 * dr + 3 * LANE]

    wpre = w0_ref[...] + _dot(jnp.tanh(wd).astype(BF16), w2_ref[...])
    softplus_neg = jnp.maximum(-wpre, 0.0) + jnp.log1p(jnp.exp(-jnp.abs(wpre)))
    ld_scr[...] = -jnp.exp(-softplus_neg - 0.5)
    iclr = jax.nn.sigmoid(a0_ref[...] + _dot(ad.astype(BF16), a2_ref[...]))
    gate_scr[...] = _dot(jax.nn.sigmoid(gd).astype(BF16), g2_ref[...])

    kkv = k * kk_ref[...]
    kkn = kkv / jnp.maximum(jnp.sqrt(_head_sum(kkv * kkv, e_ref)), 1e-12)
    kmod = k * (1.0 + (iclr - 1.0) * ka_ref[...])
    r_scr[...] = r
    k_scr[...] = kmod
    v_scr[...] = v
    a_scr[...] = -kkn
    b_scr[...] = kkn * iclr
    bonus_scr[...] = _head_sum(r * kmod * rk_ref[...], e_ref) * v

    bdmask = bdm_ref[...] > 0.5
    strict = trs_ref[...] > 0.5
    incl = tri_ref[...] > 0.5
    bd = functools.partial(_block_diag, bdmask=bdmask)

    def chunk(c, carry):
        rows = pl.ds(pl.multiple_of(c * CHUNK, CHUNK), CHUNK)
        for q in range(dr // BLOCK):
            cols = slice(q * BLOCK, (q + 1) * BLOCK)
            ld_c = ld_scr[rows, cols]
            ld4 = _split3_rows(ld_c)
            cum = _dot(tri4_ref[...], ld4)
            lsum = _dot_tn(ld4, onesq_ref[...])
            a_c = a_scr[rows, cols]
            b_c = b_scr[rows, cols]
            r_c = r_scr[rows, cols]
            k_c = k_scr[rows, cols]
            v_c = v_scr[rows, cols]
            at = a_c * jnp.exp(cum - ld_c)
            rt = r_c * jnp.exp(cum)
            inv = jnp.exp(-cum)
            to_end = jnp.exp(cum[CHUNK - 1:CHUNK, :] - cum)
            ar = jnp.concatenate([at, rt], axis=0).astype(BF16)
            sb = _dot_nt(ar, bd(b_c * inv))
            sk = _dot_nt(ar, bd(k_c * inv))
            n_pow = jnp.where(strict, sb[0:CHUNK], 0.0)
            a_ak = jnp.where(strict, sk[0:CHUNK], 0.0)
            a_rb = jnp.where(incl, sb[CHUNK:], 0.0)
            a_rk = jnp.where(incl, sk[CHUNK:], 0.0)
            vbd = bd(v_c)
            x1 = at
            x2 = _dot(a_ak.astype(BF16), vbd)
            o0 = _dot(a_rk.astype(BF16), vbd)
            for i in range(6):
                rhs = [bd(x1), bd(x2)] + ([bd(n_pow)] if i < 5 else [])
                res = _dot(n_pow.astype(BF16), jnp.concatenate(rhs, axis=1))
                x1 = x1 + res[:, 0:BLOCK]
                x2 = x2 + res[:, BLOCK:2 * BLOCK]
                if i < 5:
                    n_pow = res[:, 2 * BLOCK:3 * BLOCK]
            z = z_scr[q]
            zb = z.astype(BF16)
            u = _dot(x1.astype(BF16), zb) + x2
            y_scr[rows, cols] = _dot(rt.astype(BF16), zb) + _dot(a_rb.astype(BF16), bd(u)) + o0
            vk = _dot_tn((k_c * to_end).astype(BF16), v_c.astype(BF16))
            bu = _dot_tn((b_c * to_end).astype(BF16), u.astype(BF16))
            z_scr[q] = jnp.exp(lsum) * z + jnp.where(bdmask, bu + vk, 0.0)
        return carry

    lax.fori_loop(0, tl // CHUNK, chunk, 0)

    y = y_scr[...]
    inv_n = 1.0 / HEAD_DIM
    mean = _head_sum(y, e_ref) * inv_n
    yc = y - mean
    var = _head_sum(yc * yc, e_ref) * inv_n
    yn = yc * lax.rsqrt(var + RWKV_GN_EPS) * gng_ref[...] + gnb_ref[...]
    o_ref[...] = (yn + bonus_scr[...]) * gate_scr[...]


def _rwkv(p, batch, params, consts):
    t, wp = p.shape
    dr = params["w0"].shape[1]
    seq = t // batch
    tl = 256
    assert seq % tl == 0 and dr % BLOCK == 0
    nl = seq // tl
    const = lambda b, l: (0, 0)
    names = ["mu", "w0", "w2", "a0", "a2", "g2", "k_k", "k_a", "r_k", "gn_g", "gn_b"]
    cnames = ["e", "tri4", "ones_sq", "bdm", "trs", "tri"]
    ops = [params[n] for n in names] + [consts[n] for n in cnames]
    tile = lambda: pltpu.VMEM((tl, dr), F32)
    return pl.pallas_call(
        _rwkv_kernel,
        grid=(batch, nl),
        in_specs=[pl.BlockSpec((tl, wp), lambda b, l: (b * nl + l, 0))]
        + [pl.BlockSpec(o.shape, const) for o in ops],
        out_specs=pl.BlockSpec((tl, dr), lambda b, l: (b * nl + l, 0)),
        out_shape=jax.ShapeDtypeStruct((t, dr), F32),
        scratch_shapes=[pltpu.VMEM((8, wp), F32), pltpu.VMEM((dr // BLOCK, BLOCK, BLOCK), F32)]
        + [tile() for _ in range(9)],
        compiler_params=pltpu.CompilerParams(
            dimension_semantics=("arbitrary", "arbitrary"), vmem_limit_bytes=VMEM_LIMIT),
        name="rwkv7",
    )(p, *ops)


def _ssd_kernel(p_ref, cw_ref, cb_ref, dtb_ref, alog_ref, dsk_ref, ng_ref,
                tri4_ref, ones_ref, onesn_ref, bdm_ref, tri_ref, triu_ref,
                o_ref,
                last_scr, st_scr, xs_scr, xdt_scr, a_scr, bc_scr, y_scr):
    tl, ds = o_ref.shape
    n_grp, _, d_state = st_scr.shape
    dc = last_scr.shape[1]

    @pl.when(pl.program_id(1) == 0)
    def _():
        last_scr[...] = jnp.zeros_like(last_scr)
        st_scr[...] = jnp.zeros_like(st_scr)

    xbc = p_ref[:, ds:ds + dc]
    ext = jnp.concatenate([last_scr[...], xbc], axis=0)
    last_scr[...] = xbc[tl - 8:, :]
    width = cw_ref.shape[0]
    conv = xbc * cw_ref[width - 1:width, :] + cb_ref[...]
    for i in range(1, width):
        conv = conv + pltpu.roll(ext, i, axis=0)[8:, :] * cw_ref[width - 1 - i:width - i, :]
    xc = conv * jax.nn.sigmoid(conv)
    xs = xc[:, 0:ds]
    dtr = p_ref[:, ds + dc:ds + dc + ds] + dtb_ref[...]
    dt = jnp.maximum(dtr, 0.0) + jnp.log1p(jnp.exp(-jnp.abs(dtr)))
    xs_scr[...] = xs
    xdt_scr[...] = xs * dt
    a_scr[...] = dt * (-jnp.exp(alog_ref[...]))
    bc_scr[...] = xc[:, ds:]

    bdmask = bdm_ref[...] > 0.5
    incl = tri_ref[...] > 0.5
    upper = triu_ref[...] > 0.5

    def chunk(c, carry):
        rows = pl.ds(pl.multiple_of(c * CHUNK, CHUNK), CHUNK)
        for g in range(n_grp):
            cols = slice(g * BLOCK, (g + 1) * BLOCK)
            a_c = a_scr[rows, cols]
            a4 = _split3_rows(a_c)
            e1 = _dot(tri4_ref[...], a4)
            e2 = _dot(ones_ref[...], _split3_rows(jnp.where(upper, a_c, 0.0)))
            l_w = jnp.exp(jnp.where(incl, e1 - e2, -jnp.inf))
            b_g = bc_scr[rows, g * d_state:(g + 1) * d_state].astype(BF16)
            c_g = bc_scr[rows, (n_grp + g) * d_state:(n_grp + g + 1) * d_state].astype(BF16)
            scores = _dot_nt(c_g, jnp.concatenate([b_g, b_g, b_g, b_g], axis=0))
            xdt_c = xdt_scr[rows, cols]
            y_diag = _dot((scores * l_w).astype(BF16), _block_diag(xdt_c, bdmask))
            st = st_scr[g]
            y_off = _dot_nt(c_g, st.astype(BF16)) * jnp.exp(e1)
            y_scr[rows, cols] = y_diag + y_off
            to_end = jnp.exp(e1[CHUNK - 1:CHUNK, :] - e1)
            new = _dot_tn((xdt_c * to_end).astype(BF16), b_g)
            st_scr[g] = st * jnp.exp(_dot_tn(a4, onesn_ref[...])) + new
        return carry

    lax.fori_loop(0, tl // CHUNK, chunk, 0)

    z = p_ref[:, 0:ds]
    y = (y_scr[...] + dsk_ref[...] * xs_scr[...]) * (z * jax.nn.sigmoid(z))
    o_ref[...] = _rms(y, ng_ref[...], SSM_NORM_EPS)


def _ssd(p, batch, params, consts, n_grp, d_state):
    t, wp = p.shape
    ds = params["norm_g"].shape[1]
    dc = params["conv_w"].shape[1]
    seq = t // batch
    tl = 256
    assert seq % tl == 0 and ds == n_grp * BLOCK
    nl = seq // tl
    const = lambda b, l: (0, 0)
    names = ["conv_w", "conv_b", "dt_bias", "a_log", "d_skip", "norm_g"]
    cnames = ["tri4", "ones", "ones_n", "bdm", "tri", "triu"]
    ops = [params[n] for n in names] + [consts[n] for n in cnames]
    tile = lambda: pltpu.VMEM((tl, ds), F32)
    return pl.pallas_call(
        _ssd_kernel,
        grid=(batch, nl),
        in_specs=[pl.BlockSpec((tl, wp), lambda b, l: (b * nl + l, 0))]
        + [pl.BlockSpec(o.shape, const) for o in ops],
        out_specs=pl.BlockSpec((tl, ds), lambda b, l: (b * nl + l, 0)),
        out_shape=jax.ShapeDtypeStruct((t, ds), F32),
        scratch_shapes=[pltpu.VMEM((8, dc), F32), pltpu.VMEM((n_grp, BLOCK, d_state), F32),
                        tile(), tile(), tile(), pltpu.VMEM((tl, 2 * n_grp * d_state), F32), tile()],
        compiler_params=pltpu.CompilerParams(
            dimension_semantics=("arbitrary", "arbitrary"), vmem_limit_bytes=VMEM_LIMIT),
        name="ssd",
    )(p, *ops)


def _mask_consts():
    row = jnp.arange(BLOCK)[:, None]
    col = jnp.arange(BLOCK)[None, :]
    t = jnp.arange(CHUNK)[:, None]
    s = col % CHUNK
    tri64 = (jnp.arange(CHUNK)[None, :] <= t)
    return {
        "bdm": (row // CHUNK == col // CHUNK).astype(F32),
        "trs": (s < t).astype(F32),
        "tri": (s <= t).astype(F32),
        "triu": (s >= t).astype(F32),
        "tri4": jnp.concatenate([tri64, tri64, tri64, jnp.zeros_like(tri64)], axis=1).astype(BF16),
        "ones": jnp.ones((CHUNK, BLOCK), BF16),
    }


def _pad_cols(w, width):
    return jnp.pad(w, ((0, 0), (0, width - w.shape[1])))


def kernel(x, norm_ffn1, ffn1_w_gate, ffn1_w_up, ffn1_w_down, norm_mix, w_in, rwkv_mu, rwkv_w0, rwkv_w2,
           rwkv_a0, rwkv_a2, rwkv_g2, rwkv_k_k, rwkv_k_a, rwkv_r_k, rwkv_gn_g, rwkv_gn_b, ssm_conv_w,
           ssm_conv_b, ssm_dt_bias, ssm_a_log, ssm_d, ssm_norm, w_out, norm_ffn2, ffn2_w_gate, ffn2_w_up,
           ffn2_w_down, norm_final):
    batch, seq, d = x.shape
    depth = norm_ffn1.shape[0]
    dr = rwkv_w0.shape[1]
    ds = ssm_norm.shape[1]
    dc = ssm_conv_w.shape[2]
    n_heads_s = ssm_a_log.shape[1]
    lora_w = rwkv_w2.shape[1]
    lora_a = rwkv_a2.shape[1]
    lora_g = rwkv_g2.shape[1]
    n_grp = ds // BLOCK
    d_state = (dc - ds) // (2 * n_grp)
    assert ds // n_heads_s == HEAD_DIM and lora_g == LANE and lora_w <= LANE and lora_a <= LANE

    consts = _mask_consts()
    consts["ones_sq"] = jnp.ones((BLOCK, BLOCK), BF16)
    consts["ones_n"] = jnp.ones((BLOCK, d_state), BF16)
    hid = jnp.arange(dr) // HEAD_DIM
    consts["e"] = (hid[:, None] == hid[None, :]).astype(BF16)
    row = lambda a: a.reshape(1, -1)
    rep = lambda a: jnp.repeat(a, HEAD_DIM).reshape(1, -1)

    xt = x.reshape(batch * seq, d)
    for i in range(depth):
        bf = lambda w: w[i].astype(BF16)
        x1 = _ffn1(xt, row(norm_ffn1[i]), bf(ffn1_w_gate), bf(ffn1_w_up), bf(ffn1_w_down))

        w = w_in[i]
        o_w, o_a, o_g, o_s = 3 * dr, 3 * dr + lora_w, 3 * dr + lora_w + lora_a, 3 * dr + lora_w + lora_a + lora_g
        w_r = jnp.concatenate([w[:, :o_w], _pad_cols(w[:, o_w:o_a], LANE), _pad_cols(w[:, o_a:o_g], LANE),
                               w[:, o_g:o_s]], axis=1).astype(BF16)
        w_s = jnp.concatenate([w[:, o_s:o_s + ds + dc],
                               jnp.repeat(w[:, o_s + ds + dc:], HEAD_DIM, axis=1)], axis=1).astype(BF16)
        mu = rwkv_mu[i]
        mu_p = jnp.concatenate([mu[:o_w], jnp.pad(mu[o_w:o_a], (0, LANE - lora_w)),
                                jnp.pad(mu[o_a:o_g], (0, LANE - lora_a)), mu[o_g:]]).reshape(1, -1)
        p_r, p_s = _inproj(x1, row(norm_mix[i]), w_r, w_s)

        rwkv_params = {
            "mu": mu_p, "w0": row(rwkv_w0[i]), "a0": row(rwkv_a0[i]),
            "w2": jnp.pad(rwkv_w2[i], ((0, LANE - lora_w), (0, 0))).astype(BF16),
            "a2": jnp.pad(rwkv_a2[i], ((0, LANE - lora_a), (0, 0))).astype(BF16),
            "g2": rwkv_g2[i].astype(BF16),
            "k_k": row(rwkv_k_k[i]), "k_a": row(rwkv_k_a[i]), "r_k": row(rwkv_r_k[i]),
            "gn_g": row(rwkv_gn_g[i]), "gn_b": row(rwkv_gn_b[i]),
        }
        y_r = _rwkv(p_r, batch, rwkv_params, consts)

        ssd_params = {
            "conv_w": ssm_conv_w[i], "conv_b": row(ssm_conv_b[i]), "dt_bias": rep(ssm_dt_bias[i]),
            "a_log": rep(ssm_a_log[i]), "d_skip": rep(ssm_d[i]), "norm_g": row(ssm_norm[i]),
        }
        y_s = _ssd(p_s, batch, ssd_params, consts, n_grp, d_state)

        wo = w_out[i].astype(BF16)
        last = i == depth - 1
        assert last, "the fused final norm assumes a single layer"
        xt = _ffn2(x1, y_r, y_s, wo[:dr], wo[dr:], row(norm_ffn2[i]), bf(ffn2_w_gate), bf(ffn2_w_up),
                   bf(ffn2_w_down), row(norm_final))
    return xt.reshape(batch, seq, d)
```

```python
import functools

import jax
import jax.numpy as jnp
from jax import lax
from jax.experimental import pallas as pl
from jax.experimental.pallas import tpu as pltpu

F32 = jnp.float32
BF16 = jnp.bfloat16

NORM_EPS = 1e-6
RWKV_GN_EPS = 64e-5
SSM_NORM_EPS = 1e-5

HEAD_DIM = 64
CHUNK = 64
BLOCK = 256
LANE = 128

NT_DIMS = (((1,), (1,)), ((), ()))
TN_DIMS = (((0,), (0,)), ((), ()))

VMEM_LIMIT = 56 * 1024 * 1024


def _dot(a, b):
    return jnp.dot(a, b, preferred_element_type=F32)


def _dot_nt(a, b):
    return lax.dot_general(a, b, NT_DIMS, preferred_element_type=F32)


def _dot_tn(a, b):
    return lax.dot_general(a, b, TN_DIMS, preferred_element_type=F32)


def _rms(x, g, eps):
    return x * lax.rsqrt(jnp.mean(x * x, axis=-1, keepdims=True) + eps) * g


def _split3_rows(x):
    hi = x.astype(BF16)
    r1 = x - hi.astype(F32)
    mid = r1.astype(BF16)
    lo = (r1 - mid.astype(F32)).astype(BF16)
    return jnp.concatenate([hi, mid, lo, jnp.zeros_like(hi)], axis=0)


def _head_sum(x, e_ref):
    return _dot(x.astype(BF16), e_ref[...])


def _block_diag(xb, bdmask):
    return jnp.where(bdmask, jnp.concatenate([xb, xb, xb, xb], axis=0), jnp.zeros((), BF16))


def _swiglu_accumulate(h_scr, wg_ref, wu_ref, wd_ref, acc_scr):
    h = h_scr[...]
    gate = _dot(h, wg_ref[...])
    up = _dot(h, wu_ref[...])
    act = (gate * jax.nn.sigmoid(gate) * up).astype(BF16)
    acc_scr[...] += _dot(act, wd_ref[...])


def _ffn1_kernel(x_ref, g_ref, wg_ref, wu_ref, wd_ref, o_ref, h_scr, acc_scr):
    j = pl.program_id(1)

    @pl.when(j == 0)
    def _():
        h_scr[...] = _rms(x_ref[...], g_ref[...], NORM_EPS).astype(BF16)
        acc_scr[...] = jnp.zeros_like(acc_scr)

    _swiglu_accumulate(h_scr, wg_ref, wu_ref, wd_ref, acc_scr)

    @pl.when(j == pl.num_programs(1) - 1)
    def _():
        o_ref[...] = x_ref[...] + 0.5 * acc_scr[...]


def _ffn2_kernel(x_ref, yr_ref, ys_ref, wor_ref, wos_ref, g_ref, wg_ref, wu_ref, wd_ref, gf_ref,
                 o_ref, h_scr, acc_scr, x_scr):
    j = pl.program_id(1)

    @pl.when(j == 0)
    def _():
        x2 = (x_ref[...] + _dot(yr_ref[...].astype(BF16), wor_ref[...])
              + _dot(ys_ref[...].astype(BF16), wos_ref[...]))
        x_scr[...] = x2
        h_scr[...] = _rms(x2, g_ref[...], NORM_EPS).astype(BF16)
        acc_scr[...] = jnp.zeros_like(acc_scr)

    _swiglu_accumulate(h_scr, wg_ref, wu_ref, wd_ref, acc_scr)

    @pl.when(j == pl.num_programs(1) - 1)
    def _():
        o_ref[...] = _rms(x_scr[...] + 0.5 * acc_scr[...], gf_ref[...], NORM_EPS)


def _ffn_tiles(t, d_ff):
    tm = 512
    tf = d_ff // 2
    assert t % tm == 0 and d_ff % 2 == 0 and tf % LANE == 0
    return tm, tf


def _ffn1(x, g, wg, wu, wd):
    t, d = x.shape
    d_ff = wg.shape[1]
    tm, tf = _ffn_tiles(t, d_ff)
    row = lambda i, j: (i, 0)
    return pl.pallas_call(
        _ffn1_kernel,
        grid=(t // tm, d_ff // tf),
        in_specs=[
            pl.BlockSpec((tm, d), row),
            pl.BlockSpec((1, d), lambda i, j: (0, 0)),
            pl.BlockSpec((d, tf), lambda i, j: (0, j)),
            pl.BlockSpec((d, tf), lambda i, j: (0, j)),
            pl.BlockSpec((tf, d), lambda i, j: (j, 0)),
        ],
        out_specs=pl.BlockSpec((tm, d), row),
        out_shape=jax.ShapeDtypeStruct((t, d), F32),
        scratch_shapes=[pltpu.VMEM((tm, d), BF16), pltpu.VMEM((tm, d), F32)],
        compiler_params=pltpu.CompilerParams(
            dimension_semantics=("arbitrary", "arbitrary"), vmem_limit_bytes=VMEM_LIMIT),
        name="ffn1",
    )(x, g, wg, wu, wd)


def _ffn2(x, yr, ys, wor, wos, g, wg, wu, wd, gf):
    t, d = x.shape
    d_ff = wg.shape[1]
    dm = yr.shape[1]
    tm, tf = _ffn_tiles(t, d_ff)
    row = lambda i, j: (i, 0)
    const = lambda i, j: (0, 0)
    return pl.pallas_call(
        _ffn2_kernel,
        grid=(t // tm, d_ff // tf),
        in_specs=[
            pl.BlockSpec((tm, d), row),
            pl.BlockSpec((tm, dm), row),
            pl.BlockSpec((tm, dm), row),
            pl.BlockSpec((dm, d), const),
            pl.BlockSpec((dm, d), const),
            pl.BlockSpec((1, d), const),
            pl.BlockSpec((d, tf), lambda i, j: (0, j)),
            pl.BlockSpec((d, tf), lambda i, j: (0, j)),
            pl.BlockSpec((tf, d), lambda i, j: (j, 0)),
            pl.BlockSpec((1, d), const),
        ],
        out_specs=pl.BlockSpec((tm, d), row),
        out_shape=jax.ShapeDtypeStruct((t, d), F32),
        scratch_shapes=[pltpu.VMEM((tm, d), BF16), pltpu.VMEM((tm, d), F32), pltpu.VMEM((tm, d), F32)],
        compiler_params=pltpu.CompilerParams(
            dimension_semantics=("arbitrary", "arbitrary"), vmem_limit_bytes=VMEM_LIMIT),
        name="ffn2",
    )(x, yr, ys, wor, wos, g, wg, wu, wd, gf)


def _inproj_kernel(x_ref, g_ref, wr_ref, ws_ref, pr_ref, ps_ref):
    h = _rms(x_ref[...], g_ref[...], NORM_EPS).astype(BF16)
    pr_ref[...] = _dot(h, wr_ref[...])
    ps_ref[...] = _dot(h, ws_ref[...])


def _inproj(x, g, wr, ws):
    t, d = x.shape
    tm = 512
    assert t % tm == 0
    const = lambda i: (0, 0)
    return pl.pallas_call(
        _inproj_kernel,
        grid=(t // tm,),
        in_specs=[
            pl.BlockSpec((tm, d), lambda i: (i, 0)),
            pl.BlockSpec((1, d), const),
            pl.BlockSpec(wr.shape, const),
            pl.BlockSpec(ws.shape, const),
        ],
        out_specs=[pl.BlockSpec((tm, wr.shape[1]), lambda i: (i, 0)),
                   pl.BlockSpec((tm, ws.shape[1]), lambda i: (i, 0))],
        out_shape=[jax.ShapeDtypeStruct((t, wr.shape[1]), F32),
                   jax.ShapeDtypeStruct((t, ws.shape[1]), F32)],
        compiler_params=pltpu.CompilerParams(
            dimension_semantics=("arbitrary",), vmem_limit_bytes=VMEM_LIMIT),
        name="in_proj",
    )(x, g, wr, ws)


def _rwkv_kernel(p_ref, mu_ref, w0_ref, w2_ref, a0_ref, a2_ref, g2_ref, kk_ref, ka_ref, rk_ref,
                 gng_ref, gnb_ref, e_ref, tri4_ref, onesq_ref, bdm_ref, bdm32_ref, trm_ref, eye_ref,
                 o_ref,
                 last_scr, z_scr, r_scr, k_scr, v_scr, a_scr, b_scr, ld_scr, y_scr, bonus_scr, gate_scr):
    tl, dr = o_ref.shape

    @pl.when(pl.program_id(1) == 0)
    def _():
        last_scr[...] = jnp.zeros_like(last_scr)
        z_scr[...] = jnp.zeros_like(z_scr)

    p = p_ref[...]
    ext = jnp.concatenate([last_scr[...], p], axis=0)
    shifted = pltpu.roll(ext, 1, axis=0)[8:, :]
    last_scr[...] = p[tl - 8:, :]
    ps = p + (shifted - p) * mu_ref[...]

    r = ps[:, 0:dr]
    k = ps[:, dr:2 * dr]
    v = ps[:, 2 * dr:3 * dr]
    wd = ps[:, 3 * dr:3 * dr + LANE]
    ad = ps[:, 3 * dr + LANE:3 * dr + 2 * LANE]
    gd = ps[:, 3 * dr + 2 * LANE:3 * dr + 3 * LANE]

    wpre = w0_ref[...] + _dot(jnp.tanh(wd).astype(BF16), w2_ref[...])
    softplus_neg = jnp.maximum(-wpre, 0.0) + jnp.log1p(jnp.exp(-jnp.abs(wpre)))
    ld_scr[...] = -jnp.exp(-softplus_neg - 0.5)
    iclr = jax.nn.sigmoid(a0_ref[...] + _dot(ad.astype(BF16), a2_ref[...]))
    gate_scr[...] = _dot(jax.nn.sigmoid(gd).astype(BF16), g2_ref[...])

    kkv = k * kk_ref[...]
    kkn = kkv / jnp.maximum(jnp.sqrt(_head_sum(kkv * kkv, e_ref)), 1e-12)
    kmod = k * (1.0 + (iclr - 1.0) * ka_ref[...])
    r_scr[...] = r
    k_scr[...] = kmod
    v_scr[...] = v
    a_scr[...] = -kkn
    b_scr[...] = kkn * iclr
    bonus_scr[...] = _head_sum(r * kmod * rk_ref[...], e_ref) * v

    bdmask = bdm_ref[...] > 0
    bdmask32 = bdm32_ref[...] > 0.5
    trmask = trm_ref[...] > 0.5
    bd = functools.partial(_block_diag, bdmask=bdmask)
    n_q = dr // BLOCK

    probs = [(slice(c * CHUNK, (c + 1) * CHUNK), slice(q * BLOCK, (q + 1) * BLOCK), q)
             for c in range(tl // CHUNK) for q in range(n_q)]
    each = lambda f, *lists: [f(*args) for args in zip(*lists)]

    ld = [ld_scr[rows, cols] for rows, cols, _ in probs]
    ld4 = each(_split3_rows, ld)
    cum = each(lambda x: _dot(tri4_ref[...], x), ld4)
    gcol = each(lambda x: jnp.exp(_dot_tn(x, onesq_ref[...])), ld4)
    b_c = [b_scr[rows, cols] for rows, cols, _ in probs]
    k_c = [k_scr[rows, cols] for rows, cols, _ in probs]
    vb = [v_scr[rows, cols].astype(BF16) for rows, cols, _ in probs]
    atb = [(a_scr[rows, cols] * jnp.exp(cm - l)).astype(BF16) for (rows, cols, _), cm, l in zip(probs, cum, ld)]
    rtb = [(r_scr[rows, cols] * jnp.exp(cm)).astype(BF16) for (rows, cols, _), cm in zip(probs, cum)]
    inv = each(lambda cm: jnp.exp(-cm), cum)
    to_end = each(lambda cm: jnp.exp(cm[CHUNK - 1:CHUNK, :] - cm), cum)
    ar = each(lambda a, r: jnp.concatenate([a, r], axis=0), atb, rtb)
    masked = lambda s: jnp.where(trmask, s, 0.0)
    sb32 = each(lambda a, b, i: masked(_dot_nt(a, bd((b * i).astype(BF16)))), ar, b_c, inv)
    sk = each(lambda a, k, i: masked(_dot_nt(a, bd((k * i).astype(BF16)))).astype(BF16), ar, k_c, inv)
    kv = each(lambda s, v: _dot(s, bd(v)), sk, vb)
    t_inv = each(lambda s: eye_ref[...] + s[0:CHUNK], sb32)
    pw = each(lambda s: s[0:CHUNK].astype(BF16), sb32)
    pw = each(lambda p: _dot(p, bd(p)).astype(BF16), pw)
    for i in range(1, 5):
        res = each(lambda t, p: _dot(jnp.concatenate([t.astype(BF16), p], axis=0), bd(p)), t_inv, pw)
        t_inv = each(lambda t, r: t + r[0:CHUNK], t_inv, res)
        pw = each(lambda r: r[CHUNK:].astype(BF16), res)
    t_inv = each(lambda t, p: t + _dot(t.astype(BF16), bd(p)), t_inv, pw)
    xx = each(lambda t, a, kvj: _dot(t.astype(BF16),
                                     jnp.concatenate([bd(a), bd(kvj[0:CHUNK].astype(BF16))], axis=1)),
              t_inv, atb, kv)
    x1b = each(lambda x: x[:, 0:BLOCK].astype(BF16), xx)
    x2 = each(lambda x: x[:, BLOCK:], xx)
    bgb = each(lambda b, e: (b * e).astype(BF16), b_c, to_end)
    kgb = each(lambda k, e: (k * e).astype(BF16), k_c, to_end)
    pmat = each(lambda b, x: jnp.where(bdmask32, _dot_tn(b, x), 0.0).astype(BF16), bgb, x1b)
    cmat = each(lambda b, k, x, v: jnp.where(
        bdmask32, _dot_tn(jnp.concatenate([b, k], axis=0), jnp.concatenate([x.astype(BF16), v], axis=0)), 0.0),
        bgb, kgb, x2, vb)

    zs = [z_scr[q] for q in range(n_q)]
    zb = []
    for (_, _, q), g, pm, cm in zip(probs, gcol, pmat, cmat):
        zb.append(zs[q].astype(BF16))
        zs[q] = g * zs[q] + _dot(pm, zb[-1]) + cm
    for q in range(n_q):
        z_scr[q] = zs[q]
    xz = each(lambda x, r, z: _dot(jnp.concatenate([x, r], axis=0), z), x1b, rtb, zb)
    ubd = each(lambda m, x: bd((m[0:CHUNK] + x).astype(BF16)), xz, x2)
    for (rows, cols, _), m, s, u, o in zip(probs, xz, sb32, ubd, kv):
        y_scr[rows, cols] = m[CHUNK:] + _dot(s[CHUNK:].astype(BF16), u) + o[CHUNK:]

    y = y_scr[...]
    inv_n = 1.0 / HEAD_DIM
    mean = _head_sum(y, e_ref) * inv_n
    yc = y - mean
    var = _head_sum(yc * yc, e_ref) * inv_n
    yn = yc * lax.rsqrt(var + RWKV_GN_EPS) * gng_ref[...] + gnb_ref[...]
    o_ref[...] = (yn + bonus_scr[...]) * gate_scr[...]


def _rwkv(p, batch, params, consts):
    t, wp = p.shape
    dr = params["w0"].shape[1]
    seq = t // batch
    tl = 256
    assert seq % tl == 0 and dr % BLOCK == 0
    nl = seq // tl
    const = lambda b, l: (0, 0)
    names = ["mu", "w0", "w2", "a0", "a2", "g2", "k_k", "k_a", "r_k", "gn_g", "gn_b"]
    cnames = ["e", "tri4", "ones_sq", "bdm", "bdm32", "trm", "eye"]
    ops = [params[n] for n in names] + [consts[n] for n in cnames]
    tile = lambda: pltpu.VMEM((tl, dr), F32)
    return pl.pallas_call(
        _rwkv_kernel,
        grid=(batch, nl),
        in_specs=[pl.BlockSpec((tl, wp), lambda b, l: (b * nl + l, 0))]
        + [pl.BlockSpec(o.shape, const) for o in ops],
        out_specs=pl.BlockSpec((tl, dr), lambda b, l: (b * nl + l, 0)),
        out_shape=jax.ShapeDtypeStruct((t, dr), F32),
        scratch_shapes=[pltpu.VMEM((8, wp), F32), pltpu.VMEM((dr // BLOCK, BLOCK, BLOCK), F32)]
        + [tile() for _ in range(9)],
        compiler_params=pltpu.CompilerParams(
            dimension_semantics=("arbitrary", "arbitrary"), vmem_limit_bytes=VMEM_LIMIT),
        name="rwkv7",
    )(p, *ops)


def _ssd_kernel(p_ref, cw_ref, cb_ref, dtb_ref, alog_ref, dsk_ref, ng_ref,
                tri4_ref, ones_ref, onesn_ref, bdm_ref, tri_ref, triu_ref,
                o_ref,
                last_scr, st_scr, xs_scr, xdt_scr, a_scr, bc_scr, y_scr):
    tl, ds = o_ref.shape
    n_grp, _, d_state = st_scr.shape
    dc = last_scr.shape[1]

    @pl.when(pl.program_id(1) == 0)
    def _():
        last_scr[...] = jnp.zeros_like(last_scr)
        st_scr[...] = jnp.zeros_like(st_scr)

    xbc = p_ref[:, ds:ds + dc]
    ext = jnp.concatenate([last_scr[...], xbc], axis=0)
    last_scr[...] = xbc[tl - 8:, :]
    width = cw_ref.shape[0]
    conv = xbc * cw_ref[width - 1:width, :] + cb_ref[...]
    for i in range(1, width):
        conv = conv + pltpu.roll(ext, i, axis=0)[8:, :] * cw_ref[width - 1 - i:width - i, :]
    xc = conv * jax.nn.sigmoid(conv)
    xs = xc[:, 0:ds]
    dtr = p_ref[:, ds + dc:ds + dc + ds] + dtb_ref[...]
    dt = jnp.maximum(dtr, 0.0) + jnp.log1p(jnp.exp(-jnp.abs(dtr)))
    xs_scr[...] = xs
    xdt_scr[...] = xs * dt
    a_scr[...] = dt * (-jnp.exp(alog_ref[...]))
    bc_scr[...] = xc[:, ds:]

    bdmask = bdm_ref[...] > 0
    incl = tri_ref[...] > 0.5
    upper = triu_ref[...] > 0.5

    probs = [(slice(c * CHUNK, (c + 1) * CHUNK), g) for c in range(tl // CHUNK) for g in range(n_grp)]
    each = lambda f, *lists: [f(*args) for args in zip(*lists)]
    gcols = lambda g: slice(g * BLOCK, (g + 1) * BLOCK)

    a_c = [a_scr[rows, gcols(g)] for rows, g in probs]
    a4 = each(_split3_rows, a_c)
    a4u = each(lambda a: _split3_rows(jnp.where(upper, a, 0.0)), a_c)
    e1 = each(lambda a: _dot(tri4_ref[...], a), a4)
    e2 = each(lambda a: _dot(ones_ref[...], a), a4u)
    decay = each(lambda a: jnp.exp(_dot_tn(a, onesn_ref[...])), a4)
    l_w = each(lambda x, y: jnp.exp(jnp.where(incl, x - y, -jnp.inf)), e1, e2)
    b_g = [bc_scr[rows, g * d_state:(g + 1) * d_state].astype(BF16) for rows, g in probs]
    c_g = [bc_scr[rows, (n_grp + g) * d_state:(n_grp + g + 1) * d_state].astype(BF16) for rows, g in probs]
    scores = each(lambda c, b: _dot_nt(c, jnp.concatenate([b, b, b, b], axis=0)), c_g, b_g)
    xdt_c = [xdt_scr[rows, gcols(g)] for rows, g in probs]
    y_diag = each(lambda s, l, x: _dot((s * l).astype(BF16), _block_diag(x.astype(BF16), bdmask)),
                  scores, l_w, xdt_c)
    new = each(lambda x, e, b: _dot_tn((x * jnp.exp(e[CHUNK - 1:CHUNK, :] - e)).astype(BF16), b),
               xdt_c, e1, b_g)

    sts = [st_scr[g] for g in range(n_grp)]
    stb = []
    for (_, g), d, n in zip(probs, decay, new):
        stb.append(sts[g].astype(BF16))
        sts[g] = sts[g] * d + n
    for g in range(n_grp):
        st_scr[g] = sts[g]
    for (rows, g), c, s, yd, e in zip(probs, c_g, stb, y_diag, e1):
        y_scr[rows, gcols(g)] = yd + _dot_nt(c, s) * jnp.exp(e)

    z = p_ref[:, 0:ds]
    y = (y_scr[...] + dsk_ref[...] * xs_scr[...]) * (z * jax.nn.sigmoid(z))
    o_ref[...] = _rms(y, ng_ref[...], SSM_NORM_EPS)


def _ssd(p, batch, params, consts, n_grp, d_state):
    t, wp = p.shape
    ds = params["norm_g"].shape[1]
    dc = params["conv_w"].shape[1]
    seq = t // batch
    tl = 256
    assert seq % tl == 0 and ds == n_grp * BLOCK
    nl = seq // tl
    const = lambda b, l: (0, 0)
    names = ["conv_w", "conv_b", "dt_bias", "a_log", "d_skip", "norm_g"]
    cnames = ["tri4", "ones", "ones_n", "bdm", "tri", "triu"]
    ops = [params[n] for n in names] + [consts[n] for n in cnames]
    tile = lambda: pltpu.VMEM((tl, ds), F32)
    return pl.pallas_call(
        _ssd_kernel,
        grid=(batch, nl),
        in_specs=[pl.BlockSpec((tl, wp), lambda b, l: (b * nl + l, 0))]
        + [pl.BlockSpec(o.shape, const) for o in ops],
        out_specs=pl.BlockSpec((tl, ds), lambda b, l: (b * nl + l, 0)),
        out_shape=jax.ShapeDtypeStruct((t, ds), F32),
        scratch_shapes=[pltpu.VMEM((8, dc), F32), pltpu.VMEM((n_grp, BLOCK, d_state), F32),
                        tile(), tile(), tile(), pltpu.VMEM((tl, 2 * n_grp * d_state), F32), tile()],
        compiler_params=pltpu.CompilerParams(
            dimension_semantics=("arbitrary", "arbitrary"), vmem_limit_bytes=VMEM_LIMIT),
        name="ssd",
    )(p, *ops)


def _mask_consts():
    row = jnp.arange(BLOCK)[:, None]
    col = jnp.arange(BLOCK)[None, :]
    t = jnp.arange(CHUNK)[:, None]
    s = col % CHUNK
    tri64 = (jnp.arange(CHUNK)[None, :] <= t)
    bdm = row // CHUNK == col // CHUNK
    return {
        "bdm": bdm.astype(BF16),
        "bdm32": bdm.astype(F32),
        "trm": jnp.concatenate([s < t, s <= t], axis=0).astype(F32),
        "tri": (s <= t).astype(F32),
        "triu": (s >= t).astype(F32),
        "eye": (s == t).astype(F32),
        "tri4": jnp.concatenate([tri64, tri64, tri64, jnp.zeros_like(tri64)], axis=1).astype(BF16),
        "ones": jnp.ones((CHUNK, BLOCK), BF16),
    }


def _pad_cols(w, width):
    return jnp.pad(w, ((0, 0), (0, width - w.shape[1])))


def kernel(x, norm_ffn1, ffn1_w_gate, ffn1_w_up, ffn1_w_down, norm_mix, w_in, rwkv_mu, rwkv_w0, rwkv_w2,
           rwkv_a0, rwkv_a2, rwkv_g2, rwkv_k_k, rwkv_k_a, rwkv_r_k, rwkv_gn_g, rwkv_gn_b, ssm_conv_w,
           ssm_conv_b, ssm_dt_bias, ssm_a_log, ssm_d, ssm_norm, w_out, norm_ffn2, ffn2_w_gate, ffn2_w_up,
           ffn2_w_down, norm_final):
    batch, seq, d = x.shape
    depth = norm_ffn1.shape[0]
    dr = rwkv_w0.shape[1]
    ds = ssm_norm.shape[1]
    dc = ssm_conv_w.shape[2]
    n_heads_s = ssm_a_log.shape[1]
    lora_w = rwkv_w2.shape[1]
    lora_a = rwkv_a2.shape[1]
    lora_g = rwkv_g2.shape[1]
    n_grp = ds // BLOCK
    d_state = (dc - ds) // (2 * n_grp)
    assert ds // n_heads_s == HEAD_DIM and lora_g == LANE and lora_w <= LANE and lora_a <= LANE

    consts = _mask_consts()
    consts["ones_sq"] = jnp.ones((BLOCK, BLOCK), BF16)
    consts["ones_n"] = jnp.ones((BLOCK, d_state), BF16)
    hid = jnp.arange(dr) // HEAD_DIM
    consts["e"] = (hid[:, None] == hid[None, :]).astype(BF16)
    row = lambda a: a.reshape(1, -1)
    rep = lambda a: jnp.repeat(a, HEAD_DIM).reshape(1, -1)

    xt = x.reshape(batch * seq, d)
    for i in range(depth):
        bf = lambda w: w[i].astype(BF16)
        x1 = _ffn1(xt, row(norm_ffn1[i]), bf(ffn1_w_gate), bf(ffn1_w_up), bf(ffn1_w_down))

        w = w_in[i]
        o_w, o_a, o_g, o_s = 3 * dr, 3 * dr + lora_w, 3 * dr + lora_w + lora_a, 3 * dr + lora_w + lora_a + lora_g
        w_r = jnp.concatenate([w[:, :o_w], _pad_cols(w[:, o_w:o_a], LANE), _pad_cols(w[:, o_a:o_g], LANE),
                               w[:, o_g:o_s]], axis=1).astype(BF16)
        w_s = jnp.concatenate([w[:, o_s:o_s + ds + dc],
                               jnp.repeat(w[:, o_s + ds + dc:], HEAD_DIM, axis=1)], axis=1).astype(BF16)
        mu = rwkv_mu[i]
        mu_p = jnp.concatenate([mu[:o_w], jnp.pad(mu[o_w:o_a], (0, LANE - lora_w)),
                                jnp.pad(mu[o_a:o_g], (0, LANE - lora_a)), mu[o_g:]]).reshape(1, -1)
        p_r, p_s = _inproj(x1, row(norm_mix[i]), w_r, w_s)

        rwkv_params = {
            "mu": mu_p, "w0": row(rwkv_w0[i]), "a0": row(rwkv_a0[i]),
            "w2": jnp.pad(rwkv_w2[i], ((0, LANE - lora_w), (0, 0))).astype(BF16),
            "a2": jnp.pad(rwkv_a2[i], ((0, LANE - lora_a), (0, 0))).astype(BF16),
            "g2": rwkv_g2[i].astype(BF16),
            "k_k": row(rwkv_k_k[i]), "k_a": row(rwkv_k_a[i]), "r_k": row(rwkv_r_k[i]),
            "gn_g": row(rwkv_gn_g[i]), "gn_b": row(rwkv_gn_b[i]),
        }
        y_r = _rwkv(p_r, batch, rwkv_params, consts)

        ssd_params = {
            "conv_w": ssm_conv_w[i], "conv_b": row(ssm_conv_b[i]), "dt_bias": rep(ssm_dt_bias[i]),
            "a_log": rep(ssm_a_log[i]), "d_skip": rep(ssm_d[i]), "norm_g": row(ssm_norm[i]),
        }
        y_s = _ssd(p_s, batch, ssd_params, consts, n_grp, d_state)

        wo = w_out[i].astype(BF16)
        last = i == depth - 1
        assert last, "the fused final norm assumes a single layer"
        xt = _ffn2(x1, y_r, y_s, wo[:dr], wo[dr:], row(norm_ffn2[i]), bf(ffn2_w_gate), bf(ffn2_w_up),
                   bf(ffn2_w_down), row(norm_final))
    return xt.reshape(batch, seq, d)
```

```python
import functools

import jax
import jax.numpy as jnp
from jax import lax
from jax.experimental import pallas as pl
from jax.experimental.pallas import tpu as pltpu

F32 = jnp.float32
BF16 = jnp.bfloat16

NORM_EPS = 1e-6
RWKV_GN_EPS = 64e-5
SSM_NORM_EPS = 1e-5

HEAD_DIM = 64
CHUNK = 64
BLOCK = 256
LANE = 128

NT_DIMS = (((1,), (1,)), ((), ()))
TN_DIMS = (((0,), (0,)), ((), ()))

VMEM_LIMIT = 56 * 1024 * 1024


def _dot(a, b):
    return jnp.dot(a, b, preferred_element_type=F32)


def _dot_nt(a, b):
    return lax.dot_general(a, b, NT_DIMS, preferred_element_type=F32)


def _dot_tn(a, b):
    return lax.dot_general(a, b, TN_DIMS, preferred_element_type=F32)


def _rms(x, g, eps):
    return x * lax.rsqrt(jnp.mean(x * x, axis=-1, keepdims=True) + eps) * g


def _split3_rows(x):
    hi = x.astype(BF16)
    r1 = x - hi.astype(F32)
    mid = r1.astype(BF16)
    lo = (r1 - mid.astype(F32)).astype(BF16)
    return jnp.concatenate([hi, mid, lo, jnp.zeros_like(hi)], axis=0)


def _head_sum(x, e_ref):
    return _dot(x.astype(BF16), e_ref[...])


def _block_diag(xb, bdmask):
    return jnp.where(bdmask, jnp.concatenate([xb, xb, xb, xb], axis=0), jnp.zeros((), BF16))


FF_CHUNK = 256
FFN_ROWS = 512


def _swiglu(h_scr, wg_ref, wu_ref, wd_ref, act_scr):
    d_ff = wg_ref.shape[1]
    for c in range(d_ff // FF_CHUNK):
        cs = slice(c * FF_CHUNK, (c + 1) * FF_CHUNK)
        gate = _dot(h_scr[...], wg_ref[:, cs])
        up = _dot(h_scr[...], wu_ref[:, cs])
        act_scr[:, cs] = (gate * jax.nn.sigmoid(gate) * up).astype(BF16)
    return _dot(act_scr[...], wd_ref[...])


def _ffn1_kernel(x_ref, g_ref, wg_ref, wu_ref, wd_ref, o_ref, h_scr, act_scr):
    h_scr[...] = _rms(x_ref[...], g_ref[...], NORM_EPS).astype(BF16)
    o_ref[...] = x_ref[...] + 0.5 * _swiglu(h_scr, wg_ref, wu_ref, wd_ref, act_scr)


def _ffn2_kernel(x_ref, yr_ref, ys_ref, wor_ref, wos_ref, g_ref, wg_ref, wu_ref, wd_ref, gf_ref,
                 o_ref, h_scr, act_scr, x_scr):
    x2 = (x_ref[...] + _dot(yr_ref[...].astype(BF16), wor_ref[...])
          + _dot(ys_ref[...].astype(BF16), wos_ref[...]))
    x_scr[...] = x2
    h_scr[...] = _rms(x2, g_ref[...], NORM_EPS).astype(BF16)
    y = _swiglu(h_scr, wg_ref, wu_ref, wd_ref, act_scr)
    o_ref[...] = _rms(x_scr[...] + 0.5 * y, gf_ref[...], NORM_EPS)


def _resident(shape):
    return pl.BlockSpec(shape, lambda i: (0,) * len(shape), pipeline_mode=pl.Buffered(1))


def _ffn1(x, g, wg, wu, wd):
    t, d = x.shape
    d_ff = wg.shape[1]
    tm = FFN_ROWS
    assert t % tm == 0 and d_ff % FF_CHUNK == 0
    row = lambda i: (i, 0)
    return pl.pallas_call(
        _ffn1_kernel,
        grid=(t // tm,),
        in_specs=[pl.BlockSpec((tm, d), row), _resident(g.shape), _resident(wg.shape), _resident(wu.shape),
                  _resident(wd.shape)],
        out_specs=pl.BlockSpec((tm, d), row),
        out_shape=jax.ShapeDtypeStruct((t, d), F32),
        scratch_shapes=[pltpu.VMEM((tm, d), BF16), pltpu.VMEM((tm, d_ff), BF16)],
        compiler_params=pltpu.CompilerParams(
            dimension_semantics=("arbitrary",), vmem_limit_bytes=VMEM_LIMIT),
        name="ffn1",
    )(x, g, wg, wu, wd)


def _ffn2(x, yr, ys, wor, wos, g, wg, wu, wd, gf):
    t, d = x.shape
    d_ff = wg.shape[1]
    dm = yr.shape[1]
    tm = FFN_ROWS
    assert t % tm == 0 and d_ff % FF_CHUNK == 0
    row = lambda i: (i, 0)
    return pl.pallas_call(
        _ffn2_kernel,
        grid=(t // tm,),
        in_specs=[pl.BlockSpec((tm, d), row), pl.BlockSpec((tm, dm), row), pl.BlockSpec((tm, dm), row),
                  _resident(wor.shape), _resident(wos.shape), _resident(g.shape), _resident(wg.shape),
                  _resident(wu.shape), _resident(wd.shape), _resident(gf.shape)],
        out_specs=pl.BlockSpec((tm, d), row),
        out_shape=jax.ShapeDtypeStruct((t, d), F32),
        scratch_shapes=[pltpu.VMEM((tm, d), BF16), pltpu.VMEM((tm, d_ff), BF16), pltpu.VMEM((tm, d), F32)],
        compiler_params=pltpu.CompilerParams(
            dimension_semantics=("arbitrary",), vmem_limit_bytes=VMEM_LIMIT),
        name="ffn2",
    )(x, yr, ys, wor, wos, g, wg, wu, wd, gf)


def _inproj_kernel(x_ref, g_ref, wr_ref, ws_ref, pr_ref, ps_ref):
    h = _rms(x_ref[...], g_ref[...], NORM_EPS).astype(BF16)
    pr_ref[...] = _dot(h, wr_ref[...])
    ps_ref[...] = _dot(h, ws_ref[...])


def _inproj(x, g, wr, ws):
    t, d = x.shape
    tm = 512
    assert t % tm == 0
    const = lambda i: (0, 0)
    return pl.pallas_call(
        _inproj_kernel,
        grid=(t // tm,),
        in_specs=[
            pl.BlockSpec((tm, d), lambda i: (i, 0)),
            pl.BlockSpec((1, d), const),
            pl.BlockSpec(wr.shape, const),
            pl.BlockSpec(ws.shape, const),
        ],
        out_specs=[pl.BlockSpec((tm, wr.shape[1]), lambda i: (i, 0)),
                   pl.BlockSpec((tm, ws.shape[1]), lambda i: (i, 0))],
        out_shape=[jax.ShapeDtypeStruct((t, wr.shape[1]), F32),
                   jax.ShapeDtypeStruct((t, ws.shape[1]), F32)],
        compiler_params=pltpu.CompilerParams(
            dimension_semantics=("arbitrary",), vmem_limit_bytes=VMEM_LIMIT),
        name="in_proj",
    )(x, g, wr, ws)


def _rwkv_kernel(p_ref, mu_ref, w0_ref, w2_ref, a0_ref, a2_ref, g2_ref, kk_ref, ka_ref, rk_ref,
                 gng_ref, gnb_ref, e_ref, tri4_ref, onesq_ref, bdm_ref, bdm32_ref, trm_ref, eye_ref,
                 o_ref,
                 last_scr, z_scr, r_scr, k_scr, v_scr, a_scr, b_scr, ld_scr, y_scr, bonus_scr, gate_scr):
    tl, dr = o_ref.shape

    @pl.when(pl.program_id(1) == 0)
    def _():
        last_scr[...] = jnp.zeros_like(last_scr)
        z_scr[...] = jnp.zeros_like(z_scr)

    p = p_ref[...]
    ext = jnp.concatenate([last_scr[...], p], axis=0)
    shifted = pltpu.roll(ext, 1, axis=0)[8:, :]
    last_scr[...] = p[tl - 8:, :]
    ps = p + (shifted - p) * mu_ref[...]

    r = ps[:, 0:dr]
    k = ps[:, dr:2 * dr]
    v = ps[:, 2 * dr:3 * dr]
    wd = ps[:, 3 * dr:3 * dr + LANE]
    ad = ps[:, 3 * dr + LANE:3 * dr + 2 * LANE]
    gd = ps[:, 3 * dr + 2 * LANE:3 * dr + 3 * LANE]

    wpre = w0_ref[...] + _dot(jnp.tanh(wd).astype(BF16), w2_ref[...])
    softplus_neg = jnp.maximum(-wpre, 0.0) + jnp.log1p(jnp.exp(-jnp.abs(wpre)))
    ld_scr[...] = -jnp.exp(-softplus_neg - 0.5)
    iclr = jax.nn.sigmoid(a0_ref[...] + _dot(ad.astype(BF16), a2_ref[...]))
    gate_scr[...] = _dot(jax.nn.sigmoid(gd).astype(BF16), g2_ref[...])

    kkv = k * kk_ref[...]
    kkn = kkv / jnp.maximum(jnp.sqrt(_head_sum(kkv * kkv, e_ref)), 1e-12)
    kmod = k * (1.0 + (iclr - 1.0) * ka_ref[...])
    r_scr[...] = r
    k_scr[...] = kmod
    v_scr[...] = v
    a_scr[...] = -kkn
    b_scr[...] = kkn * iclr
    bonus_scr[...] = _head_sum(r * kmod * rk_ref[...], e_ref) * v

    bdmask = bdm_ref[...] > 0
    bdmask32 = bdm32_ref[...] > 0.5
    trmask = trm_ref[...] > 0.5
    bd = functools.partial(_block_diag, bdmask=bdmask)
    n_q = dr // BLOCK

    probs = [(slice(c * CHUNK, (c + 1) * CHUNK), slice(q * BLOCK, (q + 1) * BLOCK), q)
             for c in range(tl // CHUNK) for q in range(n_q)]
    each = lambda f, *lists: [f(*args) for args in zip(*lists)]

    ld = [ld_scr[rows, cols] for rows, cols, _ in probs]
    ld4 = each(_split3_rows, ld)
    cum = each(lambda x: _dot(tri4_ref[...], x), ld4)
    gcol = each(lambda x: jnp.exp(_dot_tn(x, onesq_ref[...])), ld4)
    b_c = [b_scr[rows, cols] for rows, cols, _ in probs]
    k_c = [k_scr[rows, cols] for rows, cols, _ in probs]
    vb = [v_scr[rows, cols].astype(BF16) for rows, cols, _ in probs]
    atb = [(a_scr[rows, cols] * jnp.exp(cm - l)).astype(BF16) for (rows, cols, _), cm, l in zip(probs, cum, ld)]
    rtb = [(r_scr[rows, cols] * jnp.exp(cm)).astype(BF16) for (rows, cols, _), cm in zip(probs, cum)]
    inv = each(lambda cm: jnp.exp(-cm), cum)
    to_end = each(lambda cm: jnp.exp(cm[CHUNK - 1:CHUNK, :] - cm), cum)
    ar = each(lambda a, r: jnp.concatenate([a, r], axis=0), atb, rtb)
    masked = lambda s: jnp.where(trmask, s, 0.0)
    sb32 = each(lambda a, b, i: masked(_dot_nt(a, bd((b * i).astype(BF16)))), ar, b_c, inv)
    sk = each(lambda a, k, i: masked(_dot_nt(a, bd((k * i).astype(BF16)))).astype(BF16), ar, k_c, inv)
    kv = each(lambda s, v: _dot(s, bd(v)), sk, vb)
    t_inv = each(lambda s: eye_ref[...] + s[0:CHUNK], sb32)
    pw = each(lambda s: s[0:CHUNK].astype(BF16), sb32)
    pw = each(lambda p: _dot(p, bd(p)).astype(BF16), pw)
    for i in range(1, 5):
        res = each(lambda t, p: _dot(jnp.concatenate([t.astype(BF16), p], axis=0), bd(p)), t_inv, pw)
        t_inv = each(lambda t, r: t + r[0:CHUNK], t_inv, res)
        pw = each(lambda r: r[CHUNK:].astype(BF16), res)
    t_inv = each(lambda t, p: t + _dot(t.astype(BF16), bd(p)), t_inv, pw)
    xx = each(lambda t, a, kvj: _dot(t.astype(BF16),
                                     jnp.concatenate([bd(a), bd(kvj[0:CHUNK].astype(BF16))], axis=1)),
              t_inv, atb, kv)
    x1b = each(lambda x: x[:, 0:BLOCK].astype(BF16), xx)
    x2 = each(lambda x: x[:, BLOCK:], xx)
    bgb = each(lambda b, e: (b * e).astype(BF16), b_c, to_end)
    kgb = each(lambda k, e: (k * e).astype(BF16), k_c, to_end)
    pmat = each(lambda b, x: jnp.where(bdmask32, _dot_tn(b, x), 0.0).astype(BF16), bgb, x1b)
    cmat = each(lambda b, k, x, v: jnp.where(
        bdmask32, _dot_tn(jnp.concatenate([b, k], axis=0), jnp.concatenate([x.astype(BF16), v], axis=0)), 0.0),
        bgb, kgb, x2, vb)

    zs = [z_scr[q] for q in range(n_q)]
    zb = []
    for (_, _, q), g, pm, cm in zip(probs, gcol, pmat, cmat):
        zb.append(zs[q].astype(BF16))
        zs[q] = g * zs[q] + _dot(pm, zb[-1]) + cm
    for q in range(n_q):
        z_scr[q] = zs[q]
    xz = each(lambda x, r, z: _dot(jnp.concatenate([x, r], axis=0), z), x1b, rtb, zb)
    ubd = each(lambda m, x: bd((m[0:CHUNK] + x).astype(BF16)), xz, x2)
    for (rows, cols, _), m, s, u, o in zip(probs, xz, sb32, ubd, kv):
        y_scr[rows, cols] = m[CHUNK:] + _dot(s[CHUNK:].astype(BF16), u) + o[CHUNK:]

    y = y_scr[...]
    inv_n = 1.0 / HEAD_DIM
    mean = _head_sum(y, e_ref) * inv_n
    yc = y - mean
    var = _head_sum(yc * yc, e_ref) * inv_n
    yn = yc * lax.rsqrt(var + RWKV_GN_EPS) * gng_ref[...] + gnb_ref[...]
    o_ref[...] = (yn + bonus_scr[...]) * gate_scr[...]


def _rwkv(p, batch, params, consts):
    t, wp = p.shape
    dr = params["w0"].shape[1]
    seq = t // batch
    tl = 256
    assert seq % tl == 0 and dr % BLOCK == 0
    nl = seq // tl
    const = lambda b, l: (0, 0)
    names = ["mu", "w0", "w2", "a0", "a2", "g2", "k_k", "k_a", "r_k", "gn_g", "gn_b"]
    cnames = ["e", "tri4", "ones_sq", "bdm", "bdm32", "trm", "eye"]
    ops = [params[n] for n in names] + [consts[n] for n in cnames]
    tile = lambda: pltpu.VMEM((tl, dr), F32)
    return pl.pallas_call(
        _rwkv_kernel,
        grid=(batch, nl),
        in_specs=[pl.BlockSpec((tl, wp), lambda b, l: (b * nl + l, 0))]
        + [pl.BlockSpec(o.shape, const) for o in ops],
        out_specs=pl.BlockSpec((tl, dr), lambda b, l: (b * nl + l, 0)),
        out_shape=jax.ShapeDtypeStruct((t, dr), F32),
        scratch_shapes=[pltpu.VMEM((8, wp), F32), pltpu.VMEM((dr // BLOCK, BLOCK, BLOCK), F32)]
        + [tile() for _ in range(9)],
        compiler_params=pltpu.CompilerParams(
            dimension_semantics=("arbitrary", "arbitrary"), vmem_limit_bytes=VMEM_LIMIT),
        name="rwkv7",
    )(p, *ops)


def _ssd_kernel(p_ref, cw_ref, cb_ref, dtb_ref, alog_ref, dsk_ref, ng_ref,
                tri4_ref, ones_ref, onesn_ref, bdm_ref, tri_ref, triu_ref,
                o_ref,
                last_scr, st_scr, xs_scr, xdt_scr, a_scr, bc_scr, y_scr):
    tl, ds = o_ref.shape
    n_grp, _, d_state = st_scr.shape
    dc = last_scr.shape[1]

    @pl.when(pl.program_id(1) == 0)
    def _():
        last_scr[...] = jnp.zeros_like(last_scr)
        st_scr[...] = jnp.zeros_like(st_scr)

    xbc = p_ref[:, ds:ds + dc]
    ext = jnp.concatenate([last_scr[...], xbc], axis=0)
    last_scr[...] = xbc[tl - 8:, :]
    width = cw_ref.shape[0]
    conv = xbc * cw_ref[width - 1:width, :] + cb_ref[...]
    for i in range(1, width):
        conv = conv + pltpu.roll(ext, i, axis=0)[8:, :] * cw_ref[width - 1 - i:width - i, :]
    xc = conv * jax.nn.sigmoid(conv)
    xs = xc[:, 0:ds]
    dtr = p_ref[:, ds + dc:ds + dc + ds] + dtb_ref[...]
    dt = jnp.maximum(dtr, 0.0) + jnp.log1p(jnp.exp(-jnp.abs(dtr)))
    xs_scr[...] = xs
    xdt_scr[...] = xs * dt
    a_scr[...] = dt * (-jnp.exp(alog_ref[...]))
    bc_scr[...] = xc[:, ds:]

    bdmask = bdm_ref[...] > 0
    incl = tri_ref[...] > 0.5
    upper = triu_ref[...] > 0.5

    probs = [(slice(c * CHUNK, (c + 1) * CHUNK), g) for c in range(tl // CHUNK) for g in range(n_grp)]
    each = lambda f, *lists: [f(*args) for args in zip(*lists)]
    gcols = lambda g: slice(g * BLOCK, (g + 1) * BLOCK)

    a_c = [a_scr[rows, gcols(g)] for rows, g in probs]
    a4 = each(_split3_rows, a_c)
    a4u = each(lambda a: _split3_rows(jnp.where(upper, a, 0.0)), a_c)
    e1 = each(lambda a: _dot(tri4_ref[...], a), a4)
    e2 = each(lambda a: _dot(ones_ref[...], a), a4u)
    decay = each(lambda a: jnp.exp(_dot_tn(a, onesn_ref[...])), a4)
    l_w = each(lambda x, y: jnp.exp(jnp.where(incl, x - y, -jnp.inf)), e1, e2)
    b_g = [bc_scr[rows, g * d_state:(g + 1) * d_state].astype(BF16) for rows, g in probs]
    c_g = [bc_scr[rows, (n_grp + g) * d_state:(n_grp + g + 1) * d_state].astype(BF16) for rows, g in probs]
    scores = each(lambda c, b: _dot_nt(c, jnp.concatenate([b, b, b, b], axis=0)), c_g, b_g)
    xdt_c = [xdt_scr[rows, gcols(g)] for rows, g in probs]
    y_diag = each(lambda s, l, x: _dot((s * l).astype(BF16), _block_diag(x.astype(BF16), bdmask)),
                  scores, l_w, xdt_c)
    new = each(lambda x, e, b: _dot_tn((x * jnp.exp(e[CHUNK - 1:CHUNK, :] - e)).astype(BF16), b),
               xdt_c, e1, b_g)

    sts = [st_scr[g] for g in range(n_grp)]
    stb = []
    for (_, g), d, n in zip(probs, decay, new):
        stb.append(sts[g].astype(BF16))
        sts[g] = sts[g] * d + n
    for g in range(n_grp):
        st_scr[g] = sts[g]
    for (rows, g), c, s, yd, e in zip(probs, c_g, stb, y_diag, e1):
        y_scr[rows, gcols(g)] = yd + _dot_nt(c, s) * jnp.exp(e)

    z = p_ref[:, 0:ds]
    y = (y_scr[...] + dsk_ref[...] * xs_scr[...]) * (z * jax.nn.sigmoid(z))
    o_ref[...] = _rms(y, ng_ref[...], SSM_NORM_EPS)


def _ssd(p, batch, params, consts, n_grp, d_state):
    t, wp = p.shape
    ds = params["norm_g"].shape[1]
    dc = params["conv_w"].shape[1]
    seq = t // batch
    tl = 256
    assert seq % tl == 0 and ds == n_grp * BLOCK
    nl = seq // tl
    const = lambda b, l: (0, 0)
    names = ["conv_w", "conv_b", "dt_bias", "a_log", "d_skip", "norm_g"]
    cnames = ["tri4", "ones", "ones_n", "bdm", "tri", "triu"]
    ops = [params[n] for n in names] + [consts[n] for n in cnames]
    tile = lambda: pltpu.VMEM((tl, ds), F32)
    return pl.pallas_call(
        _ssd_kernel,
        grid=(batch, nl),
        in_specs=[pl.BlockSpec((tl, wp), lambda b, l: (b * nl + l, 0))]
        + [pl.BlockSpec(o.shape, const) for o in ops],
        out_specs=pl.BlockSpec((tl, ds), lambda b, l: (b * nl + l, 0)),
        out_shape=jax.ShapeDtypeStruct((t, ds), F32),
        scratch_shapes=[pltpu.VMEM((8, dc), F32), pltpu.VMEM((n_grp, BLOCK, d_state), F32),
                        tile(), tile(), tile(), pltpu.VMEM((tl, 2 * n_grp * d_state), F32), tile()],
        compiler_params=pltpu.CompilerParams(
            dimension_semantics=("arbitrary", "arbitrary"), vmem_limit_bytes=VMEM_LIMIT),
        name="ssd",
    )(p, *ops)


def _mask_consts():
    row = jnp.arange(BLOCK)[:, None]
    col = jnp.arange(BLOCK)[None, :]
    t = jnp.arange(CHUNK)[:, None]
    s = col % CHUNK
    tri64 = (jnp.arange(CHUNK)[None, :] <= t)
    bdm = row // CHUNK == col // CHUNK
    return {
        "bdm": bdm.astype(BF16),
        "bdm32": bdm.astype(F32),
        "trm": jnp.concatenate([s < t, s <= t], axis=0).astype(F32),
        "tri": (s <= t).astype(F32),
        "triu": (s >= t).astype(F32),
        "eye": (s == t).astype(F32),
        "tri4": jnp.concatenate([tri64, tri64, tri64, jnp.zeros_like(tri64)], axis=1).astype(BF16),
        "ones": jnp.ones((CHUNK, BLOCK), BF16),
    }


def _pad_cols(w, width):
    return jnp.pad(w, ((0, 0), (0, width - w.shape[1])))


def kernel(x, norm_ffn1, ffn1_w_gate, ffn1_w_up, ffn1_w_down, norm_mix, w_in, rwkv_mu, rwkv_w0, rwkv_w2,
           rwkv_a0, rwkv_a2, rwkv_g2, rwkv_k_k, rwkv_k_a, rwkv_r_k, rwkv_gn_g, rwkv_gn_b, ssm_conv_w,
           ssm_conv_b, ssm_dt_bias, ssm_a_log, ssm_d, ssm_norm, w_out, norm_ffn2, ffn2_w_gate, ffn2_w_up,
           ffn2_w_down, norm_final):
    batch, seq, d = x.shape
    depth = norm_ffn1.shape[0]
    dr = rwkv_w0.shape[1]
    ds = ssm_norm.shape[1]
    dc = ssm_conv_w.shape[2]
    n_heads_s = ssm_a_log.shape[1]
    lora_w = rwkv_w2.shape[1]
    lora_a = rwkv_a2.shape[1]
    lora_g = rwkv_g2.shape[1]
    n_grp = ds // BLOCK
    d_state = (dc - ds) // (2 * n_grp)
    assert ds // n_heads_s == HEAD_DIM and lora_g == LANE and lora_w <= LANE and lora_a <= LANE

    consts = _mask_consts()
    consts["ones_sq"] = jnp.ones((BLOCK, BLOCK), BF16)
    consts["ones_n"] = jnp.ones((BLOCK, d_state), BF16)
    hid = jnp.arange(dr) // HEAD_DIM
    consts["e"] = (hid[:, None] == hid[None, :]).astype(BF16)
    row = lambda a: a.reshape(1, -1)
    rep = lambda a: jnp.repeat(a, HEAD_DIM).reshape(1, -1)

    xt = x.reshape(batch * seq, d)
    for i in range(depth):
        bf = lambda w: w[i].astype(BF16)
        x1 = _ffn1(xt, row(norm_ffn1[i]), bf(ffn1_w_gate), bf(ffn1_w_up), bf(ffn1_w_down))

        w = w_in[i]
        o_w, o_a, o_g, o_s = 3 * dr, 3 * dr + lora_w, 3 * dr + lora_w + lora_a, 3 * dr + lora_w + lora_a + lora_g
        w_r = jnp.concatenate([w[:, :o_w], _pad_cols(w[:, o_w:o_a], LANE), _pad_cols(w[:, o_a:o_g], LANE),
                               w[:, o_g:o_s]], axis=1).astype(BF16)
        w_s = jnp.concatenate([w[:, o_s:o_s + ds + dc],
                               jnp.repeat(w[:, o_s + ds + dc:], HEAD_DIM, axis=1)], axis=1).astype(BF16)
        mu = rwkv_mu[i]
        mu_p = jnp.concatenate([mu[:o_w], jnp.pad(mu[o_w:o_a], (0, LANE - lora_w)),
                                jnp.pad(mu[o_a:o_g], (0, LANE - lora_a)), mu[o_g:]]).reshape(1, -1)
        p_r, p_s = _inproj(x1, row(norm_mix[i]), w_r, w_s)

        rwkv_params = {
            "mu": mu_p, "w0": row(rwkv_w0[i]), "a0": row(rwkv_a0[i]),
            "w2": jnp.pad(rwkv_w2[i], ((0, LANE - lora_w), (0, 0))).astype(BF16),
            "a2": jnp.pad(rwkv_a2[i], ((0, LANE - lora_a), (0, 0))).astype(BF16),
            "g2": rwkv_g2[i].astype(BF16),
            "k_k": row(rwkv_k_k[i]), "k_a": row(rwkv_k_a[i]), "r_k": row(rwkv_r_k[i]),
            "gn_g": row(rwkv_gn_g[i]), "gn_b": row(rwkv_gn_b[i]),
        }
        y_r = _rwkv(p_r, batch, rwkv_params, consts)

        ssd_params = {
            "conv_w": ssm_conv_w[i], "conv_b": row(ssm_conv_b[i]), "dt_bias": rep(ssm_dt_bias[i]),
            "a_log": rep(ssm_a_log[i]), "d_skip": rep(ssm_d[i]), "norm_g": row(ssm_norm[i]),
        }
        y_s = _ssd(p_s, batch, ssd_params, consts, n_grp, d_state)

        wo = w_out[i].astype(BF16)
        last = i == depth - 1
        assert last, "the fused final norm assumes a single layer"
        xt = _ffn2(x1, y_r, y_s, wo[:dr], wo[dr:], row(norm_ffn2[i]), bf(ffn2_w_gate), bf(ffn2_w_up),
                   bf(ffn2_w_down), row(norm_final))
    return xt.reshape(batch, seq, d)
```

```python
import functools

import jax
import jax.numpy as jnp
from jax import lax
from jax.experimental import pallas as pl
from jax.experimental.pallas import tpu as pltpu

F32 = jnp.float32
BF16 = jnp.bfloat16

NORM_EPS = 1e-6
RWKV_GN_EPS = 64e-5
SSM_NORM_EPS = 1e-5

HEAD_DIM = 64
CHUNK = 64
BLOCK = 256
LANE = 128
HALO = 8

NT_DIMS = (((1,), (1,)), ((), ()))
TN_DIMS = (((0,), (0,)), ((), ()))

VMEM_LIMIT = 56 * 1024 * 1024


def _dot(a, b):
    return jnp.dot(a, b, preferred_element_type=F32)


def _dot_nt(a, b):
    return lax.dot_general(a, b, NT_DIMS, preferred_element_type=F32)


def _dot_tn(a, b):
    return lax.dot_general(a, b, TN_DIMS, preferred_element_type=F32)


def _rms(x, g, eps):
    return x * lax.rsqrt(jnp.mean(x * x, axis=-1, keepdims=True) + eps) * g


def _split2_rows(x):
    hi = x.astype(BF16)
    lo = (x - hi.astype(F32)).astype(BF16)
    return jnp.concatenate([hi, lo], axis=0)


def _split2_cols(x):
    hi = x.astype(BF16)
    lo = (x - hi.astype(F32)).astype(BF16)
    return jnp.concatenate([hi, lo], axis=1)


def _head_sum(x, e_ref):
    return _dot(x.astype(BF16), e_ref[...])


def _block_diag(xb, bdm):
    return jnp.concatenate([xb, xb, xb, xb], axis=0) * bdm


def _each(f, *lists):
    return [f(*args) for args in zip(*lists)]


FF_CHUNK = 256
FFN_ROWS = 512


def _swiglu(h_scr, wg_ref, wu_ref, wd_ref, act_scr):
    d_ff = wg_ref.shape[1]
    for c in range(d_ff // FF_CHUNK):
        cs = slice(c * FF_CHUNK, (c + 1) * FF_CHUNK)
        gate = _dot(h_scr[...], wg_ref[:, cs])
        up = _dot(h_scr[...], wu_ref[:, cs])
        act_scr[:, cs] = (gate * jax.nn.sigmoid(gate) * up).astype(BF16)
    return _dot(act_scr[...], wd_ref[...])


def _ffn1_kernel(x_ref, g_ref, wg_ref, wu_ref, wd_ref, o_ref, h_scr, act_scr):
    h_scr[...] = _rms(x_ref[...], g_ref[...], NORM_EPS).astype(BF16)
    o_ref[...] = x_ref[...] + 0.5 * _swiglu(h_scr, wg_ref, wu_ref, wd_ref, act_scr)


def _ffn2_kernel(x_ref, yr_ref, ys_ref, wor_ref, wos_ref, g_ref, wg_ref, wu_ref, wd_ref, gf_ref,
                 o_ref, h_scr, act_scr, x_scr):
    x2 = (x_ref[...] + _dot(yr_ref[...].astype(BF16), wor_ref[...])
          + _dot(ys_ref[...].astype(BF16), wos_ref[...]))
    x_scr[...] = x2
    h_scr[...] = _rms(x2, g_ref[...], NORM_EPS).astype(BF16)
    y = _swiglu(h_scr, wg_ref, wu_ref, wd_ref, act_scr)
    o_ref[...] = _rms(x_scr[...] + 0.5 * y, gf_ref[...], NORM_EPS)


def _resident(shape, n_grid=1):
    zeros = (0,) * len(shape)
    return pl.BlockSpec(shape, lambda *_: zeros, pipeline_mode=pl.Buffered(1))


def _ffn1(x, g, wg, wu, wd):
    t, d = x.shape
    d_ff = wg.shape[1]
    tm = FFN_ROWS
    assert t % tm == 0 and d_ff % FF_CHUNK == 0
    row = lambda i: (i, 0)
    return pl.pallas_call(
        _ffn1_kernel,
        grid=(t // tm,),
        in_specs=[pl.BlockSpec((tm, d), row), _resident(g.shape), _resident(wg.shape), _resident(wu.shape),
                  _resident(wd.shape)],
        out_specs=pl.BlockSpec((tm, d), row),
        out_shape=jax.ShapeDtypeStruct((t, d), F32),
        scratch_shapes=[pltpu.VMEM((tm, d), BF16), pltpu.VMEM((tm, d_ff), BF16)],
        compiler_params=pltpu.CompilerParams(
            dimension_semantics=("arbitrary",), vmem_limit_bytes=VMEM_LIMIT),
        name="ffn1",
    )(x, g, wg, wu, wd)


def _ffn2(x, yr, ys, wor, wos, g, wg, wu, wd, gf):
    t, d = x.shape
    d_ff = wg.shape[1]
    dm = yr.shape[1]
    tm = FFN_ROWS
    assert t % tm == 0 and d_ff % FF_CHUNK == 0
    row = lambda i: (i, 0)
    return pl.pallas_call(
        _ffn2_kernel,
        grid=(t // tm,),
        in_specs=[pl.BlockSpec((tm, d), row), pl.BlockSpec((tm, dm), row), pl.BlockSpec((tm, dm), row),
                  _resident(wor.shape), _resident(wos.shape), _resident(g.shape), _resident(wg.shape),
                  _resident(wu.shape), _resident(wd.shape), _resident(gf.shape)],
        out_specs=pl.BlockSpec((tm, d), row),
        out_shape=jax.ShapeDtypeStruct((t, d), F32),
        scratch_shapes=[pltpu.VMEM((tm, d), BF16), pltpu.VMEM((tm, d_ff), BF16), pltpu.VMEM((tm, d), F32)],
        compiler_params=pltpu.CompilerParams(
            dimension_semantics=("arbitrary",), vmem_limit_bytes=VMEM_LIMIT),
        name="ffn2",
    )(x, yr, ys, wor, wos, g, wg, wu, wd, gf)


def _inproj_kernel(x_ref, g_ref, wr_ref, ws_ref, pr_ref, ps_ref):
    h = _rms(x_ref[...], g_ref[...], NORM_EPS).astype(BF16)
    pr_ref[...] = _dot(h, wr_ref[...])
    ps_ref[...] = _dot(h, ws_ref[...])


def _inproj(x, g, wr, ws):
    t, d = x.shape
    tm = 512
    assert t % tm == 0
    return pl.pallas_call(
        _inproj_kernel,
        grid=(t // tm,),
        in_specs=[pl.BlockSpec((tm, d), lambda i: (i, 0)), _resident(g.shape), _resident(wr.shape),
                  _resident(ws.shape)],
        out_specs=[pl.BlockSpec((tm, wr.shape[1]), lambda i: (i, 0)),
                   pl.BlockSpec((tm, ws.shape[1]), lambda i: (i, 0))],
        out_shape=[jax.ShapeDtypeStruct((t, wr.shape[1]), F32),
                   jax.ShapeDtypeStruct((t, ws.shape[1]), F32)],
        compiler_params=pltpu.CompilerParams(
            dimension_semantics=("arbitrary",), vmem_limit_bytes=VMEM_LIMIT),
        name="in_proj",
    )(x, g, wr, ws)


def _rwkv_kernel(p_ref, mu_ref, w0_ref, w2_ref, a0_ref, a2_ref, g2_ref, kk_ref, ka_ref, rk_ref,
                 gng_ref, gnb_ref, e_ref, tri2_ref, bdm_ref, bdm32_ref, trm_ref, eye_ref,
                 o_ref,
                 ext_scr, zt_scr, r_scr, k_scr, v_scr, a_scr, b_scr, ld_scr, y_scr, bonus_scr, gate_scr):
    tl, dr = o_ref.shape

    @pl.when(pl.program_id(1) == 0)
    def _():
        ext_scr[0:HALO, :] = jnp.zeros((HALO, ext_scr.shape[1]), F32)
        zt_scr[...] = jnp.zeros_like(zt_scr)

    p = p_ref[...]
    ext_scr[HALO:, :] = p
    shifted = ext_scr[HALO - 1:HALO - 1 + tl, :]
    ext_scr[0:HALO, :] = p[tl - HALO:, :]
    ps = p + (shifted - p) * mu_ref[...]

    r = ps[:, 0:dr]
    k = ps[:, dr:2 * dr]
    v = ps[:, 2 * dr:3 * dr]
    wd = ps[:, 3 * dr:3 * dr + LANE]
    ad = ps[:, 3 * dr + LANE:3 * dr + 2 * LANE]
    gd = ps[:, 3 * dr + 2 * LANE:3 * dr + 3 * LANE]

    wpre = w0_ref[...] + _dot(jnp.tanh(wd).astype(BF16), w2_ref[...])
    ld_scr[...] = jax.nn.sigmoid(wpre) * (-jnp.exp(-0.5))
    iclr = jax.nn.sigmoid(a0_ref[...] + _dot(ad.astype(BF16), a2_ref[...]))
    gate_scr[...] = _dot(jax.nn.sigmoid(gd).astype(BF16), g2_ref[...])

    kkv = k * kk_ref[...]
    kkn = kkv * lax.rsqrt(jnp.maximum(_head_sum(kkv * kkv, e_ref), 1e-24))
    kmod = k * (1.0 + (iclr - 1.0) * ka_ref[...])
    r_scr[...] = r
    k_scr[...] = kmod
    v_scr[...] = v
    a_scr[...] = -kkn
    b_scr[...] = kkn * iclr
    bonus_scr[...] = _head_sum(r * kmod * rk_ref[...], e_ref) * v

    bdmask32 = bdm32_ref[...] > 0.5
    trmask = trm_ref[...] > 0.5
    bd = lambda xb: _block_diag(xb, bdm_ref[...])
    n_q = dr // BLOCK

    probs = [(slice(c * CHUNK, (c + 1) * CHUNK), slice(q * BLOCK, (q + 1) * BLOCK), q)
             for c in range(tl // CHUNK) for q in range(n_q)]
    tile = lambda scr: [scr[rows, cols] for rows, cols, _ in probs]

    ld = tile(ld_scr)
    cum = _each(lambda x: _dot(tri2_ref[...], _split2_rows(x)), ld)
    b_c = tile(b_scr)
    k_c = tile(k_scr)
    vb = _each(lambda x: x.astype(BF16), tile(v_scr))
    atb = _each(lambda a, cm, l: (a * jnp.exp(cm - l)).astype(BF16), tile(a_scr), cum, ld)
    rtb = _each(lambda x, cm: (x * jnp.exp(cm)).astype(BF16), tile(r_scr), cum)
    inv = _each(lambda cm: jnp.exp(-cm), cum)
    g_end = _each(lambda cm: jnp.exp(cm[CHUNK - 1:CHUNK, :]), cum)
    to_end = _each(lambda cm: jnp.exp(cm[CHUNK - 1:CHUNK, :] - cm), cum)
    ar = _each(lambda a, x: jnp.concatenate([a, x], axis=0), atb, rtb)
    masked = lambda s: jnp.where(trmask, s, 0.0)
    sb32 = _each(lambda a, b, i: masked(_dot_nt(a, bd((b * i).astype(BF16)))), ar, b_c, inv)
    sk = _each(lambda a, x, i: masked(_dot_nt(a, bd((x * i).astype(BF16)))).astype(BF16), ar, k_c, inv)
    kv = _each(lambda s, x: _dot(s, bd(x)), sk, vb)
    t_inv = _each(lambda s: eye_ref[...] + s[0:CHUNK], sb32)
    pw = _each(lambda s: s[0:CHUNK].astype(BF16), sb32)
    pw = _each(lambda x: _dot(x, bd(x)).astype(BF16), pw)
    for i in range(1, 5):
        res = _each(lambda t, x: _dot(jnp.concatenate([t.astype(BF16), x], axis=0), bd(x)), t_inv, pw)
        t_inv = _each(lambda t, x: t + x[0:CHUNK], t_inv, res)
        pw = _each(lambda x: x[CHUNK:].astype(BF16), res)
    t_inv = _each(lambda t, x: t + _dot(t.astype(BF16), bd(x)), t_inv, pw)
    xx = _each(lambda t, a, x: _dot(t.astype(BF16),
                                    jnp.concatenate([bd(a), bd(x[0:CHUNK].astype(BF16))], axis=1)),
               t_inv, atb, kv)
    x1b = _each(lambda x: x[:, 0:BLOCK].astype(BF16), xx)
    x2 = _each(lambda x: x[:, BLOCK:], xx)
    bgb = _each(lambda b, e: (b * e).astype(BF16), b_c, to_end)
    kgb = _each(lambda x, e: (x * e).astype(BF16), k_c, to_end)
    p_t = _each(lambda x, b: _dot_tn(x, b).astype(BF16), x1b, bgb)
    c_t = _each(lambda x, v, b, kg: _dot_tn(jnp.concatenate([x.astype(BF16), v], axis=0),
                                            jnp.concatenate([b, kg], axis=0)), x2, vb, bgb, kgb)

    zs = [zt_scr[q] for q in range(n_q)]
    zb = []
    for (_, _, q), g, pt, ct in zip(probs, g_end, p_t, c_t):
        zb.append(zs[q].astype(BF16))
        zs[q] = zs[q] * g + jnp.where(bdmask32, _dot(zb[-1], pt) + ct, 0.0)
    for q in range(n_q):
        zt_scr[q] = zs[q]
    xz = _each(lambda x, rt, z: _dot_nt(jnp.concatenate([x, rt], axis=0), z), x1b, rtb, zb)
    ubd = _each(lambda m, x: bd((m[0:CHUNK] + x).astype(BF16)), xz, x2)
    for (rows, cols, _), m, s, u, o in zip(probs, xz, sb32, ubd, kv):
        y_scr[rows, cols] = m[CHUNK:] + _dot(s[CHUNK:].astype(BF16), u) + o[CHUNK:]

    y = y_scr[...]
    inv_n = 1.0 / HEAD_DIM
    mean = _head_sum(y, e_ref) * inv_n
    yc = y - mean
    var = _head_sum(yc * yc, e_ref) * inv_n
    yn = yc * lax.rsqrt(var + RWKV_GN_EPS) * gng_ref[...] + gnb_ref[...]
    o_ref[...] = (yn + bonus_scr[...]) * gate_scr[...]


def _rwkv(p, batch, params, consts):
    t, wp = p.shape
    dr = params["w0"].shape[1]
    seq = t // batch
    tl = 256
    assert seq % tl == 0 and dr % BLOCK == 0
    nl = seq // tl
    names = ["mu", "w0", "w2", "a0", "a2", "g2", "k_k", "k_a", "r_k", "gn_g", "gn_b"]
    cnames = ["e", "tri2", "bdm", "bdm32", "trm", "eye"]
    ops = [params[n] for n in names] + [consts[n] for n in cnames]
    tile = lambda: pltpu.VMEM((tl, dr), F32)
    return pl.pallas_call(
        _rwkv_kernel,
        grid=(batch, nl),
        in_specs=[pl.BlockSpec((tl, wp), lambda b, l: (b * nl + l, 0))] + [_resident(o.shape) for o in ops],
        out_specs=pl.BlockSpec((tl, dr), lambda b, l: (b * nl + l, 0)),
        out_shape=jax.ShapeDtypeStruct((t, dr), F32),
        scratch_shapes=[pltpu.VMEM((tl + HALO, wp), F32), pltpu.VMEM((dr // BLOCK, BLOCK, BLOCK), F32)]
        + [tile() for _ in range(9)],
        compiler_params=pltpu.CompilerParams(
            dimension_semantics=("arbitrary", "arbitrary"), vmem_limit_bytes=VMEM_LIMIT),
        name="rwkv7",
    )(p, *ops)


def _ssd_kernel(p_ref, cw_ref, cb_ref, dtb_ref, alog_ref, dsk_ref, ng_ref, exp_ref,
                tri2_ref, ones_ref, bdm_ref, tri_ref, triu_ref,
                o_ref,
                ext_scr, st_scr, xs_scr, xdt_scr, a_scr, bc_scr, y_scr):
    tl, ds = o_ref.shape
    n_grp, d_state, _ = st_scr.shape
    dc = ext_scr.shape[1]

    @pl.when(pl.program_id(1) == 0)
    def _():
        ext_scr[0:HALO, :] = jnp.zeros((HALO, dc), F32)
        st_scr[...] = jnp.zeros_like(st_scr)

    xbc = p_ref[:, ds:ds + dc]
    ext_scr[HALO:, :] = xbc
    width = cw_ref.shape[0]
    conv = xbc * cw_ref[width - 1:width, :] + cb_ref[...]
    for i in range(1, width):
        conv = conv + ext_scr[HALO - i:HALO - i + tl, :] * cw_ref[width - 1 - i:width - i, :]
    ext_scr[0:HALO, :] = xbc[tl - HALO:, :]
    xc = conv * jax.nn.sigmoid(conv)
    xs = xc[:, 0:ds]
    dtr = p_ref[:, ds + dc:ds + dc + LANE] + dtb_ref[...]
    dt_c = jnp.maximum(dtr, 0.0) + jnp.log1p(jnp.exp(-jnp.abs(dtr)))
    dt = _dot(_split2_cols(dt_c), exp_ref[...])
    xs_scr[...] = xs
    xdt_scr[...] = xs * dt
    a_scr[...] = dt * (-jnp.exp(alog_ref[...]))
    bc_scr[...] = xc[:, ds:]

    incl = tri_ref[...] > 0.5
    upper = triu_ref[...] > 0.5

    probs = [(slice(c * CHUNK, (c + 1) * CHUNK), g) for c in range(tl // CHUNK) for g in range(n_grp)]
    gcols = lambda g: slice(g * BLOCK, (g + 1) * BLOCK)

    a_c = [a_scr[rows, gcols(g)] for rows, g in probs]
    e1 = _each(lambda a: _dot(tri2_ref[...], _split2_rows(a)), a_c)
    e2 = _each(lambda a: _dot(ones_ref[...], _split2_rows(jnp.where(upper, a, 0.0))), a_c)
    l_w = _each(lambda x, y: jnp.exp(jnp.where(incl, x - y, -jnp.inf)), e1, e2)
    b_g = [bc_scr[rows, g * d_state:(g + 1) * d_state].astype(BF16) for rows, g in probs]
    c_g = [bc_scr[rows, (n_grp + g) * d_state:(n_grp + g + 1) * d_state].astype(BF16) for rows, g in probs]
    scores = _each(lambda c, b: _dot_nt(c, jnp.concatenate([b, b, b, b], axis=0)), c_g, b_g)
    xdt_c = [xdt_scr[rows, gcols(g)] for rows, g in probs]
    y_diag = _each(lambda s, l, x: _dot((s * l).astype(BF16), _block_diag(x.astype(BF16), bdm_ref[...])),
                   scores, l_w, xdt_c)
    new = _each(lambda b, x, e: _dot_tn(b, (x * jnp.exp(e[CHUNK - 1:CHUNK, :] - e)).astype(BF16)),
                b_g, xdt_c, e1)

    sts = [st_scr[g] for g in range(n_grp)]
    stb = []
    for (_, g), e, n in zip(probs, e1, new):
        stb.append(sts[g].astype(BF16))
        sts[g] = sts[g] * jnp.exp(e[CHUNK - 1:CHUNK, :]) + n
    for g in range(n_grp):
        st_scr[g] = sts[g]
    for (rows, g), c, s, yd, e in zip(probs, c_g, stb, y_diag, e1):
        y_scr[rows, gcols(g)] = yd + _dot(c, s) * jnp.exp(e)

    z = p_ref[:, 0:ds]
    y = (y_scr[...] + dsk_ref[...] * xs_scr[...]) * (z * jax.nn.sigmoid(z))
    o_ref[...] = _rms(y, ng_ref[...], SSM_NORM_EPS)


def _ssd(p, batch, params, consts, n_grp, d_state):
    t, wp = p.shape
    ds = params["norm_g"].shape[1]
    dc = params["conv_w"].shape[1]
    seq = t // batch
    tl = 256
    assert seq % tl == 0 and ds == n_grp * BLOCK and params["conv_w"].shape[0] <= HALO
    nl = seq // tl
    names = ["conv_w", "conv_b", "dt_bias", "a_log", "d_skip", "norm_g", "expand"]
    cnames = ["tri2", "ones", "bdm", "tri", "triu"]
    ops = [params[n] for n in names] + [consts[n] for n in cnames]
    tile = lambda: pltpu.VMEM((tl, ds), F32)
    return pl.pallas_call(
        _ssd_kernel,
        grid=(batch, nl),
        in_specs=[pl.BlockSpec((tl, wp), lambda b, l: (b * nl + l, 0))] + [_resident(o.shape) for o in ops],
        out_specs=pl.BlockSpec((tl, ds), lambda b, l: (b * nl + l, 0)),
        out_shape=jax.ShapeDtypeStruct((t, ds), F32),
        scratch_shapes=[pltpu.VMEM((tl + HALO, dc), F32), pltpu.VMEM((n_grp, d_state, BLOCK), F32),
                        tile(), tile(), tile(), pltpu.VMEM((tl, 2 * n_grp * d_state), F32), tile()],
        compiler_params=pltpu.CompilerParams(
            dimension_semantics=("arbitrary", "arbitrary"), vmem_limit_bytes=VMEM_LIMIT),
        name="ssd",
    )(p, *ops)


def _mask_consts():
    row = jnp.arange(BLOCK)[:, None]
    col = jnp.arange(BLOCK)[None, :]
    t = jnp.arange(CHUNK)[:, None]
    s = col % CHUNK
    tri64 = (jnp.arange(CHUNK)[None, :] <= t)
    bdm = row // CHUNK == col // CHUNK
    return {
        "bdm": bdm.astype(BF16),
        "bdm32": bdm.astype(F32),
        "trm": jnp.concatenate([s < t, s <= t], axis=0).astype(F32),
        "tri": (s <= t).astype(F32),
        "triu": (s >= t).astype(F32),
        "eye": (s == t).astype(F32),
        "tri2": jnp.concatenate([tri64, tri64], axis=1).astype(BF16),
        "ones": jnp.ones((CHUNK, 2 * CHUNK), BF16),
    }


def _pad_cols(w, width):
    return jnp.pad(w, ((0, 0), (0, width - w.shape[1])))


def kernel(x, norm_ffn1, ffn1_w_gate, ffn1_w_up, ffn1_w_down, norm_mix, w_in, rwkv_mu, rwkv_w0, rwkv_w2,
           rwkv_a0, rwkv_a2, rwkv_g2, rwkv_k_k, rwkv_k_a, rwkv_r_k, rwkv_gn_g, rwkv_gn_b, ssm_conv_w,
           ssm_conv_b, ssm_dt_bias, ssm_a_log, ssm_d, ssm_norm, w_out, norm_ffn2, ffn2_w_gate, ffn2_w_up,
           ffn2_w_down, norm_final):
    batch, seq, d = x.shape
    depth = norm_ffn1.shape[0]
    dr = rwkv_w0.shape[1]
    ds = ssm_norm.shape[1]
    dc = ssm_conv_w.shape[2]
    n_heads_s = ssm_a_log.shape[1]
    lora_w = rwkv_w2.shape[1]
    lora_a = rwkv_a2.shape[1]
    lora_g = rwkv_g2.shape[1]
    n_grp = ds // BLOCK
    d_state = (dc - ds) // (2 * n_grp)
    assert ds // n_heads_s == HEAD_DIM and lora_g == LANE and lora_w <= LANE and lora_a <= LANE
    assert n_heads_s <= LANE

    consts = _mask_consts()
    hid = jnp.arange(dr) // HEAD_DIM
    consts["e"] = (hid[:, None] == hid[None, :]).astype(BF16)
    lane_head = jnp.arange(LANE)[:, None] == (jnp.arange(ds) // HEAD_DIM)[None, :]
    expand = jnp.concatenate([lane_head, lane_head], axis=0).astype(BF16)
    row = lambda a: a.reshape(1, -1)
    rep = lambda a: jnp.repeat(a, HEAD_DIM).reshape(1, -1)

    xt = x.reshape(batch * seq, d)
    for i in range(depth):
        bf = lambda w: w[i].astype(BF16)
        x1 = _ffn1(xt, row(norm_ffn1[i]), bf(ffn1_w_gate), bf(ffn1_w_up), bf(ffn1_w_down))

        w = w_in[i]
        o_w, o_a, o_g, o_s = 3 * dr, 3 * dr + lora_w, 3 * dr + lora_w + lora_a, 3 * dr + lora_w + lora_a + lora_g
        w_r = jnp.concatenate([w[:, :o_w], _pad_cols(w[:, o_w:o_a], LANE), _pad_cols(w[:, o_a:o_g], LANE),
                               w[:, o_g:o_s]], axis=1).astype(BF16)
        w_s = jnp.concatenate([w[:, o_s:o_s + ds + dc], _pad_cols(w[:, o_s + ds + dc:], LANE)],
                              axis=1).astype(BF16)
        mu = rwkv_mu[i]
        mu_p = jnp.concatenate([mu[:o_w], jnp.pad(mu[o_w:o_a], (0, LANE - lora_w)),
                                jnp.pad(mu[o_a:o_g], (0, LANE - lora_a)), mu[o_g:]]).reshape(1, -1)
        p_r, p_s = _inproj(x1, row(norm_mix[i]), w_r, w_s)

        rwkv_params = {
            "mu": mu_p, "w0": row(rwkv_w0[i]), "a0": row(rwkv_a0[i]),
            "w2": jnp.pad(rwkv_w2[i], ((0, LANE - lora_w), (0, 0))).astype(BF16),
            "a2": jnp.pad(rwkv_a2[i], ((0, LANE - lora_a), (0, 0))).astype(BF16),
            "g2": rwkv_g2[i].astype(BF16),
            "k_k": row(rwkv_k_k[i]), "k_a": row(rwkv_k_a[i]), "r_k": row(rwkv_r_k[i]),
            "gn_g": row(rwkv_gn_g[i]), "gn_b": row(rwkv_gn_b[i]),
        }
        y_r = _rwkv(p_r, batch, rwkv_params, consts)

        ssd_params = {
            "conv_w": ssm_conv_w[i], "conv_b": row(ssm_conv_b[i]),
            "dt_bias": jnp.pad(ssm_dt_bias[i], (0, LANE - n_heads_s)).reshape(1, -1),
            "a_log": rep(ssm_a_log[i]), "d_skip": rep(ssm_d[i]), "norm_g": row(ssm_norm[i]),
            "expand": expand,
        }
        y_s = _ssd(p_s, batch, ssd_params, consts, n_grp, d_state)

        wo = w_out[i].astype(BF16)
        assert i == depth - 1, "the fused final norm assumes a single layer"
        xt = _ffn2(x1, y_r, y_s, wo[:dr], wo[dr:], row(norm_ffn2[i]), bf(ffn2_w_gate), bf(ffn2_w_up),
                   bf(ffn2_w_down), row(norm_final))
    return xt.reshape(batch, seq, d)
```

```python
import jax
import jax.numpy as jnp
from jax import lax
from jax.experimental import pallas as pl
from jax.experimental.pallas import tpu as pltpu

F32 = jnp.float32
BF16 = jnp.bfloat16

NORM_EPS = 1e-6
RWKV_GN_EPS = 64e-5
SSM_NORM_EPS = 1e-5

HEAD_DIM = 64
CHUNK = 64
BLOCK = 256
LANE = 128
HALO = 8
MIX_ROWS = 512

NT_DIMS = (((1,), (1,)), ((), ()))
TN_DIMS = (((0,), (0,)), ((), ()))

VMEM_LIMIT = 56 * 1024 * 1024


def _dot(a, b):
    return jnp.dot(a, b, preferred_element_type=F32)


def _dot_nt(a, b):
    return lax.dot_general(a, b, NT_DIMS, preferred_element_type=F32)


def _dot_tn(a, b):
    return lax.dot_general(a, b, TN_DIMS, preferred_element_type=F32)


def _rms(x, g, eps):
    return x * lax.rsqrt(jnp.mean(x * x, axis=-1, keepdims=True) + eps) * g


def _split2_rows(x):
    hi = x.astype(BF16)
    lo = (x - hi.astype(F32)).astype(BF16)
    return jnp.concatenate([hi, lo], axis=0)


def _split2_cols(x):
    hi = x.astype(BF16)
    lo = (x - hi.astype(F32)).astype(BF16)
    return jnp.concatenate([hi, lo], axis=1)


def _head_sum(x, bdm_ref):
    xb = x.astype(BF16)
    return jnp.concatenate([_dot(xb[:, c:c + BLOCK], bdm_ref[...]) for c in range(0, x.shape[1], BLOCK)],
                           axis=1)


def _block_diag(xb, bdm):
    return jnp.concatenate([xb, xb, xb, xb], axis=0) * bdm


def _each(f, *lists):
    return [f(*args) for args in zip(*lists)]


def _resident(shape):
    zeros = (0,) * len(shape)
    return pl.BlockSpec(shape, lambda *_: zeros, pipeline_mode=pl.Buffered(1))


FF_CHUNK = 256
FFN_ROWS = 512


def _swiglu(h_scr, wg_ref, wu_ref, wd_ref, act_scr):
    d_ff = wg_ref.shape[1]
    for c in range(d_ff // FF_CHUNK):
        cs = slice(c * FF_CHUNK, (c + 1) * FF_CHUNK)
        gate = _dot(h_scr[...], wg_ref[:, cs])
        up = _dot(h_scr[...], wu_ref[:, cs])
        act_scr[:, cs] = (gate * jax.nn.sigmoid(gate) * up).astype(BF16)
    return _dot(act_scr[...], wd_ref[...])


def _ffn1_kernel(x_ref, g_ref, wg_ref, wu_ref, wd_ref, o_ref, h_scr, act_scr):
    h_scr[...] = _rms(x_ref[...], g_ref[...], NORM_EPS).astype(BF16)
    o_ref[...] = x_ref[...] + 0.5 * _swiglu(h_scr, wg_ref, wu_ref, wd_ref, act_scr)


def _ffn2_kernel(x_ref, yr_ref, ys_ref, wor_ref, wos_ref, g_ref, wg_ref, wu_ref, wd_ref, gf_ref,
                 o_ref, h_scr, act_scr, x_scr):
    x2 = (x_ref[...] + _dot(yr_ref[...].astype(BF16), wor_ref[...])
          + _dot(ys_ref[...].astype(BF16), wos_ref[...]))
    x_scr[...] = x2
    h_scr[...] = _rms(x2, g_ref[...], NORM_EPS).astype(BF16)
    y = _swiglu(h_scr, wg_ref, wu_ref, wd_ref, act_scr)
    o_ref[...] = _rms(x_scr[...] + 0.5 * y, gf_ref[...], NORM_EPS)


def _ffn1(x, g, wg, wu, wd):
    t, d = x.shape
    d_ff = wg.shape[1]
    tm = FFN_ROWS
    assert t % tm == 0 and d_ff % FF_CHUNK == 0
    row = lambda i: (i, 0)
    return pl.pallas_call(
        _ffn1_kernel,
        grid=(t // tm,),
        in_specs=[pl.BlockSpec((tm, d), row), _resident(g.shape), _resident(wg.shape), _resident(wu.shape),
                  _resident(wd.shape)],
        out_specs=pl.BlockSpec((tm, d), row),
        out_shape=jax.ShapeDtypeStruct((t, d), F32),
        scratch_shapes=[pltpu.VMEM((tm, d), BF16), pltpu.VMEM((tm, d_ff), BF16)],
        compiler_params=pltpu.CompilerParams(
            dimension_semantics=("arbitrary",), vmem_limit_bytes=VMEM_LIMIT),
        name="ffn1",
    )(x, g, wg, wu, wd)


def _ffn2(x, yr, ys, wor, wos, g, wg, wu, wd, gf):
    t, d = x.shape
    d_ff = wg.shape[1]
    dm = yr.shape[1]
    tm = FFN_ROWS
    assert t % tm == 0 and d_ff % FF_CHUNK == 0
    row = lambda i: (i, 0)
    return pl.pallas_call(
        _ffn2_kernel,
        grid=(t // tm,),
        in_specs=[pl.BlockSpec((tm, d), row), pl.BlockSpec((tm, dm), row), pl.BlockSpec((tm, dm), row),
                  _resident(wor.shape), _resident(wos.shape), _resident(g.shape), _resident(wg.shape),
                  _resident(wu.shape), _resident(wd.shape), _resident(gf.shape)],
        out_specs=pl.BlockSpec((tm, d), row),
        out_shape=jax.ShapeDtypeStruct((t, d), F32),
        scratch_shapes=[pltpu.VMEM((tm, d), BF16), pltpu.VMEM((tm, d_ff), BF16), pltpu.VMEM((tm, d), F32)],
        compiler_params=pltpu.CompilerParams(
            dimension_semantics=("arbitrary",), vmem_limit_bytes=VMEM_LIMIT),
        name="ffn2",
    )(x, yr, ys, wor, wos, g, wg, wu, wd, gf)


def _inproj_kernel(x_ref, g_ref, w_ref, p_ref):
    p_ref[...] = _dot(_rms(x_ref[...], g_ref[...], NORM_EPS).astype(BF16), w_ref[...])


def _inproj(x, g, w):
    t, d = x.shape
    tm = 512
    assert t % tm == 0
    return pl.pallas_call(
        _inproj_kernel,
        grid=(t // tm,),
        in_specs=[pl.BlockSpec((tm, d), lambda i: (i, 0)), _resident(g.shape), _resident(w.shape)],
        out_specs=pl.BlockSpec((tm, w.shape[1]), lambda i: (i, 0)),
        out_shape=jax.ShapeDtypeStruct((t, w.shape[1]), F32),
        compiler_params=pltpu.CompilerParams(
            dimension_semantics=("arbitrary",), vmem_limit_bytes=VMEM_LIMIT),
        name="in_proj",
    )(x, g, w)


def _mixer_call(body, p, col_block, width_in, width_out, ops, scratch, batch, name):
    t = p.shape[0]
    tl = MIX_ROWS
    nl = t // batch // tl
    assert t == batch * nl * tl
    tile = lambda b, l: (b * nl + l, 0)
    return pl.pallas_call(
        body,
        grid=(batch, nl),
        in_specs=[pl.BlockSpec((tl, width_in), lambda b, l: (b * nl + l, col_block))]
        + [_resident(o.shape) for o in ops],
        out_specs=pl.BlockSpec((tl, width_out), tile),
        out_shape=jax.ShapeDtypeStruct((t, width_out), F32),
        scratch_shapes=scratch,
        compiler_params=pltpu.CompilerParams(
            dimension_semantics=("arbitrary", "arbitrary"), vmem_limit_bytes=VMEM_LIMIT),
        name=name,
    )(p, *ops)


def _rwkv_kernel(p_ref, mu_ref, w0_ref, w2_ref, a0_ref, a2_ref, g2_ref, kk_ref, ka_ref, rk_ref,
                 gng_ref, gnb_ref, tri2_ref, bdm_ref, bdm32_ref, trm_ref, eye_ref,
                 o_ref,
                 ext_scr, zt_scr, r_scr, k_scr, v_scr, a_scr, b_scr, ld_scr, y_scr, bonus_scr, gate_scr):
    tl, dr = o_ref.shape

    @pl.when(pl.program_id(1) == 0)
    def _():
        ext_scr[0:HALO, :] = jnp.zeros((HALO, ext_scr.shape[1]), F32)
        zt_scr[...] = jnp.zeros_like(zt_scr)

    p = p_ref[...]
    ext_scr[HALO:, :] = p
    shifted = ext_scr[HALO - 1:HALO - 1 + tl, :]
    ext_scr[0:HALO, :] = p[tl - HALO:, :]
    ps = p + (shifted - p) * mu_ref[...]

    r = ps[:, 0:dr]
    k = ps[:, dr:2 * dr]
    v = ps[:, 2 * dr:3 * dr]
    wa = ps[:, 3 * dr:3 * dr + LANE]
    gd = ps[:, 3 * dr + LANE:3 * dr + 2 * LANE]

    wpre = w0_ref[...] + _dot(jnp.tanh(wa).astype(BF16), w2_ref[...])
    ld_scr[...] = jax.nn.sigmoid(wpre) * (-jnp.exp(-0.5))
    iclr = jax.nn.sigmoid(a0_ref[...] + _dot(wa.astype(BF16), a2_ref[...]))
    gate_scr[...] = _dot(jax.nn.sigmoid(gd).astype(BF16), g2_ref[...])

    kkv = k * kk_ref[...]
    kkn = kkv * lax.rsqrt(jnp.maximum(_head_sum(kkv * kkv, bdm_ref), 1e-24))
    kmod = k * (1.0 + (iclr - 1.0) * ka_ref[...])
    r_scr[...] = r
    k_scr[...] = kmod
    v_scr[...] = v
    a_scr[...] = -kkn
    b_scr[...] = kkn * iclr
    bonus_scr[...] = _head_sum(r * kmod * rk_ref[...], bdm_ref) * v

    bdmask32 = bdm32_ref[...] > 0.5
    trmask = trm_ref[...] > 0.5
    bd = lambda xb: _block_diag(xb, bdm_ref[...])
    n_q = dr // BLOCK

    probs = [(slice(c * CHUNK, (c + 1) * CHUNK), slice(q * BLOCK, (q + 1) * BLOCK), q)
             for c in range(tl // CHUNK) for q in range(n_q)]
    tile = lambda scr: [scr[rows, cols] for rows, cols, _ in probs]

    ld = tile(ld_scr)
    cum = _each(lambda x: _dot(tri2_ref[...], _split2_rows(x)), ld)
    b_c = tile(b_scr)
    k_c = tile(k_scr)
    vb = _each(lambda x: x.astype(BF16), tile(v_scr))
    atb = _each(lambda a, cm, l: (a * jnp.exp(cm - l)).astype(BF16), tile(a_scr), cum, ld)
    rtb = _each(lambda x, cm: (x * jnp.exp(cm)).astype(BF16), tile(r_scr), cum)
    inv = _each(lambda cm: jnp.exp(-cm), cum)
    g_end = _each(lambda cm: jnp.exp(cm[CHUNK - 1:CHUNK, :]), cum)
    to_end = _each(lambda cm: jnp.exp(cm[CHUNK - 1:CHUNK, :] - cm), cum)
    ar = _each(lambda a, x: jnp.concatenate([a, x], axis=0), atb, rtb)
    masked = lambda x: jnp.where(trmask, x, 0.0)
    sb32 = _each(lambda a, b, i: masked(_dot_nt(a, bd((b * i).astype(BF16)))), ar, b_c, inv)
    sk = _each(lambda a, x, i: masked(_dot_nt(a, bd((x * i).astype(BF16)))).astype(BF16), ar, k_c, inv)
    kv = _each(lambda x, y: _dot(x, bd(y)), sk, vb)
    t_inv = _each(lambda x: eye_ref[...] + x[0:CHUNK], sb32)
    pw = _each(lambda x: x[0:CHUNK].astype(BF16), sb32)
    pw = _each(lambda x: _dot(x, bd(x)).astype(BF16), pw)
    for i in range(1, 5):
        res = _each(lambda t, x: _dot(jnp.concatenate([t.astype(BF16), x], axis=0), bd(x)), t_inv, pw)
        t_inv = _each(lambda t, x: t + x[0:CHUNK], t_inv, res)
        pw = _each(lambda x: x[CHUNK:].astype(BF16), res)
    t_inv = _each(lambda t, x: t + _dot(t.astype(BF16), bd(x)), t_inv, pw)
    xx = _each(lambda t, a, x: _dot(t.astype(BF16),
                                    jnp.concatenate([bd(a), bd(x[0:CHUNK].astype(BF16))], axis=1)),
               t_inv, atb, kv)
    x1b = _each(lambda x: x[:, 0:BLOCK].astype(BF16), xx)
    x2 = _each(lambda x: x[:, BLOCK:], xx)
    bgb = _each(lambda b, e: (b * e).astype(BF16), b_c, to_end)
    kgb = _each(lambda x, e: (x * e).astype(BF16), k_c, to_end)
    p_t = _each(lambda x, b: _dot_tn(x, b).astype(BF16), x1b, bgb)
    c_t = _each(lambda x, y, b, kg: _dot_tn(jnp.concatenate([x.astype(BF16), y], axis=0),
                                            jnp.concatenate([b, kg], axis=0)), x2, vb, bgb, kgb)

    zs = [zt_scr[q] for q in range(n_q)]
    zb = []
    for (_, _, q), g, pt, ct in zip(probs, g_end, p_t, c_t):
        zb.append(zs[q].astype(BF16))
        zs[q] = zs[q] * g + jnp.where(bdmask32, _dot(zb[-1], pt) + ct, 0.0)
    for q in range(n_q):
        zt_scr[q] = zs[q]
    xz = _each(lambda x, rt, z: _dot_nt(jnp.concatenate([x, rt], axis=0), z), x1b, rtb, zb)
    ubd = _each(lambda m, x: bd((m[0:CHUNK] + x).astype(BF16)), xz, x2)
    for (rows, cols, _), m, n_rb, u, o in zip(probs, xz, sb32, ubd, kv):
        y_scr[rows, cols] = m[CHUNK:] + _dot(n_rb[CHUNK:].astype(BF16), u) + o[CHUNK:]

    y = y_scr[...]
    inv_n = 1.0 / HEAD_DIM
    mean = _head_sum(y, bdm_ref) * inv_n
    yc = y - mean
    var = _head_sum(yc * yc, bdm_ref) * inv_n
    yn = yc * lax.rsqrt(var + RWKV_GN_EPS) * gng_ref[...] + gnb_ref[...]
    o_ref[...] = (yn + bonus_scr[...]) * gate_scr[...]


def _rwkv(p, batch, params, consts):
    dr = params["w0"].shape[1]
    wp = 3 * dr + 2 * LANE
    assert dr % BLOCK == 0
    names = ["mu", "w0", "w2", "a0", "a2", "g2", "k_k", "k_a", "r_k", "gn_g", "gn_b"]
    cnames = ["tri2", "bdm", "bdm32", "trm", "eye"]
    ops = [params[n] for n in names] + [consts[n] for n in cnames]
    tl = MIX_ROWS
    scratch = ([pltpu.VMEM((tl + HALO, wp), F32), pltpu.VMEM((dr // BLOCK, BLOCK, BLOCK), F32)]
               + [pltpu.VMEM((tl, dr), F32) for _ in range(9)])
    return _mixer_call(_rwkv_kernel, p, 0, wp, dr, ops, scratch, batch, "rwkv7")


def _ssd_kernel(p_ref, cw_ref, cb_ref, dtb_ref, alog_ref, dsk_ref, ng_ref, exp_ref,
                tri2_ref, ones_ref, bdm_ref, tri_ref, triu_ref,
                o_ref,
                ext_scr, st_scr, xs_scr, xdt_scr, a_scr, bc_scr, y_scr):
    tl, ds = o_ref.shape
    n_grp, d_state, _ = st_scr.shape
    dc = ext_scr.shape[1]

    @pl.when(pl.program_id(1) == 0)
    def _():
        ext_scr[0:HALO, :] = jnp.zeros((HALO, dc), F32)
        st_scr[...] = jnp.zeros_like(st_scr)

    xbc = p_ref[:, ds:ds + dc]
    ext_scr[HALO:, :] = xbc
    width = cw_ref.shape[0]
    conv = xbc * cw_ref[width - 1:width, :] + cb_ref[...]
    for i in range(1, width):
        conv = conv + ext_scr[HALO - i:HALO - i + tl, :] * cw_ref[width - 1 - i:width - i, :]
    ext_scr[0:HALO, :] = xbc[tl - HALO:, :]
    xc = conv * jax.nn.sigmoid(conv)
    xs = xc[:, 0:ds]
    dtr = p_ref[:, ds + dc:ds + dc + LANE] + dtb_ref[...]
    dt_c = jnp.maximum(dtr, 0.0) + jnp.log1p(jnp.exp(-jnp.abs(dtr)))
    dt = _dot(_split2_cols(dt_c), exp_ref[...])
    xs_scr[...] = xs
    xdt_scr[...] = xs * dt
    a_scr[...] = dt * (-jnp.exp(alog_ref[...]))
    bc_scr[...] = xc[:, ds:]

    incl = tri_ref[...] > 0.5
    upper = triu_ref[...] > 0.5

    probs = [(slice(c * CHUNK, (c + 1) * CHUNK), g) for c in range(tl // CHUNK) for g in range(n_grp)]
    gcols = lambda g: slice(g * BLOCK, (g + 1) * BLOCK)

    a_c = [a_scr[rows, gcols(g)] for rows, g in probs]
    e1 = _each(lambda a: _dot(tri2_ref[...], _split2_rows(a)), a_c)
    e2 = _each(lambda a: _dot(ones_ref[...], _split2_rows(jnp.where(upper, a, 0.0))), a_c)
    l_w = _each(lambda x, y: jnp.exp(jnp.where(incl, x - y, -jnp.inf)), e1, e2)
    b_g = [bc_scr[rows, g * d_state:(g + 1) * d_state].astype(BF16) for rows, g in probs]
    c_g = [bc_scr[rows, (n_grp + g) * d_state:(n_grp + g + 1) * d_state].astype(BF16) for rows, g in probs]
    scores = _each(lambda c, b: _dot_nt(c, jnp.concatenate([b, b, b, b], axis=0)), c_g, b_g)
    xdt_c = [xdt_scr[rows, gcols(g)] for rows, g in probs]
    y_diag = _each(lambda x, l, y: _dot((x * l).astype(BF16), _block_diag(y.astype(BF16), bdm_ref[...])),
                   scores, l_w, xdt_c)
    new = _each(lambda b, x, e: _dot_tn(b, (x * jnp.exp(e[CHUNK - 1:CHUNK, :] - e)).astype(BF16)),
                b_g, xdt_c, e1)

    sts = [st_scr[g] for g in range(n_grp)]
    stb = []
    for (_, g), e, n in zip(probs, e1, new):
        stb.append(sts[g].astype(BF16))
        sts[g] = sts[g] * jnp.exp(e[CHUNK - 1:CHUNK, :]) + n
    for g in range(n_grp):
        st_scr[g] = sts[g]
    for (rows, g), c, st, yd, e in zip(probs, c_g, stb, y_diag, e1):
        y_scr[rows, gcols(g)] = yd + _dot(c, st) * jnp.exp(e)

    z = p_ref[:, 0:ds]
    y = (y_scr[...] + dsk_ref[...] * xs_scr[...]) * (z * jax.nn.sigmoid(z))
    o_ref[...] = _rms(y, ng_ref[...], SSM_NORM_EPS)


def _ssd(p, batch, params, consts, n_grp, d_state, wp):
    ds = params["norm_g"].shape[1]
    dc = params["conv_w"].shape[1]
    assert ds == n_grp * BLOCK and params["conv_w"].shape[0] <= HALO and ds + dc + LANE <= wp
    names = ["conv_w", "conv_b", "dt_bias", "a_log", "d_skip", "norm_g", "expand"]
    cnames = ["tri2", "ones", "bdm", "tri", "triu"]
    ops = [params[n] for n in names] + [consts[n] for n in cnames]
    tl = MIX_ROWS
    tile = lambda: pltpu.VMEM((tl, ds), F32)
    scratch = [pltpu.VMEM((tl + HALO, dc), F32), pltpu.VMEM((n_grp, d_state, BLOCK), F32),
               tile(), tile(), tile(), pltpu.VMEM((tl, 2 * n_grp * d_state), F32), tile()]
    return _mixer_call(_ssd_kernel, p, 1, wp, ds, ops, scratch, batch, "ssd")


def _mask_consts():
    row = jnp.arange(BLOCK)[:, None]
    col = jnp.arange(BLOCK)[None, :]
    t = jnp.arange(CHUNK)[:, None]
    s = col % CHUNK
    tri64 = (jnp.arange(CHUNK)[None, :] <= t)
    bdm = row // CHUNK == col // CHUNK
    return {
        "bdm": bdm.astype(BF16),
        "bdm32": bdm.astype(F32),
        "trm": jnp.concatenate([s < t, s <= t], axis=0).astype(F32),
        "tri": (s <= t).astype(F32),
        "triu": (s >= t).astype(F32),
        "eye": (s == t).astype(F32),
        "tri2": jnp.concatenate([tri64, tri64], axis=1).astype(BF16),
        "ones": jnp.ones((CHUNK, 2 * CHUNK), BF16),
    }


def kernel(x, norm_ffn1, ffn1_w_gate, ffn1_w_up, ffn1_w_down, norm_mix, w_in, rwkv_mu, rwkv_w0, rwkv_w2,
           rwkv_a0, rwkv_a2, rwkv_g2, rwkv_k_k, rwkv_k_a, rwkv_r_k, rwkv_gn_g, rwkv_gn_b, ssm_conv_w,
           ssm_conv_b, ssm_dt_bias, ssm_a_log, ssm_d, ssm_norm, w_out, norm_ffn2, ffn2_w_gate, ffn2_w_up,
           ffn2_w_down, norm_final):
    batch, seq, d = x.shape
    depth = norm_ffn1.shape[0]
    dr = rwkv_w0.shape[1]
    ds = ssm_norm.shape[1]
    dc = ssm_conv_w.shape[2]
    n_heads_s = ssm_a_log.shape[1]
    lora_w = rwkv_w2.shape[1]
    lora_a = rwkv_a2.shape[1]
    lora_g = rwkv_g2.shape[1]
    n_grp = ds // BLOCK
    d_state = (dc - ds) // (2 * n_grp)
    wp = 3 * dr + 2 * LANE
    assert ds // n_heads_s == HEAD_DIM and lora_w + lora_a == LANE and lora_g == LANE
    assert n_heads_s <= LANE and ds + dc + LANE <= wp and seq % MIX_ROWS == 0

    consts = _mask_consts()
    lane_head = jnp.arange(LANE)[:, None] == (jnp.arange(ds) // HEAD_DIM)[None, :]
    expand = jnp.concatenate([lane_head, lane_head], axis=0).astype(BF16)
    row = lambda a: a.reshape(1, -1)
    rep = lambda a: jnp.repeat(a, HEAD_DIM).reshape(1, -1)

    xt = x.reshape(batch * seq, d)
    for i in range(depth):
        bf = lambda w: w[i].astype(BF16)
        x1 = _ffn1(xt, row(norm_ffn1[i]), bf(ffn1_w_gate), bf(ffn1_w_up), bf(ffn1_w_down))

        w = w_in[i]
        w_p = jnp.pad(w, ((0, 0), (0, 2 * wp - w.shape[1]))).astype(BF16)
        p = _inproj(x1, row(norm_mix[i]), w_p)

        rwkv_params = {
            "mu": row(rwkv_mu[i]), "w0": row(rwkv_w0[i]), "a0": row(rwkv_a0[i]),
            "w2": jnp.pad(rwkv_w2[i], ((0, lora_a), (0, 0))).astype(BF16),
            "a2": jnp.pad(rwkv_a2[i], ((lora_w, 0), (0, 0))).astype(BF16),
            "g2": rwkv_g2[i].astype(BF16),
            "k_k": row(rwkv_k_k[i]), "k_a": row(rwkv_k_a[i]), "r_k": row(rwkv_r_k[i]),
            "gn_g": row(rwkv_gn_g[i]), "gn_b": row(rwkv_gn_b[i]),
        }
        y_r = _rwkv(p, batch, rwkv_params, consts)

        ssd_params = {
            "conv_w": ssm_conv_w[i], "conv_b": row(ssm_conv_b[i]),
            "dt_bias": jnp.pad(ssm_dt_bias[i], (0, LANE - n_heads_s)).reshape(1, -1),
            "a_log": rep(ssm_a_log[i]), "d_skip": rep(ssm_d[i]), "norm_g": row(ssm_norm[i]),
            "expand": expand,
        }
        y_s = _ssd(p, batch, ssd_params, consts, n_grp, d_state, wp)

        wo = w_out[i].astype(BF16)
        assert i == depth - 1, "the fused final norm assumes a single layer"
        xt = _ffn2(x1, y_r, y_s, wo[:dr], wo[dr:], row(norm_ffn2[i]), bf(ffn2_w_gate), bf(ffn2_w_up),
                   bf(ffn2_w_down), row(norm_final))
    return xt.reshape(batch, seq, d)
```

```python
import functools

import jax
import jax.numpy as jnp
from jax import lax
from jax.experimental import pallas as pl
from jax.experimental.pallas import tpu as pltpu

F32 = jnp.float32
BF16 = jnp.bfloat16

NORM_EPS = 1e-6
RWKV_GN_EPS = 64e-5
SSM_NORM_EPS = 1e-5

HEAD_DIM = 64
CHUNK = 64
BLOCK = 256
LANE = 128
HALO = 8
MIX_ROWS = 512

NT_DIMS = (((1,), (1,)), ((), ()))
TN_DIMS = (((0,), (0,)), ((), ()))

VMEM_LIMIT = 56 * 1024 * 1024


def _dot(a, b):
    return jnp.dot(a, b, preferred_element_type=F32)


def _dot_nt(a, b):
    return lax.dot_general(a, b, NT_DIMS, preferred_element_type=F32)


def _dot_tn(a, b):
    return lax.dot_general(a, b, TN_DIMS, preferred_element_type=F32)


def _rms(x, g, eps):
    return x * lax.rsqrt(jnp.mean(x * x, axis=-1, keepdims=True) + eps) * g


def _split2_rows(x):
    hi = x.astype(BF16)
    lo = (x - hi.astype(F32)).astype(BF16)
    return jnp.concatenate([hi, lo], axis=0)


def _split2_cols(x):
    hi = x.astype(BF16)
    lo = (x - hi.astype(F32)).astype(BF16)
    return jnp.concatenate([hi, lo], axis=1)


def _head_sum(x, bdm_ref):
    xb = x.astype(BF16)
    return jnp.concatenate([_dot(xb[:, c:c + BLOCK], bdm_ref[...]) for c in range(0, x.shape[1], BLOCK)],
                           axis=1)


def _block_diag(xb, bdm):
    return jnp.concatenate([xb, xb, xb, xb], axis=0) * bdm


def _each(f, *lists):
    return [f(*args) for args in zip(*lists)]


def _resident(shape):
    zeros = (0,) * len(shape)
    return pl.BlockSpec(shape, lambda *_: zeros, pipeline_mode=pl.Buffered(1))


FF_CHUNK = 256
FFN_ROWS = 512


def _norm_split(x, g, h_scr, r_scr):
    h_scr[...] = (x * g).astype(BF16)
    r = lax.rsqrt(jnp.mean(x * x, axis=-1, keepdims=True) + NORM_EPS)
    r_scr[...] = jnp.broadcast_to(r, r_scr.shape)


def _swiglu(h_scr, r_scr, wg_ref, wu_ref, wd_ref, act_scr):
    d_ff = wg_ref.shape[1]
    for c in range(d_ff // FF_CHUNK):
        cs = slice(c * FF_CHUNK, (c + 1) * FF_CHUNK)
        r = jnp.concatenate([r_scr[...]] * (FF_CHUNK // LANE), axis=1)
        gate = _dot(h_scr[...], wg_ref[:, cs]) * r
        up = _dot(h_scr[...], wu_ref[:, cs]) * r
        act_scr[:, cs] = (gate * jax.nn.sigmoid(gate) * up).astype(BF16)
    return _dot(act_scr[...], wd_ref[...])


def _ffn1_kernel(x_ref, g_ref, wg_ref, wu_ref, wd_ref, o_ref, h_scr, r_scr, act_scr):
    _norm_split(x_ref[...], g_ref[...], h_scr, r_scr)
    o_ref[...] = x_ref[...] + 0.5 * _swiglu(h_scr, r_scr, wg_ref, wu_ref, wd_ref, act_scr)


def _ffn2_kernel(x_ref, yr_ref, ys_ref, wor_ref, wos_ref, g_ref, wg_ref, wu_ref, wd_ref, gf_ref,
                 o_ref, h_scr, r_scr, act_scr, x_scr):
    x2 = (x_ref[...] + _dot(yr_ref[...].astype(BF16), wor_ref[...])
          + _dot(ys_ref[...].astype(BF16), wos_ref[...]))
    x_scr[...] = x2
    _norm_split(x2, g_ref[...], h_scr, r_scr)
    y = _swiglu(h_scr, r_scr, wg_ref, wu_ref, wd_ref, act_scr)
    o_ref[...] = _rms(x_scr[...] + 0.5 * y, gf_ref[...], NORM_EPS)


def _ffn1(x, g, wg, wu, wd):
    t, d = x.shape
    d_ff = wg.shape[1]
    tm = FFN_ROWS
    assert t % tm == 0 and d_ff % FF_CHUNK == 0
    row = lambda i: (i, 0)
    return pl.pallas_call(
        _ffn1_kernel,
        grid=(t // tm,),
        in_specs=[pl.BlockSpec((tm, d), row), _resident(g.shape), _resident(wg.shape), _resident(wu.shape),
                  _resident(wd.shape)],
        out_specs=pl.BlockSpec((tm, d), row),
        out_shape=jax.ShapeDtypeStruct((t, d), F32),
        scratch_shapes=[pltpu.VMEM((tm, d), BF16), pltpu.VMEM((tm, LANE), F32), pltpu.VMEM((tm, d_ff), BF16)],
        compiler_params=pltpu.CompilerParams(
            dimension_semantics=("arbitrary",), vmem_limit_bytes=VMEM_LIMIT),
        name="ffn1",
    )(x, g, wg, wu, wd)


def _ffn2(x, yr, ys, wor, wos, g, wg, wu, wd, gf):
    t, d = x.shape
    d_ff = wg.shape[1]
    dm = yr.shape[1]
    tm = FFN_ROWS
    assert t % tm == 0 and d_ff % FF_CHUNK == 0
    row = lambda i: (i, 0)
    return pl.pallas_call(
        _ffn2_kernel,
        grid=(t // tm,),
        in_specs=[pl.BlockSpec((tm, d), row), pl.BlockSpec((tm, dm), row), pl.BlockSpec((tm, dm), row),
                  _resident(wor.shape), _resident(wos.shape), _resident(g.shape), _resident(wg.shape),
                  _resident(wu.shape), _resident(wd.shape), _resident(gf.shape)],
        out_specs=pl.BlockSpec((tm, d), row),
        out_shape=jax.ShapeDtypeStruct((t, d), F32),
        scratch_shapes=[pltpu.VMEM((tm, d), BF16), pltpu.VMEM((tm, LANE), F32), pltpu.VMEM((tm, d_ff), BF16),
                        pltpu.VMEM((tm, d), F32)],
        compiler_params=pltpu.CompilerParams(
            dimension_semantics=("arbitrary",), vmem_limit_bytes=VMEM_LIMIT),
        name="ffn2",
    )(x, yr, ys, wor, wos, g, wg, wu, wd, gf)


IN_CHUNK = 256


def _inproj_kernel(x_ref, g_ref, w_ref, wt_ref, mu_ref, cw_ref, cb_ref, p_ref, h_scr, ext_scr,
                   *, tiles_per_seq, wr, conv_lo):
    tm = x_ref.shape[0]
    dc = cw_ref.shape[1]
    w_main = w_ref.shape[1] // IN_CHUNK * IN_CHUNK
    width = cw_ref.shape[0]
    h_scr[...] = _rms(x_ref[...], g_ref[...], NORM_EPS).astype(BF16)

    @pl.when(lax.rem(pl.program_id(0), tiles_per_seq) == 0)
    def _():
        ext_scr[...] = jnp.zeros_like(ext_scr)

    def behind_halo(cols, pc):
        ext = jnp.concatenate([ext_scr[:, cols], pc], axis=0)
        ext_scr[:, cols] = pc[tm - HALO:, :]
        return ext

    def lerp(c0, pc):
        cs = slice(c0, c0 + IN_CHUNK)
        shifted = pltpu.roll(behind_halo(cs, pc), 1, axis=0)[HALO:, :]
        p_ref[:, cs] = pc + (shifted - pc) * mu_ref[:, cs]

    def conv_silu(c0, pc):
        j = c0 - conv_lo
        cw = slice(j, j + IN_CHUNK)
        ext = behind_halo(slice(wr + j, wr + j + IN_CHUNK), pc)
        tap = lambda i: cw_ref[width - 1 - i:width - i, cw]
        if width == 4:
            ext1 = pltpu.roll(ext, 1, axis=0)
            far = pltpu.roll(ext * tap(2) + ext1 * tap(3), 2, axis=0)[HALO:, :]
            conv = pc * tap(0) + ext1[HALO:, :] * tap(1) + far + cb_ref[:, cw]
        else:
            conv = pc * tap(0) + cb_ref[:, cw]
            for i in range(1, width):
                conv = conv + pltpu.roll(ext, i, axis=0)[HALO:, :] * tap(i)
        p_ref[:, c0:c0 + IN_CHUNK] = conv * jax.nn.sigmoid(conv)

    def plain(c0, pc):
        p_ref[:, c0:c0 + IN_CHUNK] = pc

    lerps = [(lerp, c0) for c0 in range(0, wr, IN_CHUNK)]
    convs = [(conv_silu, c0) for c0 in range(conv_lo, conv_lo + dc, IN_CHUNK)]
    plains = [(plain, c0) for c0 in list(range(wr, conv_lo, IN_CHUNK)) + list(range(conv_lo + dc, w_main, IN_CHUNK))]
    light = lerps + plains
    per_conv = len(light) // max(len(convs), 1)
    order = []
    for n, cv in enumerate(convs):
        order += [cv] + light[n * per_conv:(n + 1) * per_conv]
    order += light[len(convs) * per_conv:]
    for a in range(0, len(order), 2):
        pair = order[a:a + 2]
        vals = [_dot(h_scr[...], w_ref[:, c0:c0 + IN_CHUNK]) for _, c0 in pair]
        for (finish, c0), pc in zip(pair, vals):
            finish(c0, pc)
    p_ref[:, w_main:] = _dot(h_scr[...], wt_ref[...])


def _inproj(x, g, w, w_tail, mu, conv_w, conv_b, seq, wr, conv_lo):
    t, d = x.shape
    tm = 512
    dc = conv_w.shape[1]
    w_main = w.shape[1] // IN_CHUNK * IN_CHUNK
    assert t % tm == 0 and seq % tm == 0 and conv_w.shape[0] <= HALO
    assert all(v % IN_CHUNK == 0 for v in (wr, conv_lo, dc, w_main)) and wr <= conv_lo
    body = functools.partial(_inproj_kernel, tiles_per_seq=seq // tm, wr=wr, conv_lo=conv_lo)
    width_out = w_main + w_tail.shape[1]
    return pl.pallas_call(
        body,
        grid=(t // tm,),
        in_specs=[pl.BlockSpec((tm, d), lambda i: (i, 0)), _resident(g.shape), _resident(w.shape),
                  _resident(w_tail.shape), _resident(mu.shape), _resident(conv_w.shape),
                  _resident(conv_b.shape)],
        out_specs=pl.BlockSpec((tm, width_out), lambda i: (i, 0)),
        out_shape=jax.ShapeDtypeStruct((t, width_out), F32),
        scratch_shapes=[pltpu.VMEM((tm, d), BF16), pltpu.VMEM((HALO, wr + dc), F32)],
        compiler_params=pltpu.CompilerParams(
            dimension_semantics=("arbitrary",), vmem_limit_bytes=VMEM_LIMIT),
        name="in_proj",
    )(x, g, w, w_tail, mu, conv_w, conv_b)


def _mixer_call(body, p, col_block, width_in, width_out, ops, scratch, batch, name):
    t = p.shape[0]
    tl = MIX_ROWS
    nl = t // batch // tl
    assert t == batch * nl * tl
    tile = lambda b, l: (b * nl + l, 0)
    return pl.pallas_call(
        body,
        grid=(batch, nl),
        in_specs=[pl.BlockSpec((tl, width_in), lambda b, l: (b * nl + l, col_block))]
        + [_resident(o.shape) for o in ops],
        out_specs=pl.BlockSpec((tl, width_out), tile),
        out_shape=jax.ShapeDtypeStruct((t, width_out), F32),
        scratch_shapes=scratch,
        compiler_params=pltpu.CompilerParams(
            dimension_semantics=("arbitrary", "arbitrary"), vmem_limit_bytes=VMEM_LIMIT),
        name=name,
    )(p, *ops)


def _rwkv_kernel(p_ref, w0_ref, w2_ref, a0_ref, a2_ref, g2_ref, kk_ref, ka_ref, rk_ref,
                 gng_ref, gnb_ref, tri2_ref, bdm_ref, bdm32_ref, trm_ref, eye_ref,
                 o_ref,
                 zt_scr, r_scr, k_scr, v_scr, a_scr, b_scr, ld_scr, y_scr, bonus_scr, gate_scr):
    tl, dr = o_ref.shape

    @pl.when(pl.program_id(1) == 0)
    def _():
        zt_scr[...] = jnp.zeros_like(zt_scr)

    r = p_ref[:, 0:dr]
    k = p_ref[:, dr:2 * dr]
    v = p_ref[:, 2 * dr:3 * dr]
    wa = p_ref[:, 3 * dr:3 * dr + LANE]
    gd = p_ref[:, 3 * dr + LANE:3 * dr + 2 * LANE]

    wpre = w0_ref[...] + _dot(jnp.tanh(wa).astype(BF16), w2_ref[...])
    ld_scr[...] = jax.nn.sigmoid(wpre) * (-jnp.exp(-0.5))
    iclr = jax.nn.sigmoid(a0_ref[...] + _dot(wa.astype(BF16), a2_ref[...]))
    gate_scr[...] = _dot(jax.nn.sigmoid(gd).astype(BF16), g2_ref[...])

    kkv = k * kk_ref[...]
    kkn = kkv * lax.rsqrt(jnp.maximum(_head_sum(kkv * kkv, bdm_ref), 1e-24))
    kmod = k * (1.0 + (iclr - 1.0) * ka_ref[...])
    r_scr[...] = r
    k_scr[...] = kmod
    v_scr[...] = v
    a_scr[...] = -kkn
    b_scr[...] = kkn * iclr
    bonus_scr[...] = _head_sum(r * kmod * rk_ref[...], bdm_ref) * v

    bdmask32 = bdm32_ref[...] > 0.5
    trmask = trm_ref[...] > 0.5
    bd = lambda xb: _block_diag(xb, bdm_ref[...])
    n_q = dr // BLOCK

    probs = [(slice(c * CHUNK, (c + 1) * CHUNK), slice(q * BLOCK, (q + 1) * BLOCK), q)
             for c in range(tl // CHUNK) for q in range(n_q)]
    tile = lambda scr: [scr[rows, cols] for rows, cols, _ in probs]

    ld = tile(ld_scr)
    cum = _each(lambda x: _dot(tri2_ref[...], _split2_rows(x)), ld)
    b_c = tile(b_scr)
    k_c = tile(k_scr)
    vb = _each(lambda x: x.astype(BF16), tile(v_scr))
    atb = _each(lambda a, cm, l: (a * jnp.exp(cm - l)).astype(BF16), tile(a_scr), cum, ld)
    rtb = _each(lambda x, cm: (x * jnp.exp(cm)).astype(BF16), tile(r_scr), cum)
    inv = _each(lambda cm: jnp.exp(-cm), cum)
    g_end = _each(lambda cm: jnp.exp(cm[CHUNK - 1:CHUNK, :]), cum)
    to_end = _each(lambda cm: jnp.exp(cm[CHUNK - 1:CHUNK, :] - cm), cum)
    ar = _each(lambda a, x: jnp.concatenate([a, x], axis=0), atb, rtb)
    masked = lambda x: jnp.where(trmask, x, 0.0)
    sb32 = _each(lambda a, b, i: masked(_dot_nt(a, bd((b * i).astype(BF16)))), ar, b_c, inv)
    sk = _each(lambda a, x, i: masked(_dot_nt(a, bd((x * i).astype(BF16)))).astype(BF16), ar, k_c, inv)
    kv = _each(lambda x, y: _dot(x, bd(y)), sk, vb)
    t_inv = _each(lambda x: eye_ref[...] + x[0:CHUNK], sb32)
    pw = _each(lambda x: x[0:CHUNK].astype(BF16), sb32)
    pw = _each(lambda x: _dot(x, bd(x)).astype(BF16), pw)
    for i in range(1, 5):
        res = _each(lambda t, x: _dot(jnp.concatenate([t.astype(BF16), x], axis=0), bd(x)), t_inv, pw)
        t_inv = _each(lambda t, x: t + x[0:CHUNK], t_inv, res)
        pw = _each(lambda x: x[CHUNK:].astype(BF16), res)
    t_inv = _each(lambda t, x: t + _dot(t.astype(BF16), bd(x)), t_inv, pw)
    xx = _each(lambda t, a, x: _dot(t.astype(BF16),
                                    jnp.concatenate([bd(a), bd(x[0:CHUNK].astype(BF16))], axis=1)),
               t_inv, atb, kv)
    x1b = _each(lambda x: x[:, 0:BLOCK].astype(BF16), xx)
    x2 = _each(lambda x: x[:, BLOCK:], xx)
    bgb = _each(lambda b, e: (b * e).astype(BF16), b_c, to_end)
    kgb = _each(lambda x, e: (x * e).astype(BF16), k_c, to_end)
    p_t = _each(lambda x, b: _dot_tn(x, b).astype(BF16), x1b, bgb)
    c_t = _each(lambda x, y, b, kg: _dot_tn(jnp.concatenate([x.astype(BF16), y], axis=0),
                                            jnp.concatenate([b, kg], axis=0)), x2, vb, bgb, kgb)

    zs = [zt_scr[q] for q in range(n_q)]
    zb = []
    for (_, _, q), g, pt, ct in zip(probs, g_end, p_t, c_t):
        zb.append(zs[q].astype(BF16))
        zs[q] = zs[q] * g + jnp.where(bdmask32, _dot(zb[-1], pt) + ct, 0.0)
    for q in range(n_q):
        zt_scr[q] = zs[q]
    xz = _each(lambda x, rt, z: _dot_nt(jnp.concatenate([x, rt], axis=0), z), x1b, rtb, zb)
    ubd = _each(lambda m, x: bd((m[0:CHUNK] + x).astype(BF16)), xz, x2)
    for (rows, cols, _), m, n_rb, u, o in zip(probs, xz, sb32, ubd, kv):
        y_scr[rows, cols] = m[CHUNK:] + _dot(n_rb[CHUNK:].astype(BF16), u) + o[CHUNK:]

    y = y_scr[...]
    inv_n = 1.0 / HEAD_DIM
    mean = _head_sum(y, bdm_ref) * inv_n
    yc = y - mean
    var = _head_sum(yc * yc, bdm_ref) * inv_n
    yn = yc * lax.rsqrt(var + RWKV_GN_EPS) * gng_ref[...] + gnb_ref[...]
    o_ref[...] = (yn + bonus_scr[...]) * gate_scr[...]


def _rwkv(p, batch, params, consts):
    dr = params["w0"].shape[1]
    wp = 3 * dr + 2 * LANE
    assert dr % BLOCK == 0
    names = ["w0", "w2", "a0", "a2", "g2", "k_k", "k_a", "r_k", "gn_g", "gn_b"]
    cnames = ["tri2", "bdm", "bdm32", "trm", "eye"]
    ops = [params[n] for n in names] + [consts[n] for n in cnames]
    tl = MIX_ROWS
    scratch = ([pltpu.VMEM((dr // BLOCK, BLOCK, BLOCK), F32)]
               + [pltpu.VMEM((tl, dr), F32) for _ in range(9)])
    return _mixer_call(_rwkv_kernel, p, 0, wp, dr, ops, scratch, batch, "rwkv7")


def _ssd_kernel(p_ref, dtb_ref, alog_ref, dsk_ref, ng_ref, exp_ref,
                tri2_ref, ones_ref, bdm_ref, tri_ref, triu_ref,
                o_ref,
                st_scr, xs_scr, xdt_scr, a_scr, y_scr):
    tl, ds = o_ref.shape
    n_grp, d_state, _ = st_scr.shape
    dc = ds + 2 * n_grp * d_state

    @pl.when(pl.program_id(1) == 0)
    def _():
        st_scr[...] = jnp.zeros_like(st_scr)

    xs = p_ref[:, ds:2 * ds]
    dtr = p_ref[:, ds + dc:ds + dc + LANE] + dtb_ref[...]
    dt_c = jnp.maximum(dtr, 0.0) + jnp.log1p(jnp.exp(-jnp.abs(dtr)))
    dt = _dot(_split2_cols(dt_c), exp_ref[...])
    xs_scr[...] = xs
    xdt_scr[...] = xs * dt
    a_scr[...] = dt * (-jnp.exp(alog_ref[...]))
    bc0 = 2 * ds

    incl = tri_ref[...] > 0.5
    upper = triu_ref[...] > 0.5

    probs = [(slice(c * CHUNK, (c + 1) * CHUNK), g) for c in range(tl // CHUNK) for g in range(n_grp)]
    gcols = lambda g: slice(g * BLOCK, (g + 1) * BLOCK)

    a_c = [a_scr[rows, gcols(g)] for rows, g in probs]
    e1 = _each(lambda a: _dot(tri2_ref[...], _split2_rows(a)), a_c)
    e2 = _each(lambda a: _dot(ones_ref[...], _split2_rows(jnp.where(upper, a, 0.0))), a_c)
    l_w = _each(lambda x, y: jnp.exp(jnp.where(incl, x - y, -jnp.inf)), e1, e2)
    b_g = [p_ref[rows, bc0 + g * d_state:bc0 + (g + 1) * d_state].astype(BF16) for rows, g in probs]
    c_g = [p_ref[rows, bc0 + (n_grp + g) * d_state:bc0 + (n_grp + g + 1) * d_state].astype(BF16)
           for rows, g in probs]
    scores = _each(lambda c, b: _dot_nt(c, jnp.concatenate([b, b, b, b], axis=0)), c_g, b_g)
    xdt_c = [xdt_scr[rows, gcols(g)] for rows, g in probs]
    y_diag = _each(lambda x, l, y: _dot((x * l).astype(BF16), _block_diag(y.astype(BF16), bdm_ref[...])),
                   scores, l_w, xdt_c)
    new = _each(lambda b, x, e: _dot_tn(b, (x * jnp.exp(e[CHUNK - 1:CHUNK, :] - e)).astype(BF16)),
                b_g, xdt_c, e1)

    sts = [st_scr[g] for g in range(n_grp)]
    stb = []
    for (_, g), e, n in zip(probs, e1, new):
        stb.append(sts[g].astype(BF16))
        sts[g] = sts[g] * jnp.exp(e[CHUNK - 1:CHUNK, :]) + n
    for g in range(n_grp):
        st_scr[g] = sts[g]
    for (rows, g), c, st, yd, e in zip(probs, c_g, stb, y_diag, e1):
        y_scr[rows, gcols(g)] = yd + _dot(c, st) * jnp.exp(e)

    z = p_ref[:, 0:ds]
    y = (y_scr[...] + dsk_ref[...] * xs_scr[...]) * (z * jax.nn.sigmoid(z))
    o_ref[...] = _rms(y, ng_ref[...], SSM_NORM_EPS)


def _ssd(p, batch, params, consts, n_grp, d_state, wp):
    ds = params["norm_g"].shape[1]
    dc = ds + 2 * n_grp * d_state
    assert ds == n_grp * BLOCK and ds + dc + LANE <= wp
    names = ["dt_bias", "a_log", "d_skip", "norm_g", "expand"]
    cnames = ["tri2", "ones", "bdm", "tri", "triu"]
    ops = [params[n] for n in names] + [consts[n] for n in cnames]
    tl = MIX_ROWS
    tile = lambda: pltpu.VMEM((tl, ds), F32)
    scratch = [pltpu.VMEM((n_grp, d_state, BLOCK), F32), tile(), tile(), tile(), tile()]
    return _mixer_call(_ssd_kernel, p, 1, wp, ds, ops, scratch, batch, "ssd")


def _mask_consts():
    row = jnp.arange(BLOCK)[:, None]
    col = jnp.arange(BLOCK)[None, :]
    t = jnp.arange(CHUNK)[:, None]
    s = col % CHUNK
    tri64 = (jnp.arange(CHUNK)[None, :] <= t)
    bdm = row // CHUNK == col // CHUNK
    return {
        "bdm": bdm.astype(BF16),
        "bdm32": bdm.astype(F32),
        "trm": jnp.concatenate([s < t, s <= t], axis=0).astype(F32),
        "tri": (s <= t).astype(F32),
        "triu": (s >= t).astype(F32),
        "eye": (s == t).astype(F32),
        "tri2": jnp.concatenate([tri64, tri64], axis=1).astype(BF16),
        "ones": jnp.ones((CHUNK, 2 * CHUNK), BF16),
    }


def kernel(x, norm_ffn1, ffn1_w_gate, ffn1_w_up, ffn1_w_down, norm_mix, w_in, rwkv_mu, rwkv_w0, rwkv_w2,
           rwkv_a0, rwkv_a2, rwkv_g2, rwkv_k_k, rwkv_k_a, rwkv_r_k, rwkv_gn_g, rwkv_gn_b, ssm_conv_w,
           ssm_conv_b, ssm_dt_bias, ssm_a_log, ssm_d, ssm_norm, w_out, norm_ffn2, ffn2_w_gate, ffn2_w_up,
           ffn2_w_down, norm_final):
    batch, seq, d = x.shape
    depth = norm_ffn1.shape[0]
    dr = rwkv_w0.shape[1]
    ds = ssm_norm.shape[1]
    dc = ssm_conv_w.shape[2]
    n_heads_s = ssm_a_log.shape[1]
    lora_w = rwkv_w2.shape[1]
    lora_a = rwkv_a2.shape[1]
    lora_g = rwkv_g2.shape[1]
    n_grp = ds // BLOCK
    d_state = (dc - ds) // (2 * n_grp)
    wp = 3 * dr + 2 * LANE
    assert ds // n_heads_s == HEAD_DIM and lora_w + lora_a == LANE and lora_g == LANE
    assert n_heads_s <= LANE and ds + dc + LANE <= wp and seq % MIX_ROWS == 0

    consts = _mask_consts()
    lane_head = jnp.arange(LANE)[:, None] == (jnp.arange(ds) // HEAD_DIM)[None, :]
    expand = jnp.concatenate([lane_head, lane_head], axis=0).astype(BF16)
    row = lambda a: a.reshape(1, -1)
    rep = lambda a: jnp.repeat(a, HEAD_DIM).reshape(1, -1)

    xt = x.reshape(batch * seq, d)
    for i in range(depth):
        bf = lambda w: w[i].astype(BF16)
        x1 = _ffn1(xt, row(norm_ffn1[i]), bf(ffn1_w_gate), bf(ffn1_w_up), bf(ffn1_w_down))

        w = w_in[i]
        n_main = w.shape[1] // IN_CHUNK * IN_CHUNK
        w_tail = jnp.pad(w[:, n_main:], ((0, 0), (0, 2 * wp - w.shape[1]))).astype(BF16)
        p = _inproj(x1, row(norm_mix[i]), w.astype(BF16), w_tail, row(rwkv_mu[i]),
                    ssm_conv_w[i], row(ssm_conv_b[i]), seq, wp, wp + ds)

        rwkv_params = {
            "w0": row(rwkv_w0[i]), "a0": row(rwkv_a0[i]),
            "w2": jnp.pad(rwkv_w2[i], ((0, lora_a), (0, 0))).astype(BF16),
            "a2": jnp.pad(rwkv_a2[i], ((lora_w, 0), (0, 0))).astype(BF16),
            "g2": rwkv_g2[i].astype(BF16),
            "k_k": row(rwkv_k_k[i]), "k_a": row(rwkv_k_a[i]), "r_k": row(rwkv_r_k[i]),
            "gn_g": row(rwkv_gn_g[i]), "gn_b": row(rwkv_gn_b[i]),
        }
        y_r = _rwkv(p, batch, rwkv_params, consts)

        ssd_params = {
            "dt_bias": jnp.pad(ssm_dt_bias[i], (0, LANE - n_heads_s)).reshape(1, -1),
            "a_log": rep(ssm_a_log[i]), "d_skip": rep(ssm_d[i]), "norm_g": row(ssm_norm[i]),
            "expand": expand,
        }
        y_s = _ssd(p, batch, ssd_params, consts, n_grp, d_state, wp)

        wo = w_out[i].astype(BF16)
        assert i == depth - 1, "the fused final norm assumes a single layer"
        xt = _ffn2(x1, y_r, y_s, wo[:dr], wo[dr:], row(norm_ffn2[i]), bf(ffn2_w_gate), bf(ffn2_w_up),
                   bf(ffn2_w_down), row(norm_final))
    return xt.reshape(batch, seq, d)
```

```python
import functools

import jax
import jax.numpy as jnp
from jax import lax
from jax.experimental import pallas as pl
from jax.experimental.pallas import tpu as pltpu

F32 = jnp.float32
BF16 = jnp.bfloat16

NORM_EPS = 1e-6
RWKV_GN_EPS = 64e-5
SSM_NORM_EPS = 1e-5

HEAD_DIM = 64
CHUNK = 64
BLOCK = 256
LANE = 128
HALO = 8
MIX_ROWS = 512

NT_DIMS = (((1,), (1,)), ((), ()))
TN_DIMS = (((0,), (0,)), ((), ()))

VMEM_LIMIT = 56 * 1024 * 1024


def _dot(a, b):
    return jnp.dot(a, b, preferred_element_type=F32)


def _dot_nt(a, b):
    return lax.dot_general(a, b, NT_DIMS, preferred_element_type=F32)


def _dot_tn(a, b):
    return lax.dot_general(a, b, TN_DIMS, preferred_element_type=F32)


def _rms(x, g, eps):
    return x * lax.rsqrt(jnp.mean(x * x, axis=-1, keepdims=True) + eps) * g


def _split2_rows(x):
    hi = x.astype(BF16)
    lo = (x - hi.astype(F32)).astype(BF16)
    return jnp.concatenate([hi, lo], axis=0)


def _split2_cols(x):
    hi = x.astype(BF16)
    lo = (x - hi.astype(F32)).astype(BF16)
    return jnp.concatenate([hi, lo], axis=1)


def _head_sum(x, bdm_ref):
    xb = x.astype(BF16)
    return jnp.concatenate([_dot(xb[:, c:c + BLOCK], bdm_ref[...]) for c in range(0, x.shape[1], BLOCK)],
                           axis=1)


def _block_diag(xb, bdm):
    return jnp.concatenate([xb, xb, xb, xb], axis=0) * bdm


def _each(f, *lists):
    return [f(*args) for args in zip(*lists)]


def _resident(shape):
    zeros = (0,) * len(shape)
    return pl.BlockSpec(shape, lambda *_: zeros, pipeline_mode=pl.Buffered(1))


FF_CHUNK = 256
FFN_ROWS = 512


def _norm_split(x, g, h_scr, r_scr):
    h_scr[...] = (x * g).astype(BF16)
    r = lax.rsqrt(jnp.mean(x * x, axis=-1, keepdims=True) + NORM_EPS)
    r_scr[...] = jnp.broadcast_to(r, r_scr.shape)


def _swiglu(h_scr, r_scr, wg_ref, wu_ref, wd_ref, act_scr):
    d_ff = wg_ref.shape[1]
    for c in range(d_ff // FF_CHUNK):
        cs = slice(c * FF_CHUNK, (c + 1) * FF_CHUNK)
        r = jnp.concatenate([r_scr[...]] * (FF_CHUNK // LANE), axis=1)
        gate = _dot(h_scr[...], wg_ref[:, cs]) * r
        up = _dot(h_scr[...], wu_ref[:, cs]) * r
        act_scr[:, cs] = (gate * jax.nn.sigmoid(gate) * up).astype(BF16)
    return _dot(act_scr[...], wd_ref[...])


def _ffn1_kernel(x_ref, g_ref, wg_ref, wu_ref, wd_ref, o_ref, h_scr, r_scr, act_scr):
    _norm_split(x_ref[...], g_ref[...], h_scr, r_scr)
    o_ref[...] = x_ref[...] + 0.5 * _swiglu(h_scr, r_scr, wg_ref, wu_ref, wd_ref, act_scr)


def _ffn2_kernel(x_ref, yr_ref, ys_ref, wor_ref, wos_ref, g_ref, wg_ref, wu_ref, wd_ref, gf_ref,
                 o_ref, h_scr, r_scr, act_scr, x_scr):
    x2 = (x_ref[...] + _dot(yr_ref[...].astype(BF16), wor_ref[...])
          + _dot(ys_ref[...].astype(BF16), wos_ref[...]))
    x_scr[...] = x2
    _norm_split(x2, g_ref[...], h_scr, r_scr)
    y = _swiglu(h_scr, r_scr, wg_ref, wu_ref, wd_ref, act_scr)
    o_ref[...] = _rms(x_scr[...] + 0.5 * y, gf_ref[...], NORM_EPS)


def _ffn1(x, g, wg, wu, wd):
    t, d = x.shape
    d_ff = wg.shape[1]
    tm = FFN_ROWS
    assert t % tm == 0 and d_ff % FF_CHUNK == 0
    row = lambda i: (i, 0)
    return pl.pallas_call(
        _ffn1_kernel,
        grid=(t // tm,),
        in_specs=[pl.BlockSpec((tm, d), row), _resident(g.shape), _resident(wg.shape), _resident(wu.shape),
                  _resident(wd.shape)],
        out_specs=pl.BlockSpec((tm, d), row),
        out_shape=jax.ShapeDtypeStruct((t, d), F32),
        scratch_shapes=[pltpu.VMEM((tm, d), BF16), pltpu.VMEM((tm, LANE), F32), pltpu.VMEM((tm, d_ff), BF16)],
        compiler_params=pltpu.CompilerParams(
            dimension_semantics=("arbitrary",), vmem_limit_bytes=VMEM_LIMIT),
        name="ffn1",
    )(x, g, wg, wu, wd)


def _ffn2(x, yr, ys, wor, wos, g, wg, wu, wd, gf):
    t, d = x.shape
    d_ff = wg.shape[1]
    dm = yr.shape[1]
    tm = FFN_ROWS
    assert t % tm == 0 and d_ff % FF_CHUNK == 0
    row = lambda i: (i, 0)
    return pl.pallas_call(
        _ffn2_kernel,
        grid=(t // tm,),
        in_specs=[pl.BlockSpec((tm, d), row), pl.BlockSpec((tm, dm), row), pl.BlockSpec((tm, dm), row),
                  _resident(wor.shape), _resident(wos.shape), _resident(g.shape), _resident(wg.shape),
                  _resident(wu.shape), _resident(wd.shape), _resident(gf.shape)],
        out_specs=pl.BlockSpec((tm, d), row),
        out_shape=jax.ShapeDtypeStruct((t, d), F32),
        scratch_shapes=[pltpu.VMEM((tm, d), BF16), pltpu.VMEM((tm, LANE), F32), pltpu.VMEM((tm, d_ff), BF16),
                        pltpu.VMEM((tm, d), F32)],
        compiler_params=pltpu.CompilerParams(
            dimension_semantics=("arbitrary",), vmem_limit_bytes=VMEM_LIMIT),
        name="ffn2",
    )(x, yr, ys, wor, wos, g, wg, wu, wd, gf)


IN_CHUNK = 512


def _col_chunks(lo, hi):
    return [(c0, min(IN_CHUNK, hi - c0)) for c0 in range(lo, hi, IN_CHUNK)]


def _inproj_kernel(x_ref, g_ref, w_ref, mu_ref, cw_ref, cb_ref, p_ref, h_scr, ext_scr,
                   *, tiles_per_seq, wr, conv_lo):
    tm = x_ref.shape[0]
    dc = cw_ref.shape[1]
    width = cw_ref.shape[0]
    h_scr[...] = _rms(x_ref[...], g_ref[...], NORM_EPS).astype(BF16)

    @pl.when(lax.rem(pl.program_id(0), tiles_per_seq) == 0)
    def _():
        ext_scr[...] = jnp.zeros_like(ext_scr)

    def behind_halo(cols, pc):
        ext = jnp.concatenate([ext_scr[:, cols], pc], axis=0)
        ext_scr[:, cols] = pc[tm - HALO:, :]
        return ext

    def lerp(c0, n, pc):
        cs = slice(c0, c0 + n)
        shifted = pltpu.roll(behind_halo(cs, pc), 1, axis=0)[HALO:, :]
        p_ref[:, cs] = pc + (shifted - pc) * mu_ref[:, cs]

    def conv_silu(c0, n, pc):
        j = c0 - conv_lo
        cw = slice(j, j + n)
        ext = behind_halo(slice(wr + j, wr + j + n), pc)
        tap = lambda i: cw_ref[width - 1 - i:width - i, cw]
        if width == 4:
            ext1 = pltpu.roll(ext, 1, axis=0)
            far = pltpu.roll(ext * tap(2) + ext1 * tap(3), 2, axis=0)[HALO:, :]
            conv = pc * tap(0) + ext1[HALO:, :] * tap(1) + far + cb_ref[:, cw]
        else:
            conv = pc * tap(0) + cb_ref[:, cw]
            for i in range(1, width):
                conv = conv + pltpu.roll(ext, i, axis=0)[HALO:, :] * tap(i)
        p_ref[:, c0:c0 + n] = conv * jax.nn.sigmoid(conv)

    def plain(c0, n, pc):
        p_ref[:, c0:c0 + n] = pc

    convs = [(conv_silu, c) for c in _col_chunks(conv_lo, conv_lo + dc)]
    light = ([(lerp, c) for c in _col_chunks(0, wr)]
             + [(plain, c) for c in _col_chunks(wr, conv_lo) + _col_chunks(conv_lo + dc, w_ref.shape[1])])
    per_conv = len(light) // max(len(convs), 1)
    order = []
    for n, cv in enumerate(convs):
        order += [cv] + light[n * per_conv:(n + 1) * per_conv]
    order += light[len(convs) * per_conv:]
    for finish, (c0, n) in order:
        finish(c0, n, _dot(h_scr[...], w_ref[:, c0:c0 + n]))


def _inproj(x, g, w, mu, conv_w, conv_b, seq, wr, conv_lo):
    t, d = x.shape
    tm = 512
    dc = conv_w.shape[1]
    assert t % tm == 0 and seq % tm == 0 and conv_w.shape[0] <= HALO
    assert all(v % LANE == 0 for v in (wr, conv_lo, dc, w.shape[1])) and wr <= conv_lo
    body = functools.partial(_inproj_kernel, tiles_per_seq=seq // tm, wr=wr, conv_lo=conv_lo)
    return pl.pallas_call(
        body,
        grid=(t // tm,),
        in_specs=[pl.BlockSpec((tm, d), lambda i: (i, 0)), _resident(g.shape), _resident(w.shape),
                  _resident(mu.shape), _resident(conv_w.shape), _resident(conv_b.shape)],
        out_specs=pl.BlockSpec((tm, w.shape[1]), lambda i: (i, 0)),
        out_shape=jax.ShapeDtypeStruct((t, w.shape[1]), F32),
        scratch_shapes=[pltpu.VMEM((tm, d), BF16), pltpu.VMEM((HALO, wr + dc), F32)],
        compiler_params=pltpu.CompilerParams(
            dimension_semantics=("arbitrary",), vmem_limit_bytes=VMEM_LIMIT),
        name="in_proj",
    )(x, g, w, mu, conv_w, conv_b)


def _mixer_call(body, p, col_block, width_in, width_out, ops, scratch, batch, name):
    t = p.shape[0]
    tl = MIX_ROWS
    nl = t // batch // tl
    assert t == batch * nl * tl
    tile = lambda b, l: (b * nl + l, 0)
    return pl.pallas_call(
        body,
        grid=(batch, nl),
        in_specs=[pl.BlockSpec((tl, width_in), lambda b, l: (b * nl + l, col_block))]
        + [_resident(o.shape) for o in ops],
        out_specs=pl.BlockSpec((tl, width_out), tile),
        out_shape=jax.ShapeDtypeStruct((t, width_out), F32),
        scratch_shapes=scratch,
        compiler_params=pltpu.CompilerParams(
            dimension_semantics=("arbitrary", "arbitrary"), vmem_limit_bytes=VMEM_LIMIT),
        name=name,
    )(p, *ops)


def _rwkv_kernel(p_ref, w0_ref, w2_ref, a0_ref, a2_ref, g2_ref, kk_ref, ka_ref, rk_ref,
                 gng_ref, gnb_ref, tri2_ref, bdm_ref, bdm32_ref, trm_ref, eye_ref,
                 o_ref,
                 zt_scr, r_scr, k_scr, v_scr, a_scr, b_scr, ld_scr, y_scr, bonus_scr, gate_scr):
    tl, dr = o_ref.shape

    @pl.when(pl.program_id(1) == 0)
    def _():
        zt_scr[...] = jnp.zeros_like(zt_scr)

    r = p_ref[:, 0:dr]
    k = p_ref[:, dr:2 * dr]
    v = p_ref[:, 2 * dr:3 * dr]
    wa = p_ref[:, 3 * dr:3 * dr + LANE]
    gd = p_ref[:, 3 * dr + LANE:3 * dr + 2 * LANE]

    wpre = w0_ref[...] + _dot(jnp.tanh(wa).astype(BF16), w2_ref[...])
    ld_scr[...] = jax.nn.sigmoid(wpre) * (-jnp.exp(-0.5))
    iclr = jax.nn.sigmoid(a0_ref[...] + _dot(wa.astype(BF16), a2_ref[...]))
    gate_scr[...] = _dot(jax.nn.sigmoid(gd).astype(BF16), g2_ref[...])

    kkv = k * kk_ref[...]
    kkn = kkv * lax.rsqrt(jnp.maximum(_head_sum(kkv * kkv, bdm_ref), 1e-24))
    kmod = k * (1.0 + (iclr - 1.0) * ka_ref[...])
    r_scr[...] = r
    k_scr[...] = kmod
    v_scr[...] = v
    a_scr[...] = -kkn
    b_scr[...] = kkn * iclr
    bonus_scr[...] = _head_sum(r * kmod * rk_ref[...], bdm_ref) * v

    bdmask32 = bdm32_ref[...] > 0.5
    trmask = trm_ref[...] > 0.5
    bd = lambda xb: _block_diag(xb, bdm_ref[...])
    n_q = dr // BLOCK

    probs = [(slice(c * CHUNK, (c + 1) * CHUNK), slice(q * BLOCK, (q + 1) * BLOCK), q)
             for c in range(tl // CHUNK) for q in range(n_q)]
    tile = lambda scr: [scr[rows, cols] for rows, cols, _ in probs]

    ld = tile(ld_scr)
    cum = _each(lambda x: _dot(tri2_ref[...], _split2_rows(x)), ld)
    b_c = tile(b_scr)
    k_c = tile(k_scr)
    vb = _each(lambda x: x.astype(BF16), tile(v_scr))
    atb = _each(lambda a, cm, l: (a * jnp.exp(cm - l)).astype(BF16), tile(a_scr), cum, ld)
    rtb = _each(lambda x, cm: (x * jnp.exp(cm)).astype(BF16), tile(r_scr), cum)
    inv = _each(lambda cm: jnp.exp(-cm), cum)
    g_end = _each(lambda cm: jnp.exp(cm[CHUNK - 1:CHUNK, :]), cum)
    to_end = _each(lambda cm: jnp.exp(cm[CHUNK - 1:CHUNK, :] - cm), cum)
    ar = _each(lambda a, x: jnp.concatenate([a, x], axis=0), atb, rtb)
    masked = lambda x: jnp.where(trmask, x, 0.0)
    sb32 = _each(lambda a, b, i: masked(_dot_nt(a, bd((b * i).astype(BF16)))), ar, b_c, inv)
    sk = _each(lambda a, x, i: masked(_dot_nt(a, bd((x * i).astype(BF16)))).astype(BF16), ar, k_c, inv)
    kv = _each(lambda x, y: _dot(x, bd(y)), sk, vb)
    t_inv = _each(lambda x: eye_ref[...] + x[0:CHUNK], sb32)
    pw = _each(lambda x: x[0:CHUNK].astype(BF16), sb32)
    pw = _each(lambda x: _dot(x, bd(x)).astype(BF16), pw)
    for i in range(1, 5):
        res = _each(lambda t, x: _dot(jnp.concatenate([t.astype(BF16), x], axis=0), bd(x)), t_inv, pw)
        t_inv = _each(lambda t, x: t + x[0:CHUNK], t_inv, res)
        pw = _each(lambda x: x[CHUNK:].astype(BF16), res)
    t_inv = _each(lambda t, x: t + _dot(t.astype(BF16), bd(x)), t_inv, pw)
    xx = _each(lambda t, a, x: _dot(t.astype(BF16),
                                    jnp.concatenate([bd(a), bd(x[0:CHUNK].astype(BF16))], axis=1)),
               t_inv, atb, kv)
    x1b = _each(lambda x: x[:, 0:BLOCK].astype(BF16), xx)
    x2 = _each(lambda x: x[:, BLOCK:], xx)
    bgb = _each(lambda b, e: (b * e).astype(BF16), b_c, to_end)
    kgb = _each(lambda x, e: (x * e).astype(BF16), k_c, to_end)
    p_t = _each(lambda x, b: _dot_tn(x, b).astype(BF16), x1b, bgb)
    c_t = _each(lambda x, y, b, kg: _dot_tn(jnp.concatenate([x.astype(BF16), y], axis=0),
                                            jnp.concatenate([b, kg], axis=0)), x2, vb, bgb, kgb)

    zs = [zt_scr[q] for q in range(n_q)]
    zb = []
    for (_, _, q), g, pt, ct in zip(probs, g_end, p_t, c_t):
        zb.append(zs[q].astype(BF16))
        zs[q] = zs[q] * g + jnp.where(bdmask32, _dot(zb[-1], pt) + ct, 0.0)
    for q in range(n_q):
        zt_scr[q] = zs[q]
    xz = _each(lambda x, rt, z: _dot_nt(jnp.concatenate([x, rt], axis=0), z), x1b, rtb, zb)
    ubd = _each(lambda m, x: bd((m[0:CHUNK] + x).astype(BF16)), xz, x2)
    for (rows, cols, _), m, n_rb, u, o in zip(probs, xz, sb32, ubd, kv):
        y_scr[rows, cols] = m[CHUNK:] + _dot(n_rb[CHUNK:].astype(BF16), u) + o[CHUNK:]

    y = y_scr[...]
    inv_n = 1.0 / HEAD_DIM
    mean = _head_sum(y, bdm_ref) * inv_n
    yc = y - mean
    var = _head_sum(yc * yc, bdm_ref) * inv_n
    yn = yc * lax.rsqrt(var + RWKV_GN_EPS) * gng_ref[...] + gnb_ref[...]
    o_ref[...] = (yn + bonus_scr[...]) * gate_scr[...]


def _rwkv(p, batch, params, consts):
    dr = params["w0"].shape[1]
    wp = 3 * dr + 2 * LANE
    assert dr % BLOCK == 0
    names = ["w0", "w2", "a0", "a2", "g2", "k_k", "k_a", "r_k", "gn_g", "gn_b"]
    cnames = ["tri2", "bdm", "bdm32", "trm", "eye"]
    ops = [params[n] for n in names] + [consts[n] for n in cnames]
    tl = MIX_ROWS
    scratch = ([pltpu.VMEM((dr // BLOCK, BLOCK, BLOCK), F32)]
               + [pltpu.VMEM((tl, dr), F32) for _ in range(9)])
    return _mixer_call(_rwkv_kernel, p, 0, wp, dr, ops, scratch, batch, "rwkv7")


def _ssd_kernel(p_ref, dtb_ref, alog_ref, dsk_ref, ng_ref, exp_ref,
                tri2_ref, ones_ref, bdm_ref, tri_ref, triu_ref,
                o_ref,
                st_scr, xs_scr, xdt_scr, a_scr, y_scr):
    tl, ds = o_ref.shape
    n_grp, d_state, _ = st_scr.shape
    dc = ds + 2 * n_grp * d_state

    @pl.when(pl.program_id(1) == 0)
    def _():
        st_scr[...] = jnp.zeros_like(st_scr)

    xs = p_ref[:, ds:2 * ds]
    dtr = p_ref[:, ds + dc:ds + dc + LANE] + dtb_ref[...]
    dt_c = jnp.maximum(dtr, 0.0) + jnp.log1p(jnp.exp(-jnp.abs(dtr)))
    dt = _dot(_split2_cols(dt_c), exp_ref[...])
    xs_scr[...] = xs
    xdt_scr[...] = xs * dt
    a_scr[...] = dt * (-jnp.exp(alog_ref[...]))
    bc0 = 2 * ds

    incl = tri_ref[...] > 0.5
    upper = triu_ref[...] > 0.5

    probs = [(slice(c * CHUNK, (c + 1) * CHUNK), g) for c in range(tl // CHUNK) for g in range(n_grp)]
    gcols = lambda g: slice(g * BLOCK, (g + 1) * BLOCK)

    a_c = [a_scr[rows, gcols(g)] for rows, g in probs]
    e1 = _each(lambda a: _dot(tri2_ref[...], _split2_rows(a)), a_c)
    e2 = _each(lambda a: _dot(ones_ref[...], _split2_rows(jnp.where(upper, a, 0.0))), a_c)
    l_w = _each(lambda x, y: jnp.exp(jnp.where(incl, x - y, -jnp.inf)), e1, e2)
    b_g = [p_ref[rows, bc0 + g * d_state:bc0 + (g + 1) * d_state].astype(BF16) for rows, g in probs]
    c_g = [p_ref[rows, bc0 + (n_grp + g) * d_state:bc0 + (n_grp + g + 1) * d_state].astype(BF16)
           for rows, g in probs]
    scores = _each(lambda c, b: _dot_nt(c, jnp.concatenate([b, b, b, b], axis=0)), c_g, b_g)
    xdt_c = [xdt_scr[rows, gcols(g)] for rows, g in probs]
    y_diag = _each(lambda x, l, y: _dot((x * l).astype(BF16), _block_diag(y.astype(BF16), bdm_ref[...])),
                   scores, l_w, xdt_c)
    new = _each(lambda b, x, e: _dot_tn(b, (x * jnp.exp(e[CHUNK - 1:CHUNK, :] - e)).astype(BF16)),
                b_g, xdt_c, e1)

    sts = [st_scr[g] for g in range(n_grp)]
    stb = []
    for (_, g), e, n in zip(probs, e1, new):
        stb.append(sts[g].astype(BF16))
        sts[g] = sts[g] * jnp.exp(e[CHUNK - 1:CHUNK, :]) + n
    for g in range(n_grp):
        st_scr[g] = sts[g]
    for (rows, g), c, st, yd, e in zip(probs, c_g, stb, y_diag, e1):
        y_scr[rows, gcols(g)] = yd + _dot(c, st) * jnp.exp(e)

    z = p_ref[:, 0:ds]
    y = (y_scr[...] + dsk_ref[...] * xs_scr[...]) * (z * jax.nn.sigmoid(z))
    o_ref[...] = _rms(y, ng_ref[...], SSM_NORM_EPS)


def _ssd(p, batch, params, consts, n_grp, d_state, wp):
    ds = params["norm_g"].shape[1]
    dc = ds + 2 * n_grp * d_state
    assert ds == n_grp * BLOCK and ds + dc + LANE <= wp
    names = ["dt_bias", "a_log", "d_skip", "norm_g", "expand"]
    cnames = ["tri2", "ones", "bdm", "tri", "triu"]
    ops = [params[n] for n in names] + [consts[n] for n in cnames]
    tl = MIX_ROWS
    tile = lambda: pltpu.VMEM((tl, ds), F32)
    scratch = [pltpu.VMEM((n_grp, d_state, BLOCK), F32), tile(), tile(), tile(), tile()]
    return _mixer_call(_ssd_kernel, p, 1, wp, ds, ops, scratch, batch, "ssd")


def _mask_consts():
    row = jnp.arange(BLOCK)[:, None]
    col = jnp.arange(BLOCK)[None, :]
    t = jnp.arange(CHUNK)[:, None]
    s = col % CHUNK
    tri64 = (jnp.arange(CHUNK)[None, :] <= t)
    bdm = row // CHUNK == col // CHUNK
    return {
        "bdm": bdm.astype(BF16),
        "bdm32": bdm.astype(F32),
        "trm": jnp.concatenate([s < t, s <= t], axis=0).astype(F32),
        "tri": (s <= t).astype(F32),
        "triu": (s >= t).astype(F32),
        "eye": (s == t).astype(F32),
        "tri2": jnp.concatenate([tri64, tri64], axis=1).astype(BF16),
        "ones": jnp.ones((CHUNK, 2 * CHUNK), BF16),
    }


def kernel(x, norm_ffn1, ffn1_w_gate, ffn1_w_up, ffn1_w_down, norm_mix, w_in, rwkv_mu, rwkv_w0, rwkv_w2,
           rwkv_a0, rwkv_a2, rwkv_g2, rwkv_k_k, rwkv_k_a, rwkv_r_k, rwkv_gn_g, rwkv_gn_b, ssm_conv_w,
           ssm_conv_b, ssm_dt_bias, ssm_a_log, ssm_d, ssm_norm, w_out, norm_ffn2, ffn2_w_gate, ffn2_w_up,
           ffn2_w_down, norm_final):
    batch, seq, d = x.shape
    depth = norm_ffn1.shape[0]
    dr = rwkv_w0.shape[1]
    ds = ssm_norm.shape[1]
    dc = ssm_conv_w.shape[2]
    n_heads_s = ssm_a_log.shape[1]
    lora_w = rwkv_w2.shape[1]
    lora_a = rwkv_a2.shape[1]
    lora_g = rwkv_g2.shape[1]
    n_grp = ds // BLOCK
    d_state = (dc - ds) // (2 * n_grp)
    wp = 3 * dr + 2 * LANE
    assert ds // n_heads_s == HEAD_DIM and lora_w + lora_a == LANE and lora_g == LANE
    assert n_heads_s <= LANE and ds + dc + LANE <= wp and seq % MIX_ROWS == 0

    consts = _mask_consts()
    lane_head = jnp.arange(LANE)[:, None] == (jnp.arange(ds) // HEAD_DIM)[None, :]
    expand = jnp.concatenate([lane_head, lane_head], axis=0).astype(BF16)
    row = lambda a: a.reshape(1, -1)
    rep = lambda a: jnp.repeat(a, HEAD_DIM).reshape(1, -1)

    w_in_p = jnp.pad(w_in, ((0, 0), (0, 0), (0, 2 * wp - w_in.shape[2]))).astype(BF16)

    xt = x.reshape(batch * seq, d)
    for i in range(depth):
        bf = lambda w: w[i].astype(BF16)
        x1 = _ffn1(xt, row(norm_ffn1[i]), bf(ffn1_w_gate), bf(ffn1_w_up), bf(ffn1_w_down))

        p = _inproj(x1, row(norm_mix[i]), w_in_p[i], row(rwkv_mu[i]), ssm_conv_w[i], row(ssm_conv_b[i]),
                    seq, wp, wp + ds)

        rwkv_params = {
            "w0": row(rwkv_w0[i]), "a0": row(rwkv_a0[i]),
            "w2": jnp.pad(rwkv_w2[i], ((0, lora_a), (0, 0))).astype(BF16),
            "a2": jnp.pad(rwkv_a2[i], ((lora_w, 0), (0, 0))).astype(BF16),
            "g2": rwkv_g2[i].astype(BF16),
            "k_k": row(rwkv_k_k[i]), "k_a": row(rwkv_k_a[i]), "r_k": row(rwkv_r_k[i]),
            "gn_g": row(rwkv_gn_g[i]), "gn_b": row(rwkv_gn_b[i]),
        }
        y_r = _rwkv(p, batch, rwkv_params, consts)

        ssd_params = {
            "dt_bias": jnp.pad(ssm_dt_bias[i], (0, LANE - n_heads_s)).reshape(1, -1),
            "a_log": rep(ssm_a_log[i]), "d_skip": rep(ssm_d[i]), "norm_g": row(ssm_norm[i]),
            "expand": expand,
        }
        y_s = _ssd(p, batch, ssd_params, consts, n_grp, d_state, wp)

        wo = w_out[i].astype(BF16)
        assert i == depth - 1, "the fused final norm assumes a single layer"
        xt = _ffn2(x1, y_r, y_s, wo[:dr], wo[dr:], row(norm_ffn2[i]), bf(ffn2_w_gate), bf(ffn2_w_up),
                   bf(ffn2_w_down), row(norm_final))
    return xt.reshape(batch, seq, d)
```

```python
import functools

import jax
import jax.numpy as jnp
from jax import lax
from jax.experimental import pallas as pl
from jax.experimental.pallas import tpu as pltpu

F32 = jnp.float32
BF16 = jnp.bfloat16

NORM_EPS = 1e-6
RWKV_GN_EPS = 64e-5
SSM_NORM_EPS = 1e-5

HEAD_DIM = 64
CHUNK = 64
BLOCK = 256
LANE = 128
HALO = 8
MIX_ROWS = 512

NT_DIMS = (((1,), (1,)), ((), ()))
TN_DIMS = (((0,), (0,)), ((), ()))

VMEM_LIMIT = 56 * 1024 * 1024


def _dot(a, b):
    return jnp.dot(a, b, preferred_element_type=F32)


def _dot_nt(a, b):
    return lax.dot_general(a, b, NT_DIMS, preferred_element_type=F32)


def _dot_tn(a, b):
    return lax.dot_general(a, b, TN_DIMS, preferred_element_type=F32)


def _rms(x, g, eps):
    return x * lax.rsqrt(jnp.mean(x * x, axis=-1, keepdims=True) + eps) * g


def _split2_rows(x):
    hi = x.astype(BF16)
    lo = (x - hi.astype(F32)).astype(BF16)
    return jnp.concatenate([hi, lo], axis=0)


def _split2_cols(x):
    hi = x.astype(BF16)
    lo = (x - hi.astype(F32)).astype(BF16)
    return jnp.concatenate([hi, lo], axis=1)


def _head_sum(x, bdm_ref):
    xb = x.astype(BF16)
    return jnp.concatenate([_dot(xb[:, c:c + BLOCK], bdm_ref[...]) for c in range(0, x.shape[1], BLOCK)],
                           axis=1)


def _block_diag(xb, bdm):
    return jnp.concatenate([xb, xb, xb, xb], axis=0) * bdm


def _each(f, *lists):
    return [f(*args) for args in zip(*lists)]


def _resident(shape):
    zeros = (0,) * len(shape)
    return pl.BlockSpec(shape, lambda *_: zeros, pipeline_mode=pl.Buffered(1))


FF_CHUNK = 256
FFN_ROWS = 512


def _norm_split(x, g, h_scr, r_scr):
    h_scr[...] = (x * g).astype(BF16)
    r = lax.rsqrt(jnp.mean(x * x, axis=-1, keepdims=True) + NORM_EPS)
    r_scr[...] = jnp.broadcast_to(r, r_scr.shape)


def _swiglu(h_scr, r_scr, wg_ref, wu_ref, wd_ref, act_scr):
    d_ff = wg_ref.shape[1]
    for c in range(d_ff // FF_CHUNK):
        cs = slice(c * FF_CHUNK, (c + 1) * FF_CHUNK)
        r = jnp.concatenate([r_scr[...]] * (FF_CHUNK // LANE), axis=1)
        gate = _dot(h_scr[...], wg_ref[:, cs]) * r
        up = _dot(h_scr[...], wu_ref[:, cs]) * r
        act_scr[:, cs] = (gate * jax.nn.sigmoid(gate) * up).astype(BF16)
    return _dot(act_scr[...], wd_ref[...])


def _ffn1_kernel(x_ref, g_ref, wg_ref, wu_ref, wd_ref, o_ref, h_scr, r_scr, act_scr):
    _norm_split(x_ref[...], g_ref[...], h_scr, r_scr)
    o_ref[...] = x_ref[...] + 0.5 * _swiglu(h_scr, r_scr, wg_ref, wu_ref, wd_ref, act_scr)


def _ffn2_kernel(x_ref, yr_ref, ys_ref, wor_ref, wos_ref, g_ref, wg_ref, wu_ref, wd_ref, gf_ref,
                 o_ref, h_scr, r_scr, act_scr, x_scr):
    x2 = (x_ref[...] + _dot(yr_ref[...].astype(BF16), wor_ref[...])
          + _dot(ys_ref[...].astype(BF16), wos_ref[...]))
    x_scr[...] = x2
    _norm_split(x2, g_ref[...], h_scr, r_scr)
    y = _swiglu(h_scr, r_scr, wg_ref, wu_ref, wd_ref, act_scr)
    o_ref[...] = _rms(x_scr[...] + 0.5 * y, gf_ref[...], NORM_EPS)


def _ffn1(x, g, wg, wu, wd):
    t, d = x.shape
    d_ff = wg.shape[1]
    tm = FFN_ROWS
    assert t % tm == 0 and d_ff % FF_CHUNK == 0
    row = lambda i: (i, 0)
    return pl.pallas_call(
        _ffn1_kernel,
        grid=(t // tm,),
        in_specs=[pl.BlockSpec((tm, d), row), _resident(g.shape), _resident(wg.shape), _resident(wu.shape),
                  _resident(wd.shape)],
        out_specs=pl.BlockSpec((tm, d), row),
        out_shape=jax.ShapeDtypeStruct((t, d), F32),
        scratch_shapes=[pltpu.VMEM((tm, d), BF16), pltpu.VMEM((tm, LANE), F32), pltpu.VMEM((tm, d_ff), BF16)],
        compiler_params=pltpu.CompilerParams(
            dimension_semantics=("arbitrary",), vmem_limit_bytes=VMEM_LIMIT),
        name="ffn1",
    )(x, g, wg, wu, wd)


def _ffn2(x, yr, ys, wor, wos, g, wg, wu, wd, gf):
    t, d = x.shape
    d_ff = wg.shape[1]
    dm = yr.shape[1]
    tm = FFN_ROWS
    assert t % tm == 0 and d_ff % FF_CHUNK == 0
    row = lambda i: (i, 0)
    return pl.pallas_call(
        _ffn2_kernel,
        grid=(t // tm,),
        in_specs=[pl.BlockSpec((tm, d), row), pl.BlockSpec((tm, dm), row), pl.BlockSpec((tm, dm), row),
                  _resident(wor.shape), _resident(wos.shape), _resident(g.shape), _resident(wg.shape),
                  _resident(wu.shape), _resident(wd.shape), _resident(gf.shape)],
        out_specs=pl.BlockSpec((tm, d), row),
        out_shape=jax.ShapeDtypeStruct((t, d), F32),
        scratch_shapes=[pltpu.VMEM((tm, d), BF16), pltpu.VMEM((tm, LANE), F32), pltpu.VMEM((tm, d_ff), BF16),
                        pltpu.VMEM((tm, d), F32)],
        compiler_params=pltpu.CompilerParams(
            dimension_semantics=("arbitrary",), vmem_limit_bytes=VMEM_LIMIT),
        name="ffn2",
    )(x, yr, ys, wor, wos, g, wg, wu, wd, gf)


IN_CHUNK = 512


def _col_chunks(lo, hi):
    return [(c0, min(IN_CHUNK, hi - c0)) for c0 in range(lo, hi, IN_CHUNK)]


def _inproj_kernel(x_ref, g_ref, w_ref, wt_ref, mu_ref, cw_ref, cb_ref, p_ref, h_scr, ext_scr,
                   *, tiles_per_seq, wr, conv_lo):
    tm = x_ref.shape[0]
    dc = cw_ref.shape[1]
    width = cw_ref.shape[0]
    w_main = p_ref.shape[1] - wt_ref.shape[1]
    h_scr[...] = _rms(x_ref[...], g_ref[...], NORM_EPS).astype(BF16)

    @pl.when(lax.rem(pl.program_id(0), tiles_per_seq) == 0)
    def _():
        ext_scr[...] = jnp.zeros_like(ext_scr)

    def behind_halo(cols, pc):
        ext = jnp.concatenate([ext_scr[:, cols], pc], axis=0)
        ext_scr[:, cols] = pc[tm - HALO:, :]
        return ext

    def lerp(c0, n, pc):
        cs = slice(c0, c0 + n)
        shifted = pltpu.roll(behind_halo(cs, pc), 1, axis=0)[HALO:, :]
        p_ref[:, cs] = pc + (shifted - pc) * mu_ref[:, cs]

    def conv_silu(c0, n, pc):
        j = c0 - conv_lo
        cw = slice(j, j + n)
        ext = behind_halo(slice(wr + j, wr + j + n), pc)
        tap = lambda i: cw_ref[width - 1 - i:width - i, cw]
        if width == 4:
            ext1 = pltpu.roll(ext, 1, axis=0)
            far = pltpu.roll(ext * tap(2) + ext1 * tap(3), 2, axis=0)[HALO:, :]
            conv = pc * tap(0) + ext1[HALO:, :] * tap(1) + far + cb_ref[:, cw]
        else:
            conv = pc * tap(0) + cb_ref[:, cw]
            for i in range(1, width):
                conv = conv + pltpu.roll(ext, i, axis=0)[HALO:, :] * tap(i)
        p_ref[:, c0:c0 + n] = conv * jax.nn.sigmoid(conv)

    def plain(c0, n, pc):
        p_ref[:, c0:c0 + n] = pc

    convs = [(conv_silu, c) for c in _col_chunks(conv_lo, conv_lo + dc)]
    light = ([(lerp, c) for c in _col_chunks(0, wr)]
             + [(plain, c) for c in _col_chunks(wr, conv_lo) + _col_chunks(conv_lo + dc, w_main)])
    per_conv = len(light) // max(len(convs), 1)
    order = []
    for n, cv in enumerate(convs):
        order += [cv] + light[n * per_conv:(n + 1) * per_conv]
    order += light[len(convs) * per_conv:]
    for finish, (c0, n) in order:
        finish(c0, n, _dot(h_scr[...], w_ref[:, c0:c0 + n]))
    plain(w_main, wt_ref.shape[1], _dot(h_scr[...], wt_ref[...]))


def _inproj(x, g, w_all, layer, w_tail, mu, conv_w, conv_b, seq, wr, conv_lo):
    t, d = x.shape
    tm = 512
    dc = conv_w.shape[1]
    w_main = w_all.shape[2] // LANE * LANE
    width_out = w_main + w_tail.shape[1]
    assert t % tm == 0 and seq % tm == 0 and conv_w.shape[0] <= HALO
    assert all(v % LANE == 0 for v in (wr, conv_lo, dc, width_out)) and wr <= conv_lo <= w_main - dc
    body = functools.partial(_inproj_kernel, tiles_per_seq=seq // tm, wr=wr, conv_lo=conv_lo)
    return pl.pallas_call(
        body,
        grid=(t // tm,),
        in_specs=[pl.BlockSpec((tm, d), lambda i: (i, 0)), _resident(g.shape),
                  pl.BlockSpec((None,) + w_all.shape[1:], lambda i: (layer, 0, 0), pipeline_mode=pl.Buffered(1)),
                  _resident(w_tail.shape), _resident(mu.shape), _resident(conv_w.shape),
                  _resident(conv_b.shape)],
        out_specs=pl.BlockSpec((tm, width_out), lambda i: (i, 0)),
        out_shape=jax.ShapeDtypeStruct((t, width_out), F32),
        scratch_shapes=[pltpu.VMEM((tm, d), BF16), pltpu.VMEM((HALO, wr + dc), F32)],
        compiler_params=pltpu.CompilerParams(
            dimension_semantics=("arbitrary",), vmem_limit_bytes=VMEM_LIMIT),
        name="in_proj",
    )(x, g, w_all, w_tail, mu, conv_w, conv_b)


def _mixer_call(body, p, col_block, width_in, width_out, ops, scratch, batch, name):
    t = p.shape[0]
    tl = MIX_ROWS
    nl = t // batch // tl
    assert t == batch * nl * tl
    tile = lambda b, l: (b * nl + l, 0)
    return pl.pallas_call(
        body,
        grid=(batch, nl),
        in_specs=[pl.BlockSpec((tl, width_in), lambda b, l: (b * nl + l, col_block))]
        + [_resident(o.shape) for o in ops],
        out_specs=pl.BlockSpec((tl, width_out), tile),
        out_shape=jax.ShapeDtypeStruct((t, width_out), F32),
        scratch_shapes=scratch,
        compiler_params=pltpu.CompilerParams(
            dimension_semantics=("arbitrary", "arbitrary"), vmem_limit_bytes=VMEM_LIMIT),
        name=name,
    )(p, *ops)


def _rwkv_kernel(p_ref, w0_ref, w2_ref, a0_ref, a2_ref, g2_ref, kk_ref, ka_ref, rk_ref,
                 gng_ref, gnb_ref, tri2_ref, bdm_ref, bdm32_ref, trm_ref, eye_ref,
                 o_ref,
                 zt_scr, r_scr, k_scr, v_scr, a_scr, b_scr, ld_scr, y_scr, bonus_scr, gate_scr):
    tl, dr = o_ref.shape

    @pl.when(pl.program_id(1) == 0)
    def _():
        zt_scr[...] = jnp.zeros_like(zt_scr)

    r = p_ref[:, 0:dr]
    k = p_ref[:, dr:2 * dr]
    v = p_ref[:, 2 * dr:3 * dr]
    wa = p_ref[:, 3 * dr:3 * dr + LANE]
    gd = p_ref[:, 3 * dr + LANE:3 * dr + 2 * LANE]

    wpre = w0_ref[...] + _dot(jnp.tanh(wa).astype(BF16), w2_ref[...])
    ld_scr[...] = jax.nn.sigmoid(wpre) * (-jnp.exp(-0.5))
    iclr = jax.nn.sigmoid(a0_ref[...] + _dot(wa.astype(BF16), a2_ref[...]))
    gate_scr[...] = _dot(jax.nn.sigmoid(gd).astype(BF16), g2_ref[...])

    kkv = k * kk_ref[...]
    kkn = kkv * lax.rsqrt(jnp.maximum(_head_sum(kkv * kkv, bdm_ref), 1e-24))
    kmod = k * (1.0 + (iclr - 1.0) * ka_ref[...])
    r_scr[...] = r
    k_scr[...] = kmod
    v_scr[...] = v
    a_scr[...] = -kkn
    b_scr[...] = kkn * iclr
    bonus_scr[...] = _head_sum(r * kmod * rk_ref[...], bdm_ref) * v

    bdmask32 = bdm32_ref[...] > 0.5
    trmask = trm_ref[...] > 0.5
    bd = lambda xb: _block_diag(xb, bdm_ref[...])
    n_q = dr // BLOCK

    probs = [(slice(c * CHUNK, (c + 1) * CHUNK), slice(q * BLOCK, (q + 1) * BLOCK), q)
             for c in range(tl // CHUNK) for q in range(n_q)]
    tile = lambda scr: [scr[rows, cols] for rows, cols, _ in probs]

    ld = tile(ld_scr)
    cum = _each(lambda x: _dot(tri2_ref[...], _split2_rows(x)), ld)
    b_c = tile(b_scr)
    k_c = tile(k_scr)
    vb = _each(lambda x: x.astype(BF16), tile(v_scr))
    atb = _each(lambda a, cm, l: (a * jnp.exp(cm - l)).astype(BF16), tile(a_scr), cum, ld)
    rtb = _each(lambda x, cm: (x * jnp.exp(cm)).astype(BF16), tile(r_scr), cum)
    inv = _each(lambda cm: jnp.exp(-cm), cum)
    g_end = _each(lambda cm: jnp.exp(cm[CHUNK - 1:CHUNK, :]), cum)
    to_end = _each(lambda cm: jnp.exp(cm[CHUNK - 1:CHUNK, :] - cm), cum)
    ar = _each(lambda a, x: jnp.concatenate([a, x], axis=0), atb, rtb)
    masked = lambda x: jnp.where(trmask, x, 0.0)
    sb32 = _each(lambda a, b, i: masked(_dot_nt(a, bd((b * i).astype(BF16)))), ar, b_c, inv)
    sk = _each(lambda a, x, i: masked(_dot_nt(a, bd((x * i).astype(BF16)))).astype(BF16), ar, k_c, inv)
    kv = _each(lambda x, y: _dot(x, bd(y)), sk, vb)
    t_inv = _each(lambda x: eye_ref[...] + x[0:CHUNK], sb32)
    pw = _each(lambda x: x[0:CHUNK].astype(BF16), sb32)
    pw = _each(lambda x: _dot(x, bd(x)).astype(BF16), pw)
    for i in range(1, 5):
        res = _each(lambda t, x: _dot(jnp.concatenate([t.astype(BF16), x], axis=0), bd(x)), t_inv, pw)
        t_inv = _each(lambda t, x: t + x[0:CHUNK], t_inv, res)
        pw = _each(lambda x: x[CHUNK:].astype(BF16), res)
    t_inv = _each(lambda t, x: t + _dot(t.astype(BF16), bd(x)), t_inv, pw)
    xx = _each(lambda t, a, x: _dot(t.astype(BF16),
                                    jnp.concatenate([bd(a), bd(x[0:CHUNK].astype(BF16))], axis=1)),
               t_inv, atb, kv)
    x1b = _each(lambda x: x[:, 0:BLOCK].astype(BF16), xx)
    x2 = _each(lambda x: x[:, BLOCK:], xx)
    bgb = _each(lambda b, e: (b * e).astype(BF16), b_c, to_end)
    kgb = _each(lambda x, e: (x * e).astype(BF16), k_c, to_end)
    p_t = _each(lambda x, b: _dot_tn(x, b).astype(BF16), x1b, bgb)
    c_t = _each(lambda x, y, b, kg: _dot_tn(jnp.concatenate([x.astype(BF16), y], axis=0),
                                            jnp.concatenate([b, kg], axis=0)), x2, vb, bgb, kgb)

    zs = [zt_scr[q] for q in range(n_q)]
    zb = []
    for (_, _, q), g, pt, ct in zip(probs, g_end, p_t, c_t):
        zb.append(zs[q].astype(BF16))
        zs[q] = zs[q] * g + jnp.where(bdmask32, _dot(zb[-1], pt) + ct, 0.0)
    for q in range(n_q):
        zt_scr[q] = zs[q]
    xz = _each(lambda x, rt, z: _dot_nt(jnp.concatenate([x, rt], axis=0), z), x1b, rtb, zb)
    ubd = _each(lambda m, x: bd((m[0:CHUNK] + x).astype(BF16)), xz, x2)
    for (rows, cols, _), m, n_rb, u, o in zip(probs, xz, sb32, ubd, kv):
        y_scr[rows, cols] = m[CHUNK:] + _dot(n_rb[CHUNK:].astype(BF16), u) + o[CHUNK:]

    y = y_scr[...]
    inv_n = 1.0 / HEAD_DIM
    mean = _head_sum(y, bdm_ref) * inv_n
    yc = y - mean
    var = _head_sum(yc * yc, bdm_ref) * inv_n
    yn = yc * lax.rsqrt(var + RWKV_GN_EPS) * gng_ref[...] + gnb_ref[...]
    o_ref[...] = (yn + bonus_scr[...]) * gate_scr[...]


def _rwkv(p, batch, params, consts):
    dr = params["w0"].shape[1]
    wp = 3 * dr + 2 * LANE
    assert dr % BLOCK == 0
    names = ["w0", "w2", "a0", "a2", "g2", "k_k", "k_a", "r_k", "gn_g", "gn_b"]
    cnames = ["tri2", "bdm", "bdm32", "trm", "eye"]
    ops = [params[n] for n in names] + [consts[n] for n in cnames]
    tl = MIX_ROWS
    scratch = ([pltpu.VMEM((dr // BLOCK, BLOCK, BLOCK), F32)]
               + [pltpu.VMEM((tl, dr), F32) for _ in range(9)])
    return _mixer_call(_rwkv_kernel, p, 0, wp, dr, ops, scratch, batch, "rwkv7")


def _ssd_kernel(p_ref, dtb_ref, alog_ref, dsk_ref, ng_ref, exp_ref,
                tri2_ref, ones_ref, bdm_ref, tri_ref, triu_ref,
                o_ref,
                st_scr, xs_scr, xdt_scr, a_scr, y_scr):
    tl, ds = o_ref.shape
    n_grp, d_state, _ = st_scr.shape
    dc = ds + 2 * n_grp * d_state

    @pl.when(pl.program_id(1) == 0)
    def _():
        st_scr[...] = jnp.zeros_like(st_scr)

    xs = p_ref[:, ds:2 * ds]
    dtr = p_ref[:, ds + dc:ds + dc + LANE] + dtb_ref[...]
    dt_c = jnp.maximum(dtr, 0.0) + jnp.log1p(jnp.exp(-jnp.abs(dtr)))
    dt = _dot(_split2_cols(dt_c), exp_ref[...])
    xs_scr[...] = xs
    xdt_scr[...] = xs * dt
    a_scr[...] = dt * (-jnp.exp(alog_ref[...]))
    bc0 = 2 * ds

    incl = tri_ref[...] > 0.5
    upper = triu_ref[...] > 0.5

    probs = [(slice(c * CHUNK, (c + 1) * CHUNK), g) for c in range(tl // CHUNK) for g in range(n_grp)]
    gcols = lambda g: slice(g * BLOCK, (g + 1) * BLOCK)

    a_c = [a_scr[rows, gcols(g)] for rows, g in probs]
    e1 = _each(lambda a: _dot(tri2_ref[...], _split2_rows(a)), a_c)
    e2 = _each(lambda a: _dot(ones_ref[...], _split2_rows(jnp.where(upper, a, 0.0))), a_c)
    l_w = _each(lambda x, y: jnp.exp(jnp.where(incl, x - y, -jnp.inf)), e1, e2)
    b_g = [p_ref[rows, bc0 + g * d_state:bc0 + (g + 1) * d_state].astype(BF16) for rows, g in probs]
    c_g = [p_ref[rows, bc0 + (n_grp + g) * d_state:bc0 + (n_grp + g + 1) * d_state].astype(BF16)
           for rows, g in probs]
    scores = _each(lambda c, b: _dot_nt(c, jnp.concatenate([b, b, b, b], axis=0)), c_g, b_g)
    xdt_c = [xdt_scr[rows, gcols(g)] for rows, g in probs]
    y_diag = _each(lambda x, l, y: _dot((x * l).astype(BF16), _block_diag(y.astype(BF16), bdm_ref[...])),
                   scores, l_w, xdt_c)
    new = _each(lambda b, x, e: _dot_tn(b, (x * jnp.exp(e[CHUNK - 1:CHUNK, :] - e)).astype(BF16)),
                b_g, xdt_c, e1)

    sts = [st_scr[g] for g in range(n_grp)]
    stb = []
    for (_, g), e, n in zip(probs, e1, new):
        stb.append(sts[g].astype(BF16))
        sts[g] = sts[g] * jnp.exp(e[CHUNK - 1:CHUNK, :]) + n
    for g in range(n_grp):
        st_scr[g] = sts[g]
    for (rows, g), c, st, yd, e in zip(probs, c_g, stb, y_diag, e1):
        y_scr[rows, gcols(g)] = yd + _dot(c, st) * jnp.exp(e)

    z = p_ref[:, 0:ds]
    y = (y_scr[...] + dsk_ref[...] * xs_scr[...]) * (z * jax.nn.sigmoid(z))
    o_ref[...] = _rms(y, ng_ref[...], SSM_NORM_EPS)


def _ssd(p, batch, params, consts, n_grp, d_state, wp):
    ds = params["norm_g"].shape[1]
    dc = ds + 2 * n_grp * d_state
    assert ds == n_grp * BLOCK and ds + dc + LANE <= wp
    names = ["dt_bias", "a_log", "d_skip", "norm_g", "expand"]
    cnames = ["tri2", "ones", "bdm", "tri", "triu"]
    ops = [params[n] for n in names] + [consts[n] for n in cnames]
    tl = MIX_ROWS
    tile = lambda: pltpu.VMEM((tl, ds), F32)
    scratch = [pltpu.VMEM((n_grp, d_state, BLOCK), F32), tile(), tile(), tile(), tile()]
    return _mixer_call(_ssd_kernel, p, 1, wp, ds, ops, scratch, batch, "ssd")


def _mask_consts():
    row = jnp.arange(BLOCK)[:, None]
    col = jnp.arange(BLOCK)[None, :]
    t = jnp.arange(CHUNK)[:, None]
    s = col % CHUNK
    tri64 = (jnp.arange(CHUNK)[None, :] <= t)
    bdm = row // CHUNK == col // CHUNK
    return {
        "bdm": bdm.astype(BF16),
        "bdm32": bdm.astype(F32),
        "trm": jnp.concatenate([s < t, s <= t], axis=0).astype(F32),
        "tri": (s <= t).astype(F32),
        "triu": (s >= t).astype(F32),
        "eye": (s == t).astype(F32),
        "tri2": jnp.concatenate([tri64, tri64], axis=1).astype(BF16),
        "ones": jnp.ones((CHUNK, 2 * CHUNK), BF16),
    }


def kernel(x, norm_ffn1, ffn1_w_gate, ffn1_w_up, ffn1_w_down, norm_mix, w_in, rwkv_mu, rwkv_w0, rwkv_w2,
           rwkv_a0, rwkv_a2, rwkv_g2, rwkv_k_k, rwkv_k_a, rwkv_r_k, rwkv_gn_g, rwkv_gn_b, ssm_conv_w,
           ssm_conv_b, ssm_dt_bias, ssm_a_log, ssm_d, ssm_norm, w_out, norm_ffn2, ffn2_w_gate, ffn2_w_up,
           ffn2_w_down, norm_final):
    batch, seq, d = x.shape
    depth = norm_ffn1.shape[0]
    dr = rwkv_w0.shape[1]
    ds = ssm_norm.shape[1]
    dc = ssm_conv_w.shape[2]
    n_heads_s = ssm_a_log.shape[1]
    lora_w = rwkv_w2.shape[1]
    lora_a = rwkv_a2.shape[1]
    lora_g = rwkv_g2.shape[1]
    n_grp = ds // BLOCK
    d_state = (dc - ds) // (2 * n_grp)
    wp = 3 * dr + 2 * LANE
    assert ds // n_heads_s == HEAD_DIM and lora_w + lora_a == LANE and lora_g == LANE
    assert n_heads_s <= LANE and ds + dc + LANE <= wp and seq % MIX_ROWS == 0

    consts = _mask_consts()
    lane_head = jnp.arange(LANE)[:, None] == (jnp.arange(ds) // HEAD_DIM)[None, :]
    expand = jnp.concatenate([lane_head, lane_head], axis=0).astype(BF16)
    row = lambda a: a.reshape(1, -1)
    rep = lambda a: jnp.repeat(a, HEAD_DIM).reshape(1, -1)

    w_in_b = w_in.astype(BF16)
    n_main = w_in.shape[2] // LANE * LANE

    xt = x.reshape(batch * seq, d)
    for i in range(depth):
        bf = lambda w: w[i].astype(BF16)
        x1 = _ffn1(xt, row(norm_ffn1[i]), bf(ffn1_w_gate), bf(ffn1_w_up), bf(ffn1_w_down))

        w_tail = jnp.pad(w_in[i][:, n_main:], ((0, 0), (0, 2 * wp - w_in.shape[2]))).astype(BF16)
        p = _inproj(x1, row(norm_mix[i]), w_in_b, i, w_tail, row(rwkv_mu[i]), ssm_conv_w[i],
                    row(ssm_conv_b[i]), seq, wp, wp + ds)

        rwkv_params = {
            "w0": row(rwkv_w0[i]), "a0": row(rwkv_a0[i]),
            "w2": jnp.pad(rwkv_w2[i], ((0, lora_a), (0, 0))).astype(BF16),
            "a2": jnp.pad(rwkv_a2[i], ((lora_w, 0), (0, 0))).astype(BF16),
            "g2": rwkv_g2[i].astype(BF16),
            "k_k": row(rwkv_k_k[i]), "k_a": row(rwkv_k_a[i]), "r_k": row(rwkv_r_k[i]),
            "gn_g": row(rwkv_gn_g[i]), "gn_b": row(rwkv_gn_b[i]),
        }
        y_r = _rwkv(p, batch, rwkv_params, consts)

        ssd_params = {
            "dt_bias": jnp.pad(ssm_dt_bias[i], (0, LANE - n_heads_s)).reshape(1, -1),
            "a_log": rep(ssm_a_log[i]), "d_skip": rep(ssm_d[i]), "norm_g": row(ssm_norm[i]),
            "expand": expand,
        }
        y_s = _ssd(p, batch, ssd_params, consts, n_grp, d_state, wp)

        wo = w_out[i].astype(BF16)
        assert i == depth - 1, "the fused final norm assumes a single layer"
        xt = _ffn2(x1, y_r, y_s, wo[:dr], wo[dr:], row(norm_ffn2[i]), bf(ffn2_w_gate), bf(ffn2_w_up),
                   bf(ffn2_w_down), row(norm_final))
    return xt.reshape(batch, seq, d)
```

```python
import functools

import jax
import jax.numpy as jnp
from jax import lax
from jax.experimental import pallas as pl
from jax.experimental.pallas import tpu as pltpu

F32 = jnp.float32
BF16 = jnp.bfloat16

NORM_EPS = 1e-6
RWKV_GN_EPS = 64e-5
SSM_NORM_EPS = 1e-5

HEAD_DIM = 64
CHUNK = 64
BLOCK = 256
LANE = 128
HALO = 8
MIX_ROWS = 512

NT_DIMS = (((1,), (1,)), ((), ()))
TN_DIMS = (((0,), (0,)), ((), ()))

VMEM_LIMIT = 56 * 1024 * 1024


def _dot(a, b):
    return jnp.dot(a, b, preferred_element_type=F32)


def _dot_nt(a, b):
    return lax.dot_general(a, b, NT_DIMS, preferred_element_type=F32)


def _dot_tn(a, b):
    return lax.dot_general(a, b, TN_DIMS, preferred_element_type=F32)


def _rms(x, g, eps):
    return x * lax.rsqrt(jnp.mean(x * x, axis=-1, keepdims=True) + eps) * g


def _split2_rows(x):
    hi = x.astype(BF16)
    lo = (x - hi.astype(F32)).astype(BF16)
    return jnp.concatenate([hi, lo], axis=0)


def _split2_cols(x):
    hi = x.astype(BF16)
    lo = (x - hi.astype(F32)).astype(BF16)
    return jnp.concatenate([hi, lo], axis=1)


def _head_sum(x, bdm_ref):
    xb = x.astype(BF16)
    return jnp.concatenate([_dot(xb[:, c:c + BLOCK], bdm_ref[...]) for c in range(0, x.shape[1], BLOCK)],
                           axis=1)


def _block_diag(xb, bdm):
    return jnp.concatenate([xb, xb, xb, xb], axis=0) * bdm


def _each(f, *lists):
    return [f(*args) for args in zip(*lists)]


def _resident(shape):
    zeros = (0,) * len(shape)
    return pl.BlockSpec(shape, lambda *_: zeros, pipeline_mode=pl.Buffered(1))


FF_CHUNK = 256
FFN_ROWS = 512


def _norm_split(x, g, h_scr, r_scr):
    h_scr[...] = (x * g).astype(BF16)
    r = lax.rsqrt(jnp.mean(x * x, axis=-1, keepdims=True) + NORM_EPS)
    r_scr[...] = jnp.broadcast_to(r, r_scr.shape)


def _swiglu(h_scr, r_scr, wg_ref, wu_ref, wd_ref, act_scr):
    d_ff = wg_ref.shape[1]
    for c in range(d_ff // FF_CHUNK):
        cs = slice(c * FF_CHUNK, (c + 1) * FF_CHUNK)
        r = jnp.concatenate([r_scr[...]] * (FF_CHUNK // LANE), axis=1)
        gate = _dot(h_scr[...], wg_ref[:, cs]) * r
        up = _dot(h_scr[...], wu_ref[:, cs]) * r
        act_scr[:, cs] = (gate * jax.nn.sigmoid(gate) * up).astype(BF16)
    return _dot(act_scr[...], wd_ref[...])


def _ffn1_kernel(x_ref, g_ref, wg_ref, wu_ref, wd_ref, o_ref, h_scr, r_scr, act_scr):
    _norm_split(x_ref[...], g_ref[...], h_scr, r_scr)
    o_ref[...] = x_ref[...] + 0.5 * _swiglu(h_scr, r_scr, wg_ref, wu_ref, wd_ref, act_scr)


def _ffn2_kernel(x_ref, yr_ref, ys_ref, wor_ref, wos_ref, g_ref, wg_ref, wu_ref, wd_ref, gf_ref,
                 o_ref, h_scr, r_scr, act_scr, x_scr):
    x2 = x_ref[...] + _dot(yr_ref[...], wor_ref[...]) + _dot(ys_ref[...], wos_ref[...])
    x_scr[...] = x2
    _norm_split(x2, g_ref[...], h_scr, r_scr)
    y = _swiglu(h_scr, r_scr, wg_ref, wu_ref, wd_ref, act_scr)
    o_ref[...] = _rms(x_scr[...] + 0.5 * y, gf_ref[...], NORM_EPS)


def _ffn1(x, g, wg, wu, wd):
    t, d = x.shape
    d_ff = wg.shape[1]
    tm = FFN_ROWS
    assert t % tm == 0 and d_ff % FF_CHUNK == 0
    row = lambda i: (i, 0)
    return pl.pallas_call(
        _ffn1_kernel,
        grid=(t // tm,),
        in_specs=[pl.BlockSpec((tm, d), row), _resident(g.shape), _resident(wg.shape), _resident(wu.shape),
                  _resident(wd.shape)],
        out_specs=pl.BlockSpec((tm, d), row),
        out_shape=jax.ShapeDtypeStruct((t, d), F32),
        scratch_shapes=[pltpu.VMEM((tm, d), BF16), pltpu.VMEM((tm, LANE), F32), pltpu.VMEM((tm, d_ff), BF16)],
        compiler_params=pltpu.CompilerParams(
            dimension_semantics=("arbitrary",), vmem_limit_bytes=VMEM_LIMIT),
        name="ffn1",
    )(x, g, wg, wu, wd)


def _ffn2(x, yr, ys, wor, wos, g, wg, wu, wd, gf):
    t, d = x.shape
    d_ff = wg.shape[1]
    dm = yr.shape[1]
    tm = FFN_ROWS
    assert t % tm == 0 and d_ff % FF_CHUNK == 0
    row = lambda i: (i, 0)
    return pl.pallas_call(
        _ffn2_kernel,
        grid=(t // tm,),
        in_specs=[pl.BlockSpec((tm, d), row), pl.BlockSpec((tm, dm), row), pl.BlockSpec((tm, dm), row),
                  _resident(wor.shape), _resident(wos.shape), _resident(g.shape), _resident(wg.shape),
                  _resident(wu.shape), _resident(wd.shape), _resident(gf.shape)],
        out_specs=pl.BlockSpec((tm, d), row),
        out_shape=jax.ShapeDtypeStruct((t, d), F32),
        scratch_shapes=[pltpu.VMEM((tm, d), BF16), pltpu.VMEM((tm, LANE), F32), pltpu.VMEM((tm, d_ff), BF16),
                        pltpu.VMEM((tm, d), F32)],
        compiler_params=pltpu.CompilerParams(
            dimension_semantics=("arbitrary",), vmem_limit_bytes=VMEM_LIMIT),
        name="ffn2",
    )(x, yr, ys, wor, wos, g, wg, wu, wd, gf)


IN_CHUNK = 512


def _col_chunks(lo, hi):
    return [(c0, min(IN_CHUNK, hi - c0)) for c0 in range(lo, hi, IN_CHUNK)]


def _inproj_kernel(x_ref, g_ref, w_ref, wt_ref, mu_ref, cw_ref, cb_ref, p_ref, h_scr, ext_scr, raw_scr,
                   *, tiles_per_seq, wr, conv_lo):
    tm = x_ref.shape[0]
    dc = cw_ref.shape[1]
    width = cw_ref.shape[0]
    w_main = p_ref.shape[1] - wt_ref.shape[1]
    step = pl.program_id(0)
    h_scr[...] = _rms(x_ref[...], g_ref[...], NORM_EPS).astype(BF16)

    @pl.when(step == 0)
    def _():
        raw_scr[...] = jnp.zeros_like(raw_scr)

    @pl.when(lax.rem(step + tiles_per_seq - 1, tiles_per_seq) == 0)
    def _():
        ext_scr[...] = jnp.zeros_like(ext_scr)

    def behind_halo(cols, pc):
        ext = jnp.concatenate([ext_scr[:, cols], pc], axis=0)
        ext_scr[:, cols] = pc[tm - HALO:, :]
        return ext

    def lerp(c0, n, pc):
        cs = slice(c0, c0 + n)
        shifted = pltpu.roll(behind_halo(cs, pc), 1, axis=0)[HALO:, :]
        p_ref[:, cs] = pc + (shifted - pc) * mu_ref[:, cs]

    def conv_silu(c0, n, pc):
        j = c0 - conv_lo
        cw = slice(j, j + n)
        ext = behind_halo(slice(wr + j, wr + j + n), pc)
        tap = lambda i: cw_ref[width - 1 - i:width - i, cw]
        if width == 4:
            ext1 = pltpu.roll(ext, 1, axis=0)
            far = pltpu.roll(ext * tap(2) + ext1 * tap(3), 2, axis=0)[HALO:, :]
            conv = pc * tap(0) + ext1[HALO:, :] * tap(1) + far + cb_ref[:, cw]
        else:
            conv = pc * tap(0) + cb_ref[:, cw]
            for i in range(1, width):
                conv = conv + pltpu.roll(ext, i, axis=0)[HALO:, :] * tap(i)
        p_ref[:, c0:c0 + n] = conv * jax.nn.sigmoid(conv)

    def plain(c0, n, pc):
        p_ref[:, c0:c0 + n] = pc

    convs = [(conv_silu, c) for c in _col_chunks(conv_lo, conv_lo + dc)]
    light = ([(lerp, c) for c in _col_chunks(0, wr)]
             + [(plain, c) for c in _col_chunks(wr, conv_lo) + _col_chunks(conv_lo + dc, w_main)])
    per_conv = len(light) // max(len(convs), 1)
    order = []
    for n, cv in enumerate(convs):
        order += [cv] + light[n * per_conv:(n + 1) * per_conv]
    order += light[len(convs) * per_conv:]
    order.append((plain, (w_main, wt_ref.shape[1])))
    for finish, (c0, n) in order:
        before = raw_scr[:, c0:c0 + n]
        raw_scr[:, c0:c0 + n] = _dot(h_scr[...], w_ref[:, c0:c0 + n] if c0 < w_main else wt_ref[...])
        finish(c0, n, before)


def _inproj(x, g, w_all, layer, w_tail, mu, conv_w, conv_b, seq, wr, conv_lo):
    t, d = x.shape
    tm = 512
    dc = conv_w.shape[1]
    w_main = w_all.shape[2] // LANE * LANE
    width_out = w_main + w_tail.shape[1]
    assert t % tm == 0 and seq % tm == 0 and conv_w.shape[0] <= HALO
    assert all(v % LANE == 0 for v in (wr, conv_lo, dc, width_out)) and wr <= conv_lo <= w_main - dc
    body = functools.partial(_inproj_kernel, tiles_per_seq=seq // tm, wr=wr, conv_lo=conv_lo)
    n = t // tm
    return pl.pallas_call(
        body,
        grid=(n + 1,),
        in_specs=[pl.BlockSpec((tm, d), lambda i: (jnp.minimum(i, n - 1), 0)), _resident(g.shape),
                  pl.BlockSpec((None,) + w_all.shape[1:], lambda i: (layer, 0, 0), pipeline_mode=pl.Buffered(1)),
                  _resident(w_tail.shape), _resident(mu.shape), _resident(conv_w.shape),
                  _resident(conv_b.shape)],
        out_specs=pl.BlockSpec((tm, width_out), lambda i: (jnp.maximum(i - 1, 0), 0)),
        out_shape=jax.ShapeDtypeStruct((t, width_out), F32),
        scratch_shapes=[pltpu.VMEM((tm, d), BF16), pltpu.VMEM((HALO, wr + dc), F32),
                        pltpu.VMEM((tm, width_out), F32)],
        compiler_params=pltpu.CompilerParams(
            dimension_semantics=("arbitrary",), vmem_limit_bytes=VMEM_LIMIT),
        name="in_proj",
    )(x, g, w_all, w_tail, mu, conv_w, conv_b)


def _mixer_call(body, p, col_block, width_in, width_out, ops, scratch, batch, n_seq, name):
    t = p.shape[0]
    seq = t // batch
    rows = MIX_ROWS // n_seq
    assert t == batch * seq and batch % n_seq == 0 and seq % rows == 0 and rows % CHUNK == 0
    out = pl.pallas_call(
        body,
        grid=(batch // n_seq, seq // rows),
        in_specs=[pl.BlockSpec((n_seq, rows, width_in), lambda b, l: (b, l, col_block))]
        + [_resident(o.shape) for o in ops],
        out_specs=pl.BlockSpec((n_seq, rows, width_out), lambda b, l: (b, l, 0)),
        out_shape=jax.ShapeDtypeStruct((batch, seq, width_out), BF16),
        scratch_shapes=scratch,
        compiler_params=pltpu.CompilerParams(
            dimension_semantics=("arbitrary", "arbitrary"), vmem_limit_bytes=VMEM_LIMIT),
        name=name,
    )(p.reshape(batch, seq, p.shape[1]), *ops)
    return out.reshape(t, width_out)


def _rwkv_kernel(p_ref, w0_ref, w2_ref, a0_ref, a2_ref, g2_ref, kk_ref, ka_ref, rk_ref,
                 gng_ref, gnb_ref, tri2_ref, bdm_ref, bdm32_ref, trm_ref, eye_ref,
                 o_ref,
                 zt_scr, k_scr, a_scr, b_scr, ld_scr, y_scr, bonus_scr, gate_scr):
    n_seq, rows_seq, dr = o_ref.shape
    tl = n_seq * rows_seq

    @pl.when(pl.program_id(1) == 0)
    def _():
        zt_scr[...] = jnp.zeros_like(zt_scr)

    cols_of_p = lambda lo, hi: p_ref[:, :, lo:hi].reshape(tl, hi - lo)
    r = cols_of_p(0, dr)
    k = cols_of_p(dr, 2 * dr)
    v = cols_of_p(2 * dr, 3 * dr)
    wa = cols_of_p(3 * dr, 3 * dr + LANE)
    gd = cols_of_p(3 * dr + LANE, 3 * dr + 2 * LANE)

    wpre = w0_ref[...] + _dot(jnp.tanh(wa).astype(BF16), w2_ref[...])
    ld_scr[...] = jax.nn.sigmoid(wpre) * (-jnp.exp(-0.5))
    iclr = jax.nn.sigmoid(a0_ref[...] + _dot(wa.astype(BF16), a2_ref[...]))
    gate_scr[...] = _dot(jax.nn.sigmoid(gd).astype(BF16), g2_ref[...])

    kkv = k * kk_ref[...]
    kkn = kkv * lax.rsqrt(jnp.maximum(_head_sum(kkv * kkv, bdm_ref), 1e-24))
    kmod = k * (1.0 + (iclr - 1.0) * ka_ref[...])
    k_scr[...] = kmod
    a_scr[...] = -kkn
    b_scr[...] = kkn * iclr
    bonus_scr[...] = _head_sum(r * kmod * rk_ref[...], bdm_ref) * v

    bdmask32 = bdm32_ref[...] > 0.5
    trmask = trm_ref[...] > 0.5
    bd = lambda xb: _block_diag(xb, bdm_ref[...])
    n_q = dr // BLOCK

    probs = [(slice(s * rows_seq + c * CHUNK, s * rows_seq + (c + 1) * CHUNK),
              slice(q * BLOCK, (q + 1) * BLOCK), s * n_q + q)
             for c in range(rows_seq // CHUNK) for s in range(n_seq) for q in range(n_q)]
    tile = lambda scr: [scr[rows, cols] for rows, cols, _ in probs]
    from_p = lambda col0: [p_ref[s, c * CHUNK:(c + 1) * CHUNK, col0 + q * BLOCK:col0 + (q + 1) * BLOCK]
                           for c in range(rows_seq // CHUNK) for s in range(n_seq) for q in range(n_q)]

    ld = tile(ld_scr)
    cum = _each(lambda x: _dot(tri2_ref[...], _split2_rows(x)), ld)
    b_c = tile(b_scr)
    k_c = tile(k_scr)
    vb = _each(lambda x: x.astype(BF16), from_p(2 * dr))
    atb = _each(lambda a, cm, l: (a * jnp.exp(cm - l)).astype(BF16), tile(a_scr), cum, ld)
    rtb = _each(lambda x, cm: (x * jnp.exp(cm)).astype(BF16), from_p(0), cum)
    inv = _each(lambda cm: jnp.exp(-cm), cum)
    g_end = _each(lambda cm: jnp.exp(cm[CHUNK - 1:CHUNK, :]), cum)
    to_end = _each(lambda cm: jnp.exp(cm[CHUNK - 1:CHUNK, :] - cm), cum)
    ar = _each(lambda a, x: jnp.concatenate([a, x], axis=0), atb, rtb)
    masked = lambda x: jnp.where(trmask, x, 0.0)
    sb32 = _each(lambda a, b, i: masked(_dot_nt(a, bd((b * i).astype(BF16)))), ar, b_c, inv)
    sk = _each(lambda a, x, i: masked(_dot_nt(a, bd((x * i).astype(BF16)))).astype(BF16), ar, k_c, inv)
    kv = _each(lambda x, y: _dot(x, bd(y)), sk, vb)
    t_inv = _each(lambda x: eye_ref[...] + x[0:CHUNK], sb32)
    pw = _each(lambda x: x[0:CHUNK].astype(BF16), sb32)
    pw = _each(lambda x: _dot(x, bd(x)).astype(BF16), pw)
    for i in range(1, 5):
        res = _each(lambda t, x: _dot(jnp.concatenate([t.astype(BF16), x], axis=0), bd(x)), t_inv, pw)
        t_inv = _each(lambda t, x: t + x[0:CHUNK], t_inv, res)
        pw = _each(lambda x: x[CHUNK:].astype(BF16), res)
    t_inv = _each(lambda t, x: t + _dot(t.astype(BF16), bd(x)), t_inv, pw)
    xx = _each(lambda t, a, x: _dot(t.astype(BF16),
                                    jnp.concatenate([bd(a), bd(x[0:CHUNK].astype(BF16))], axis=1)),
               t_inv, atb, kv)
    x1b = _each(lambda x: x[:, 0:BLOCK].astype(BF16), xx)
    x2 = _each(lambda x: x[:, BLOCK:], xx)
    bgb = _each(lambda b, e: (b * e).astype(BF16), b_c, to_end)
    kgb = _each(lambda x, e: (x * e).astype(BF16), k_c, to_end)
    p_t = _each(lambda x, b: _dot_tn(x, b).astype(BF16), x1b, bgb)
    c_t = _each(lambda x, y, b, kg: _dot_tn(jnp.concatenate([x.astype(BF16), y], axis=0),
                                            jnp.concatenate([b, kg], axis=0)), x2, vb, bgb, kgb)

    zs = [zt_scr[j] for j in range(n_seq * n_q)]
    zb = []
    for (_, _, j), g, pt, ct in zip(probs, g_end, p_t, c_t):
        zb.append(zs[j].astype(BF16))
        zs[j] = zs[j] * g + jnp.where(bdmask32, _dot(zb[-1], pt) + ct, 0.0)
    for j in range(n_seq * n_q):
        zt_scr[j] = zs[j]
    xz = _each(lambda x, rt, z: _dot_nt(jnp.concatenate([x, rt], axis=0), z), x1b, rtb, zb)
    ubd = _each(lambda m, x: bd((m[0:CHUNK] + x).astype(BF16)), xz, x2)
    for (rows, cols, _), m, n_rb, u, o in zip(probs, xz, sb32, ubd, kv):
        y_scr[rows, cols] = m[CHUNK:] + _dot(n_rb[CHUNK:].astype(BF16), u) + o[CHUNK:]

    y = y_scr[...]
    inv_n = 1.0 / HEAD_DIM
    mean = _head_sum(y, bdm_ref) * inv_n
    yc = y - mean
    var = _head_sum(yc * yc, bdm_ref) * inv_n
    yn = yc * lax.rsqrt(var + RWKV_GN_EPS) * gng_ref[...] + gnb_ref[...]
    o_ref[...] = ((yn + bonus_scr[...]) * gate_scr[...]).astype(BF16).reshape(n_seq, rows_seq, dr)


RWKV_SEQS = 2


def _rwkv(p, batch, params, consts):
    dr = params["w0"].shape[1]
    wp = 3 * dr + 2 * LANE
    assert dr % BLOCK == 0
    names = ["w0", "w2", "a0", "a2", "g2", "k_k", "k_a", "r_k", "gn_g", "gn_b"]
    cnames = ["tri2", "bdm", "bdm32", "trm", "eye"]
    ops = [params[n] for n in names] + [consts[n] for n in cnames]
    tl = MIX_ROWS
    scratch = ([pltpu.VMEM((RWKV_SEQS * (dr // BLOCK), BLOCK, BLOCK), F32)]
               + [pltpu.VMEM((tl, dr), F32) for _ in range(7)])
    return _mixer_call(_rwkv_kernel, p, 0, wp, dr, ops, scratch, batch, RWKV_SEQS, "rwkv7")


def _ssd_kernel(p_ref, dtb_ref, alog_ref, dsk_ref, ng_ref, exp_ref,
                tri2_ref, ones_ref, bdm_ref, tri_ref, triu_ref,
                o_ref,
                st_scr, xs_scr, xdt_scr, a_scr, y_scr):
    p_ref, o_ref = p_ref.at[0], o_ref.at[0]
    tl, ds = o_ref.shape
    n_grp, d_state, _ = st_scr.shape
    dc = ds + 2 * n_grp * d_state

    @pl.when(pl.program_id(1) == 0)
    def _():
        st_scr[...] = jnp.zeros_like(st_scr)

    xs = p_ref[:, ds:2 * ds]
    dtr = p_ref[:, ds + dc:ds + dc + LANE] + dtb_ref[...]
    dt_c = jnp.maximum(dtr, 0.0) + jnp.log1p(jnp.exp(-jnp.abs(dtr)))
    dt = _dot(_split2_cols(dt_c), exp_ref[...])
    xs_scr[...] = xs
    xdt_scr[...] = xs * dt
    a_scr[...] = dt * (-jnp.exp(alog_ref[...]))
    bc0 = 2 * ds

    incl = tri_ref[...] > 0.5
    upper = triu_ref[...] > 0.5

    probs = [(slice(c * CHUNK, (c + 1) * CHUNK), g) for c in range(tl // CHUNK) for g in range(n_grp)]
    gcols = lambda g: slice(g * BLOCK, (g + 1) * BLOCK)

    a_c = [a_scr[rows, gcols(g)] for rows, g in probs]
    e1 = _each(lambda a: _dot(tri2_ref[...], _split2_rows(a)), a_c)
    e2 = _each(lambda a: _dot(ones_ref[...], _split2_rows(jnp.where(upper, a, 0.0))), a_c)
    l_w = _each(lambda x, y: jnp.exp(jnp.where(incl, x - y, -jnp.inf)), e1, e2)
    b_g = [p_ref[rows, bc0 + g * d_state:bc0 + (g + 1) * d_state].astype(BF16) for rows, g in probs]
    c_g = [p_ref[rows, bc0 + (n_grp + g) * d_state:bc0 + (n_grp + g + 1) * d_state].astype(BF16)
           for rows, g in probs]
    scores = _each(lambda c, b: _dot_nt(c, jnp.concatenate([b, b, b, b], axis=0)), c_g, b_g)
    xdt_c = [xdt_scr[rows, gcols(g)] for rows, g in probs]
    y_diag = _each(lambda x, l, y: _dot((x * l).astype(BF16), _block_diag(y.astype(BF16), bdm_ref[...])),
                   scores, l_w, xdt_c)
    new = _each(lambda b, x, e: _dot_tn(b, (x * jnp.exp(e[CHUNK - 1:CHUNK, :] - e)).astype(BF16)),
                b_g, xdt_c, e1)

    sts = [st_scr[g] for g in range(n_grp)]
    stb = []
    for (_, g), e, n in zip(probs, e1, new):
        stb.append(sts[g].astype(BF16))
        sts[g] = sts[g] * jnp.exp(e[CHUNK - 1:CHUNK, :]) + n
    for g in range(n_grp):
        st_scr[g] = sts[g]
    for (rows, g), c, st, yd, e in zip(probs, c_g, stb, y_diag, e1):
        y_scr[rows, gcols(g)] = yd + _dot(c, st) * jnp.exp(e)

    z = p_ref[:, 0:ds]
    y = (y_scr[...] + dsk_ref[...] * xs_scr[...]) * (z * jax.nn.sigmoid(z))
    o_ref[...] = _rms(y, ng_ref[...], SSM_NORM_EPS).astype(BF16)


def _ssd(p, batch, params, consts, n_grp, d_state, wp):
    ds = params["norm_g"].shape[1]
    dc = ds + 2 * n_grp * d_state
    assert ds == n_grp * BLOCK and ds + dc + LANE <= wp
    names = ["dt_bias", "a_log", "d_skip", "norm_g", "expand"]
    cnames = ["tri2", "ones", "bdm", "tri", "triu"]
    ops = [params[n] for n in names] + [consts[n] for n in cnames]
    tl = MIX_ROWS
    tile = lambda: pltpu.VMEM((tl, ds), F32)
    scratch = [pltpu.VMEM((n_grp, d_state, BLOCK), F32), tile(), tile(), tile(), tile()]
    return _mixer_call(_ssd_kernel, p, 1, wp, ds, ops, scratch, batch, 1, "ssd")


def _mask_consts():
    row = jnp.arange(BLOCK)[:, None]
    col = jnp.arange(BLOCK)[None, :]
    t = jnp.arange(CHUNK)[:, None]
    s = col % CHUNK
    tri64 = (jnp.arange(CHUNK)[None, :] <= t)
    bdm = row // CHUNK == col // CHUNK
    return {
        "bdm": bdm.astype(BF16),
        "bdm32": bdm.astype(F32),
        "trm": jnp.concatenate([s < t, s <= t], axis=0).astype(F32),
        "tri": (s <= t).astype(F32),
        "triu": (s >= t).astype(F32),
        "eye": (s == t).astype(F32),
        "tri2": jnp.concatenate([tri64, tri64], axis=1).astype(BF16),
        "ones": jnp.ones((CHUNK, 2 * CHUNK), BF16),
    }


def kernel(x, norm_ffn1, ffn1_w_gate, ffn1_w_up, ffn1_w_down, norm_mix, w_in, rwkv_mu, rwkv_w0, rwkv_w2,
           rwkv_a0, rwkv_a2, rwkv_g2, rwkv_k_k, rwkv_k_a, rwkv_r_k, rwkv_gn_g, rwkv_gn_b, ssm_conv_w,
           ssm_conv_b, ssm_dt_bias, ssm_a_log, ssm_d, ssm_norm, w_out, norm_ffn2, ffn2_w_gate, ffn2_w_up,
           ffn2_w_down, norm_final):
    batch, seq, d = x.shape
    depth = norm_ffn1.shape[0]
    dr = rwkv_w0.shape[1]
    ds = ssm_norm.shape[1]
    dc = ssm_conv_w.shape[2]
    n_heads_s = ssm_a_log.shape[1]
    lora_w = rwkv_w2.shape[1]
    lora_a = rwkv_a2.shape[1]
    lora_g = rwkv_g2.shape[1]
    n_grp = ds // BLOCK
    d_state = (dc - ds) // (2 * n_grp)
    wp = 3 * dr + 2 * LANE
    assert ds // n_heads_s == HEAD_DIM and lora_w + lora_a == LANE and lora_g == LANE
    assert n_heads_s <= LANE and ds + dc + LANE <= wp and seq % MIX_ROWS == 0

    consts = _mask_consts()
    lane_head = jnp.arange(LANE)[:, None] == (jnp.arange(ds) // HEAD_DIM)[None, :]
    expand = jnp.concatenate([lane_head, lane_head], axis=0).astype(BF16)
    row = lambda a: a.reshape(1, -1)
    rep = lambda a: jnp.repeat(a, HEAD_DIM).reshape(1, -1)

    w_in_b = w_in.astype(BF16)
    n_main = w_in.shape[2] // LANE * LANE

    xt = x.reshape(batch * seq, d)
    for i in range(depth):
        bf = lambda w: w[i].astype(BF16)
        x1 = _ffn1(xt, row(norm_ffn1[i]), bf(ffn1_w_gate), bf(ffn1_w_up), bf(ffn1_w_down))

        w_tail = jnp.pad(w_in[i][:, n_main:], ((0, 0), (0, 2 * wp - w_in.shape[2]))).astype(BF16)
        p = _inproj(x1, row(norm_mix[i]), w_in_b, i, w_tail, row(rwkv_mu[i]), ssm_conv_w[i],
                    row(ssm_conv_b[i]), seq, wp, wp + ds)

        rwkv_params = {
            "w0": row(rwkv_w0[i]), "a0": row(rwkv_a0[i]),
            "w2": jnp.pad(rwkv_w2[i], ((0, lora_a), (0, 0))).astype(BF16),
            "a2": jnp.pad(rwkv_a2[i], ((lora_w, 0), (0, 0))).astype(BF16),
            "g2": rwkv_g2[i].astype(BF16),
            "k_k": row(rwkv_k_k[i]), "k_a": row(rwkv_k_a[i]), "r_k": row(rwkv_r_k[i]),
            "gn_g": row(rwkv_gn_g[i]), "gn_b": row(rwkv_gn_b[i]),
        }
        y_r = _rwkv(p, batch, rwkv_params, consts)

        ssd_params = {
            "dt_bias": jnp.pad(ssm_dt_bias[i], (0, LANE - n_heads_s)).reshape(1, -1),
            "a_log": rep(ssm_a_log[i]), "d_skip": rep(ssm_d[i]), "norm_g": row(ssm_norm[i]),
            "expand": expand,
        }
        y_s = _ssd(p, batch, ssd_params, consts, n_grp, d_state, wp)

        wo = w_out[i].astype(BF16)
        assert i == depth - 1, "the fused final norm assumes a single layer"
        xt = _ffn2(x1, y_r, y_s, wo[:dr], wo[dr:], row(norm_ffn2[i]), bf(ffn2_w_gate), bf(ffn2_w_up),
                   bf(ffn2_w_down), row(norm_final))
    return xt.reshape(batch, seq, d)
```

```python
import functools

import jax
import jax.numpy as jnp
from jax import lax
from jax.experimental import pallas as pl
from jax.experimental.pallas import tpu as pltpu

F32 = jnp.float32
BF16 = jnp.bfloat16

NORM_EPS = 1e-6
RWKV_GN_EPS = 64e-5
SSM_NORM_EPS = 1e-5

HEAD_DIM = 64
CHUNK = 64
BLOCK = 256
LANE = 128
HALO = 8
MIX_ROWS = 512
SSD_ROWS = 1024

NT_DIMS = (((1,), (1,)), ((), ()))
TN_DIMS = (((0,), (0,)), ((), ()))

VMEM_LIMIT = 56 * 1024 * 1024


def _dot(a, b):
    return jnp.dot(a, b, preferred_element_type=F32)


def _dot_nt(a, b):
    return lax.dot_general(a, b, NT_DIMS, preferred_element_type=F32)


def _dot_tn(a, b):
    return lax.dot_general(a, b, TN_DIMS, preferred_element_type=F32)


def _rms(x, g, eps):
    return x * lax.rsqrt(jnp.mean(x * x, axis=-1, keepdims=True) + eps) * g


def _split2_rows(x):
    hi = x.astype(BF16)
    lo = (x - hi.astype(F32)).astype(BF16)
    return jnp.concatenate([hi, lo], axis=0)


def _split2_cols(x):
    hi = x.astype(BF16)
    lo = (x - hi.astype(F32)).astype(BF16)
    return jnp.concatenate([hi, lo], axis=1)


def _head_sum(x, bdm_ref):
    xb = x.astype(BF16)
    return jnp.concatenate([_dot(xb[:, c:c + BLOCK], bdm_ref[...]) for c in range(0, x.shape[1], BLOCK)],
                           axis=1)


def _block_diag(xb, bdm):
    return jnp.concatenate([xb, xb, xb, xb], axis=0) * bdm


def _each(f, *lists):
    return [f(*args) for args in zip(*lists)]


def _resident(shape):
    zeros = (0,) * len(shape)
    return pl.BlockSpec(shape, lambda *_: zeros, pipeline_mode=pl.Buffered(1))


FF_CHUNK = 256
FFN_ROWS = 512


def _norm_split(x, g, h_scr, r_scr):
    h_scr[...] = (x * g).astype(BF16)
    r = lax.rsqrt(jnp.mean(x * x, axis=-1, keepdims=True) + NORM_EPS)
    r_scr[...] = jnp.broadcast_to(r, r_scr.shape)


def _swiglu(h_scr, r_scr, wg_ref, wu_ref, wd_ref, act_scr):
    d_ff = wg_ref.shape[1]
    for c in range(d_ff // FF_CHUNK):
        cs = slice(c * FF_CHUNK, (c + 1) * FF_CHUNK)
        r = jnp.concatenate([r_scr[...]] * (FF_CHUNK // LANE), axis=1)
        gate = _dot(h_scr[...], wg_ref[:, cs]) * r
        up = _dot(h_scr[...], wu_ref[:, cs]) * r
        act_scr[:, cs] = (gate * jax.nn.sigmoid(gate) * up).astype(BF16)
    return _dot(act_scr[...], wd_ref[...])


def _ffn1_kernel(x_ref, g_ref, wg_ref, wu_ref, wd_ref, o_ref, h_scr, r_scr, act_scr):
    _norm_split(x_ref[...], g_ref[...], h_scr, r_scr)
    o_ref[...] = x_ref[...] + 0.5 * _swiglu(h_scr, r_scr, wg_ref, wu_ref, wd_ref, act_scr)


def _ffn2_kernel(x_ref, yr_ref, ys_ref, wor_ref, wos_ref, g_ref, wg_ref, wu_ref, wd_ref, gf_ref,
                 o_ref, h_scr, r_scr, act_scr, x_scr):
    x2 = x_ref[...] + _dot(yr_ref[...], wor_ref[...]) + _dot(ys_ref[...], wos_ref[...])
    x_scr[...] = x2
    _norm_split(x2, g_ref[...], h_scr, r_scr)
    y = _swiglu(h_scr, r_scr, wg_ref, wu_ref, wd_ref, act_scr)
    o_ref[...] = _rms(x_scr[...] + 0.5 * y, gf_ref[...], NORM_EPS)


def _ffn1(x, g, wg, wu, wd):
    t, d = x.shape
    d_ff = wg.shape[1]
    tm = FFN_ROWS
    assert t % tm == 0 and d_ff % FF_CHUNK == 0
    row = lambda i: (i, 0)
    return pl.pallas_call(
        _ffn1_kernel,
        grid=(t // tm,),
        in_specs=[pl.BlockSpec((tm, d), row), _resident(g.shape), _resident(wg.shape), _resident(wu.shape),
                  _resident(wd.shape)],
        out_specs=pl.BlockSpec((tm, d), row),
        out_shape=jax.ShapeDtypeStruct((t, d), F32),
        scratch_shapes=[pltpu.VMEM((tm, d), BF16), pltpu.VMEM((tm, LANE), F32), pltpu.VMEM((tm, d_ff), BF16)],
        compiler_params=pltpu.CompilerParams(
            dimension_semantics=("arbitrary",), vmem_limit_bytes=VMEM_LIMIT),
        name="ffn1",
    )(x, g, wg, wu, wd)


def _ffn2(x, yr, ys, wor, wos, g, wg, wu, wd, gf):
    t, d = x.shape
    d_ff = wg.shape[1]
    dm = yr.shape[1]
    tm = FFN_ROWS
    assert t % tm == 0 and d_ff % FF_CHUNK == 0
    row = lambda i: (i, 0)
    return pl.pallas_call(
        _ffn2_kernel,
        grid=(t // tm,),
        in_specs=[pl.BlockSpec((tm, d), row), pl.BlockSpec((tm, dm), row), pl.BlockSpec((tm, dm), row),
                  _resident(wor.shape), _resident(wos.shape), _resident(g.shape), _resident(wg.shape),
                  _resident(wu.shape), _resident(wd.shape), _resident(gf.shape)],
        out_specs=pl.BlockSpec((tm, d), row),
        out_shape=jax.ShapeDtypeStruct((t, d), F32),
        scratch_shapes=[pltpu.VMEM((tm, d), BF16), pltpu.VMEM((tm, LANE), F32), pltpu.VMEM((tm, d_ff), BF16),
                        pltpu.VMEM((tm, d), F32)],
        compiler_params=pltpu.CompilerParams(
            dimension_semantics=("arbitrary",), vmem_limit_bytes=VMEM_LIMIT),
        name="ffn2",
    )(x, yr, ys, wor, wos, g, wg, wu, wd, gf)


IN_CHUNK = 512


def _col_chunks(lo, hi):
    return [(c0, min(IN_CHUNK, hi - c0)) for c0 in range(lo, hi, IN_CHUNK)]


def _inproj_kernel(x_ref, g_ref, w_ref, wt_ref, mu_ref, cw_ref, cb_ref, p_ref, h_scr, ext_scr, raw_scr,
                   *, tiles_per_seq, wr, conv_lo):
    tm = x_ref.shape[0]
    dc = cw_ref.shape[1]
    width = cw_ref.shape[0]
    w_main = p_ref.shape[1] - wt_ref.shape[1]
    step = pl.program_id(0)
    h_scr[...] = _rms(x_ref[...], g_ref[...], NORM_EPS).astype(BF16)

    @pl.when(step == 0)
    def _():
        raw_scr[...] = jnp.zeros_like(raw_scr)

    @pl.when(lax.rem(step + tiles_per_seq - 1, tiles_per_seq) == 0)
    def _():
        ext_scr[...] = jnp.zeros_like(ext_scr)

    def behind_halo(cols, pc):
        ext = jnp.concatenate([ext_scr[:, cols], pc], axis=0)
        ext_scr[:, cols] = pc[tm - HALO:, :]
        return ext

    def lerp(c0, n, pc):
        cs = slice(c0, c0 + n)
        shifted = pltpu.roll(behind_halo(cs, pc), 1, axis=0)[HALO:, :]
        p_ref[:, cs] = pc + (shifted - pc) * mu_ref[:, cs]

    def conv_silu(c0, n, pc):
        j = c0 - conv_lo
        cw = slice(j, j + n)
        ext = behind_halo(slice(wr + j, wr + j + n), pc)
        tap = lambda i: cw_ref[width - 1 - i:width - i, cw]
        if width == 4:
            ext1 = pltpu.roll(ext, 1, axis=0)
            far = pltpu.roll(ext * tap(2) + ext1 * tap(3), 2, axis=0)[HALO:, :]
            conv = pc * tap(0) + ext1[HALO:, :] * tap(1) + far + cb_ref[:, cw]
        else:
            conv = pc * tap(0) + cb_ref[:, cw]
            for i in range(1, width):
                conv = conv + pltpu.roll(ext, i, axis=0)[HALO:, :] * tap(i)
        p_ref[:, c0:c0 + n] = conv * jax.nn.sigmoid(conv)

    def plain(c0, n, pc):
        p_ref[:, c0:c0 + n] = pc

    convs = [(conv_silu, c) for c in _col_chunks(conv_lo, conv_lo + dc)]
    light = ([(lerp, c) for c in _col_chunks(0, wr)]
             + [(plain, c) for c in _col_chunks(wr, conv_lo) + _col_chunks(conv_lo + dc, w_main)])
    per_conv = len(light) // max(len(convs), 1)
    order = []
    for n, cv in enumerate(convs):
        order += [cv] + light[n * per_conv:(n + 1) * per_conv]
    order += light[len(convs) * per_conv:]
    order.append((plain, (w_main, wt_ref.shape[1])))
    for finish, (c0, n) in order:
        before = raw_scr[:, c0:c0 + n]
        raw_scr[:, c0:c0 + n] = _dot(h_scr[...], w_ref[:, c0:c0 + n] if c0 < w_main else wt_ref[...])
        finish(c0, n, before)


def _inproj(x, g, w_all, layer, w_tail, mu, conv_w, conv_b, seq, wr, conv_lo):
    t, d = x.shape
    tm = 512
    dc = conv_w.shape[1]
    w_main = w_all.shape[2] // LANE * LANE
    width_out = w_main + w_tail.shape[1]
    assert t % tm == 0 and seq % tm == 0 and conv_w.shape[0] <= HALO
    assert all(v % LANE == 0 for v in (wr, conv_lo, dc, width_out)) and wr <= conv_lo <= w_main - dc
    body = functools.partial(_inproj_kernel, tiles_per_seq=seq // tm, wr=wr, conv_lo=conv_lo)
    n = t // tm
    return pl.pallas_call(
        body,
        grid=(n + 1,),
        in_specs=[pl.BlockSpec((tm, d), lambda i: (jnp.minimum(i, n - 1), 0)), _resident(g.shape),
                  pl.BlockSpec((None,) + w_all.shape[1:], lambda i: (layer, 0, 0), pipeline_mode=pl.Buffered(1)),
                  _resident(w_tail.shape), _resident(mu.shape), _resident(conv_w.shape),
                  _resident(conv_b.shape)],
        out_specs=pl.BlockSpec((tm, width_out), lambda i: (jnp.maximum(i - 1, 0), 0)),
        out_shape=jax.ShapeDtypeStruct((t, width_out), F32),
        scratch_shapes=[pltpu.VMEM((tm, d), BF16), pltpu.VMEM((HALO, wr + dc), F32),
                        pltpu.VMEM((tm, width_out), F32)],
        compiler_params=pltpu.CompilerParams(
            dimension_semantics=("arbitrary",), vmem_limit_bytes=VMEM_LIMIT),
        name="in_proj",
    )(x, g, w_all, w_tail, mu, conv_w, conv_b)


def _mixer_call(body, p, col_block, width_in, width_out, ops, scratch, batch, tile_rows, n_seq, name):
    t = p.shape[0]
    seq = t // batch
    rows = tile_rows // n_seq
    assert t == batch * seq and batch % n_seq == 0 and seq % rows == 0 and rows % CHUNK == 0
    out = pl.pallas_call(
        body,
        grid=(batch // n_seq, seq // rows),
        in_specs=[pl.BlockSpec((n_seq, rows, width_in), lambda b, l: (b, l, col_block))]
        + [_resident(o.shape) for o in ops],
        out_specs=pl.BlockSpec((n_seq, rows, width_out), lambda b, l: (b, l, 0)),
        out_shape=jax.ShapeDtypeStruct((batch, seq, width_out), BF16),
        scratch_shapes=scratch,
        compiler_params=pltpu.CompilerParams(
            dimension_semantics=("arbitrary", "arbitrary"), vmem_limit_bytes=VMEM_LIMIT),
        name=name,
    )(p.reshape(batch, seq, p.shape[1]), *ops)
    return out.reshape(t, width_out)


def _rwkv_kernel(p_ref, w0_ref, w2_ref, a0_ref, a2_ref, g2_ref, kk_ref, ka_ref, rk_ref,
                 gng_ref, gnb_ref, tri2_ref, bdm_ref, bdm32_ref, trm_ref, eye_ref,
                 o_ref,
                 zt_scr, k_scr, a_scr, b_scr, ld_scr, y_scr, bonus_scr, gate_scr):
    n_seq, rows_seq, dr = o_ref.shape
    tl = n_seq * rows_seq

    @pl.when(pl.program_id(1) == 0)
    def _():
        zt_scr[...] = jnp.zeros_like(zt_scr)

    cols_of_p = lambda lo, hi: p_ref[:, :, lo:hi].reshape(tl, hi - lo)
    r = cols_of_p(0, dr)
    k = cols_of_p(dr, 2 * dr)
    v = cols_of_p(2 * dr, 3 * dr)
    wa = cols_of_p(3 * dr, 3 * dr + LANE)
    gd = cols_of_p(3 * dr + LANE, 3 * dr + 2 * LANE)

    wpre = w0_ref[...] + _dot(jnp.tanh(wa).astype(BF16), w2_ref[...])
    ld_scr[...] = jax.nn.sigmoid(wpre) * (-jnp.exp(-0.5))
    iclr = jax.nn.sigmoid(a0_ref[...] + _dot(wa.astype(BF16), a2_ref[...]))
    gate_scr[...] = _dot(jax.nn.sigmoid(gd).astype(BF16), g2_ref[...])

    kkv = k * kk_ref[...]
    kkn = kkv * lax.rsqrt(jnp.maximum(_head_sum(kkv * kkv, bdm_ref), 1e-24))
    kmod = k * (1.0 + (iclr - 1.0) * ka_ref[...])
    k_scr[...] = kmod
    a_scr[...] = -kkn
    b_scr[...] = kkn * iclr
    bonus_scr[...] = _head_sum(r * kmod * rk_ref[...], bdm_ref) * v

    bdmask32 = bdm32_ref[...] > 0.5
    trmask = trm_ref[...] > 0.5
    bd = lambda xb: _block_diag(xb, bdm_ref[...])
    n_q = dr // BLOCK

    probs = [(slice(s * rows_seq + c * CHUNK, s * rows_seq + (c + 1) * CHUNK),
              slice(q * BLOCK, (q + 1) * BLOCK), s * n_q + q)
             for c in range(rows_seq // CHUNK) for s in range(n_seq) for q in range(n_q)]
    tile = lambda scr: [scr[rows, cols] for rows, cols, _ in probs]
    from_p = lambda col0: [p_ref[s, c * CHUNK:(c + 1) * CHUNK, col0 + q * BLOCK:col0 + (q + 1) * BLOCK]
                           for c in range(rows_seq // CHUNK) for s in range(n_seq) for q in range(n_q)]

    ld = tile(ld_scr)
    cum = _each(lambda x: _dot(tri2_ref[...], _split2_rows(x)), ld)
    b_c = tile(b_scr)
    k_c = tile(k_scr)
    vb = _each(lambda x: x.astype(BF16), from_p(2 * dr))
    atb = _each(lambda a, cm, l: (a * jnp.exp(cm - l)).astype(BF16), tile(a_scr), cum, ld)
    rtb = _each(lambda x, cm: (x * jnp.exp(cm)).astype(BF16), from_p(0), cum)
    inv = _each(lambda cm: jnp.exp(-cm), cum)
    g_end = _each(lambda cm: jnp.exp(cm[CHUNK - 1:CHUNK, :]), cum)
    to_end = _each(lambda cm: jnp.exp(cm[CHUNK - 1:CHUNK, :] - cm), cum)
    ar = _each(lambda a, x: jnp.concatenate([a, x], axis=0), atb, rtb)
    masked = lambda x: jnp.where(trmask, x, 0.0)
    sb32 = _each(lambda a, b, i: masked(_dot_nt(a, bd((b * i).astype(BF16)))), ar, b_c, inv)
    sk = _each(lambda a, x, i: masked(_dot_nt(a, bd((x * i).astype(BF16)))).astype(BF16), ar, k_c, inv)
    kv = _each(lambda x, y: _dot(x, bd(y)), sk, vb)
    t_inv = _each(lambda x: eye_ref[...] + x[0:CHUNK], sb32)
    pw = _each(lambda x: x[0:CHUNK].astype(BF16), sb32)
    pw = _each(lambda x: _dot(x, bd(x)).astype(BF16), pw)
    for i in range(1, 5):
        res = _each(lambda t, x: _dot(jnp.concatenate([t.astype(BF16), x], axis=0), bd(x)), t_inv, pw)
        t_inv = _each(lambda t, x: t + x[0:CHUNK], t_inv, res)
        pw = _each(lambda x: x[CHUNK:].astype(BF16), res)
    t_inv = _each(lambda t, x: t + _dot(t.astype(BF16), bd(x)), t_inv, pw)
    xx = _each(lambda t, a, x: _dot(t.astype(BF16),
                                    jnp.concatenate([bd(a), bd(x[0:CHUNK].astype(BF16))], axis=1)),
               t_inv, atb, kv)
    x1b = _each(lambda x: x[:, 0:BLOCK].astype(BF16), xx)
    x2 = _each(lambda x: x[:, BLOCK:], xx)
    bgb = _each(lambda b, e: (b * e).astype(BF16), b_c, to_end)
    kgb = _each(lambda x, e: (x * e).astype(BF16), k_c, to_end)
    p_t = _each(lambda x, b: _dot_tn(x, b).astype(BF16), x1b, bgb)
    c_t = _each(lambda x, y, b, kg: _dot_tn(jnp.concatenate([x.astype(BF16), y], axis=0),
                                            jnp.concatenate([b, kg], axis=0)), x2, vb, bgb, kgb)

    zs = [zt_scr[j] for j in range(n_seq * n_q)]
    zb = []
    for (_, _, j), g, pt, ct in zip(probs, g_end, p_t, c_t):
        zb.append(zs[j].astype(BF16))
        zs[j] = zs[j] * g + jnp.where(bdmask32, _dot(zb[-1], pt) + ct, 0.0)
    for j in range(n_seq * n_q):
        zt_scr[j] = zs[j]
    xz = _each(lambda x, rt, z: _dot_nt(jnp.concatenate([x, rt], axis=0), z), x1b, rtb, zb)
    ubd = _each(lambda m, x: bd((m[0:CHUNK] + x).astype(BF16)), xz, x2)
    for (rows, cols, _), m, n_rb, u, o in zip(probs, xz, sb32, ubd, kv):
        y_scr[rows, cols] = m[CHUNK:] + _dot(n_rb[CHUNK:].astype(BF16), u) + o[CHUNK:]

    y = y_scr[...]
    inv_n = 1.0 / HEAD_DIM
    mean = _head_sum(y, bdm_ref) * inv_n
    yc = y - mean
    var = _head_sum(yc * yc, bdm_ref) * inv_n
    yn = yc * lax.rsqrt(var + RWKV_GN_EPS) * gng_ref[...] + gnb_ref[...]
    o_ref[...] = ((yn + bonus_scr[...]) * gate_scr[...]).astype(BF16).reshape(n_seq, rows_seq, dr)


RWKV_SEQS = 2


def _rwkv(p, batch, params, consts):
    dr = params["w0"].shape[1]
    wp = 3 * dr + 2 * LANE
    assert dr % BLOCK == 0
    names = ["w0", "w2", "a0", "a2", "g2", "k_k", "k_a", "r_k", "gn_g", "gn_b"]
    cnames = ["tri2", "bdm", "bdm32", "trm", "eye"]
    ops = [params[n] for n in names] + [consts[n] for n in cnames]
    tl = MIX_ROWS
    scratch = ([pltpu.VMEM((RWKV_SEQS * (dr // BLOCK), BLOCK, BLOCK), F32)]
               + [pltpu.VMEM((tl, dr), F32) for _ in range(7)])
    return _mixer_call(_rwkv_kernel, p, 0, wp, dr, ops, scratch, batch, tl, RWKV_SEQS, "rwkv7")


def _ssd_kernel(p_ref, dtb_ref, alog_ref, dsk_ref, ng_ref, exp_ref,
                tri2_ref, ones_ref, bdm_ref, tri_ref, triu_ref,
                o_ref,
                st_scr, xs_scr, xdt_scr, a_scr, y_scr):
    p_ref, o_ref = p_ref.at[0], o_ref.at[0]
    tl, ds = o_ref.shape
    n_grp, d_state, _ = st_scr.shape
    dc = ds + 2 * n_grp * d_state

    @pl.when(pl.program_id(1) == 0)
    def _():
        st_scr[...] = jnp.zeros_like(st_scr)

    xs = p_ref[:, ds:2 * ds]
    dtr = p_ref[:, ds + dc:ds + dc + LANE] + dtb_ref[...]
    dt_c = jnp.maximum(dtr, 0.0) + jnp.log1p(jnp.exp(-jnp.abs(dtr)))
    dt = _dot(_split2_cols(dt_c), exp_ref[...])
    xs_scr[...] = xs
    xdt_scr[...] = xs * dt
    a_scr[...] = dt * (-jnp.exp(alog_ref[...]))
    bc0 = 2 * ds

    incl = tri_ref[...] > 0.5
    upper = triu_ref[...] > 0.5

    probs = [(slice(c * CHUNK, (c + 1) * CHUNK), g) for c in range(tl // CHUNK) for g in range(n_grp)]
    gcols = lambda g: slice(g * BLOCK, (g + 1) * BLOCK)

    a_c = [a_scr[rows, gcols(g)] for rows, g in probs]
    e1 = _each(lambda a: _dot(tri2_ref[...], _split2_rows(a)), a_c)
    e2 = _each(lambda a: _dot(ones_ref[...], _split2_rows(jnp.where(upper, a, 0.0))), a_c)
    l_w = _each(lambda x, y: jnp.exp(jnp.where(incl, x - y, -jnp.inf)), e1, e2)
    b_g = [p_ref[rows, bc0 + g * d_state:bc0 + (g + 1) * d_state].astype(BF16) for rows, g in probs]
    c_g = [p_ref[rows, bc0 + (n_grp + g) * d_state:bc0 + (n_grp + g + 1) * d_state].astype(BF16)
           for rows, g in probs]
    scores = _each(lambda c, b: _dot_nt(c, jnp.concatenate([b, b, b, b], axis=0)), c_g, b_g)
    xdt_c = [xdt_scr[rows, gcols(g)] for rows, g in probs]
    y_diag = _each(lambda x, l, y: _dot((x * l).astype(BF16), _block_diag(y.astype(BF16), bdm_ref[...])),
                   scores, l_w, xdt_c)
    new = _each(lambda b, x, e: _dot_tn(b, (x * jnp.exp(e[CHUNK - 1:CHUNK, :] - e)).astype(BF16)),
                b_g, xdt_c, e1)

    sts = [st_scr[g] for g in range(n_grp)]
    stb = []
    for (_, g), e, n in zip(probs, e1, new):
        stb.append(sts[g].astype(BF16))
        sts[g] = sts[g] * jnp.exp(e[CHUNK - 1:CHUNK, :]) + n
    for g in range(n_grp):
        st_scr[g] = sts[g]
    for (rows, g), c, st, yd, e in zip(probs, c_g, stb, y_diag, e1):
        y_scr[rows, gcols(g)] = yd + _dot(c, st) * jnp.exp(e)

    z = p_ref[:, 0:ds]
    y = (y_scr[...] + dsk_ref[...] * xs_scr[...]) * (z * jax.nn.sigmoid(z))
    o_ref[...] = _rms(y, ng_ref[...], SSM_NORM_EPS).astype(BF16)


def _ssd(p, batch, params, consts, n_grp, d_state, wp):
    ds = params["norm_g"].shape[1]
    dc = ds + 2 * n_grp * d_state
    assert ds == n_grp * BLOCK and ds + dc + LANE <= wp
    names = ["dt_bias", "a_log", "d_skip", "norm_g", "expand"]
    cnames = ["tri2", "ones", "bdm", "tri", "triu"]
    ops = [params[n] for n in names] + [consts[n] for n in cnames]
    tl = SSD_ROWS
    tile = lambda: pltpu.VMEM((tl, ds), F32)
    scratch = [pltpu.VMEM((n_grp, d_state, BLOCK), F32), tile(), tile(), tile(), tile()]
    return _mixer_call(_ssd_kernel, p, 1, wp, ds, ops, scratch, batch, tl, 1, "ssd")


def _mask_consts():
    row = jnp.arange(BLOCK)[:, None]
    col = jnp.arange(BLOCK)[None, :]
    t = jnp.arange(CHUNK)[:, None]
    s = col % CHUNK
    tri64 = (jnp.arange(CHUNK)[None, :] <= t)
    bdm = row // CHUNK == col // CHUNK
    return {
        "bdm": bdm.astype(BF16),
        "bdm32": bdm.astype(F32),
        "trm": jnp.concatenate([s < t, s <= t], axis=0).astype(F32),
        "tri": (s <= t).astype(F32),
        "triu": (s >= t).astype(F32),
        "eye": (s == t).astype(F32),
        "tri2": jnp.concatenate([tri64, tri64], axis=1).astype(BF16),
        "ones": jnp.ones((CHUNK, 2 * CHUNK), BF16),
    }


def kernel(x, norm_ffn1, ffn1_w_gate, ffn1_w_up, ffn1_w_down, norm_mix, w_in, rwkv_mu, rwkv_w0, rwkv_w2,
           rwkv_a0, rwkv_a2, rwkv_g2, rwkv_k_k, rwkv_k_a, rwkv_r_k, rwkv_gn_g, rwkv_gn_b, ssm_conv_w,
           ssm_conv_b, ssm_dt_bias, ssm_a_log, ssm_d, ssm_norm, w_out, norm_ffn2, ffn2_w_gate, ffn2_w_up,
           ffn2_w_down, norm_final):
    batch, seq, d = x.shape
    depth = norm_ffn1.shape[0]
    dr = rwkv_w0.shape[1]
    ds = ssm_norm.shape[1]
    dc = ssm_conv_w.shape[2]
    n_heads_s = ssm_a_log.shape[1]
    lora_w = rwkv_w2.shape[1]
    lora_a = rwkv_a2.shape[1]
    lora_g = rwkv_g2.shape[1]
    n_grp = ds // BLOCK
    d_state = (dc - ds) // (2 * n_grp)
    wp = 3 * dr + 2 * LANE
    assert ds // n_heads_s == HEAD_DIM and lora_w + lora_a == LANE and lora_g == LANE
    assert n_heads_s <= LANE and ds + dc + LANE <= wp and seq % SSD_ROWS == 0

    consts = _mask_consts()
    lane_head = jnp.arange(LANE)[:, None] == (jnp.arange(ds) // HEAD_DIM)[None, :]
    expand = jnp.concatenate([lane_head, lane_head], axis=0).astype(BF16)
    row = lambda a: a.reshape(1, -1)
    rep = lambda a: jnp.repeat(a, HEAD_DIM).reshape(1, -1)

    w_in_b = w_in.astype(BF16)
    n_main = w_in.shape[2] // LANE * LANE

    xt = x.reshape(batch * seq, d)
    for i in range(depth):
        bf = lambda w: w[i].astype(BF16)
        x1 = _ffn1(xt, row(norm_ffn1[i]), bf(ffn1_w_gate), bf(ffn1_w_up), bf(ffn1_w_down))

        w_tail = jnp.pad(w_in[i][:, n_main:], ((0, 0), (0, 2 * wp - w_in.shape[2]))).astype(BF16)
        p = _inproj(x1, row(norm_mix[i]), w_in_b, i, w_tail, row(rwkv_mu[i]), ssm_conv_w[i],
                    row(ssm_conv_b[i]), seq, wp, wp + ds)

        rwkv_params = {
            "w0": row(rwkv_w0[i]), "a0": row(rwkv_a0[i]),
            "w2": jnp.pad(rwkv_w2[i], ((0, lora_a), (0, 0))).astype(BF16),
            "a2": jnp.pad(rwkv_a2[i], ((lora_w, 0), (0, 0))).astype(BF16),
            "g2": rwkv_g2[i].astype(BF16),
            "k_k": row(rwkv_k_k[i]), "k_a": row(rwkv_k_a[i]), "r_k": row(rwkv_r_k[i]),
            "gn_g": row(rwkv_gn_g[i]), "gn_b": row(rwkv_gn_b[i]),
        }
        y_r = _rwkv(p, batch, rwkv_params, consts)

        ssd_params = {
            "dt_bias": jnp.pad(ssm_dt_bias[i], (0, LANE - n_heads_s)).reshape(1, -1),
            "a_log": rep(ssm_a_log[i]), "d_skip": rep(ssm_d[i]), "norm_g": row(ssm_norm[i]),
            "expand": expand,
        }
        y_s = _ssd(p, batch, ssd_params, consts, n_grp, d_state, wp)

        wo = w_out[i].astype(BF16)
        assert i == depth - 1, "the fused final norm assumes a single layer"
        xt = _ffn2(x1, y_r, y_s, wo[:dr], wo[dr:], row(norm_ffn2[i]), bf(ffn2_w_gate), bf(ffn2_w_up),
                   bf(ffn2_w_down), row(norm_final))
    return xt.reshape(batch, seq, d)
```

```python
import functools

import jax
import jax.numpy as jnp
from jax import lax
from jax.experimental import pallas as pl
from jax.experimental.pallas import tpu as pltpu

F32 = jnp.float32
BF16 = jnp.bfloat16

NORM_EPS = 1e-6
RWKV_GN_EPS = 64e-5
SSM_NORM_EPS = 1e-5

HEAD_DIM = 64
CHUNK = 64
BLOCK = 256
LANE = 128
HALO = 8
MIX_ROWS = 512

NT_DIMS = (((1,), (1,)), ((), ()))
TN_DIMS = (((0,), (0,)), ((), ()))

VMEM_LIMIT = 56 * 1024 * 1024


def _dot(a, b):
    return jnp.dot(a, b, preferred_element_type=F32)


def _dot_nt(a, b):
    return lax.dot_general(a, b, NT_DIMS, preferred_element_type=F32)


def _dot_tn(a, b):
    return lax.dot_general(a, b, TN_DIMS, preferred_element_type=F32)


def _rms(x, g, eps):
    return x * lax.rsqrt(jnp.mean(x * x, axis=-1, keepdims=True) + eps) * g


def _split2_rows(x):
    hi = x.astype(BF16)
    lo = (x - hi.astype(F32)).astype(BF16)
    return jnp.concatenate([hi, lo], axis=0)


def _split2_cols(x):
    hi = x.astype(BF16)
    lo = (x - hi.astype(F32)).astype(BF16)
    return jnp.concatenate([hi, lo], axis=1)


def _head_sum(x, bdm_ref):
    xb = x.astype(BF16)
    return jnp.concatenate([_dot(xb[:, c:c + BLOCK], bdm_ref[...]) for c in range(0, x.shape[1], BLOCK)],
                           axis=1)


def _block_diag(xb, bdm):
    return jnp.concatenate([xb, xb, xb, xb], axis=0) * bdm


def _each(f, *lists):
    return [f(*args) for args in zip(*lists)]


def _resident(shape):
    zeros = (0,) * len(shape)
    return pl.BlockSpec(shape, lambda *_: zeros, pipeline_mode=pl.Buffered(1))


FF_CHUNK = 256
FFN_ROWS = 512


def _norm_split(x, g, h_scr, r_scr):
    h_scr[...] = (x * g).astype(BF16)
    r = lax.rsqrt(jnp.mean(x * x, axis=-1, keepdims=True) + NORM_EPS)
    r_scr[...] = jnp.broadcast_to(r, r_scr.shape)


def _swiglu(h_scr, r_scr, wg_ref, wu_ref, wd_ref, act_scr):
    d_ff = wg_ref.shape[1]
    for c in range(d_ff // FF_CHUNK):
        cs = slice(c * FF_CHUNK, (c + 1) * FF_CHUNK)
        r = jnp.concatenate([r_scr[...]] * (FF_CHUNK // LANE), axis=1)
        gate = _dot(h_scr[...], wg_ref[:, cs]) * r
        up = _dot(h_scr[...], wu_ref[:, cs]) * r
        act_scr[:, cs] = (gate * jax.nn.sigmoid(gate) * up).astype(BF16)
    return _dot(act_scr[...], wd_ref[...])


def _ffn1_kernel(x_ref, g_ref, wg_ref, wu_ref, wd_ref, o_ref, h_scr, r_scr, act_scr):
    _norm_split(x_ref[...], g_ref[...], h_scr, r_scr)
    o_ref[...] = x_ref[...] + 0.5 * _swiglu(h_scr, r_scr, wg_ref, wu_ref, wd_ref, act_scr)


def _ffn2_kernel(x_ref, yr_ref, ys_ref, wor_ref, wos_ref, g_ref, wg_ref, wu_ref, wd_ref, gf_ref,
                 o_ref, h_scr, r_scr, act_scr, x_scr):
    x2 = x_ref[...] + _dot(yr_ref[...], wor_ref[...]) + _dot(ys_ref[...], wos_ref[...])
    x_scr[...] = x2
    _norm_split(x2, g_ref[...], h_scr, r_scr)
    y = _swiglu(h_scr, r_scr, wg_ref, wu_ref, wd_ref, act_scr)
    o_ref[...] = _rms(x_scr[...] + 0.5 * y, gf_ref[...], NORM_EPS)


def _ffn1(x, g, wg, wu, wd):
    t, d = x.shape
    d_ff = wg.shape[1]
    tm = FFN_ROWS
    assert t % tm == 0 and d_ff % FF_CHUNK == 0
    row = lambda i: (i, 0)
    return pl.pallas_call(
        _ffn1_kernel,
        grid=(t // tm,),
        in_specs=[pl.BlockSpec((tm, d), row), _resident(g.shape), _resident(wg.shape), _resident(wu.shape),
                  _resident(wd.shape)],
        out_specs=pl.BlockSpec((tm, d), row),
        out_shape=jax.ShapeDtypeStruct((t, d), F32),
        scratch_shapes=[pltpu.VMEM((tm, d), BF16), pltpu.VMEM((tm, LANE), F32), pltpu.VMEM((tm, d_ff), BF16)],
        compiler_params=pltpu.CompilerParams(
            dimension_semantics=("arbitrary",), vmem_limit_bytes=VMEM_LIMIT),
        name="ffn1",
    )(x, g, wg, wu, wd)


def _ffn2(x, yr, ys, wor, wos, g, wg, wu, wd, gf):
    t, d = x.shape
    d_ff = wg.shape[1]
    dm = yr.shape[1]
    tm = FFN_ROWS
    assert t % tm == 0 and d_ff % FF_CHUNK == 0
    row = lambda i: (i, 0)
    return pl.pallas_call(
        _ffn2_kernel,
        grid=(t // tm,),
        in_specs=[pl.BlockSpec((tm, d), row), pl.BlockSpec((tm, dm), row), pl.BlockSpec((tm, dm), row),
                  _resident(wor.shape), _resident(wos.shape), _resident(g.shape), _resident(wg.shape),
                  _resident(wu.shape), _resident(wd.shape), _resident(gf.shape)],
        out_specs=pl.BlockSpec((tm, d), row),
        out_shape=jax.ShapeDtypeStruct((t, d), F32),
        scratch_shapes=[pltpu.VMEM((tm, d), BF16), pltpu.VMEM((tm, LANE), F32), pltpu.VMEM((tm, d_ff), BF16),
                        pltpu.VMEM((tm, d), F32)],
        compiler_params=pltpu.CompilerParams(
            dimension_semantics=("arbitrary",), vmem_limit_bytes=VMEM_LIMIT),
        name="ffn2",
    )(x, yr, ys, wor, wos, g, wg, wu, wd, gf)


IN_CHUNK = 512


def _col_chunks(lo, hi):
    return [(c0, min(IN_CHUNK, hi - c0)) for c0 in range(lo, hi, IN_CHUNK)]


def _inproj_kernel(x_ref, g_ref, w_ref, wt_ref, mu_ref, cw_ref, cb_ref, p_ref, h_scr, ext_scr, raw_scr,
                   *, tiles_per_seq, wr, conv_lo):
    tm = x_ref.shape[0]
    dc = cw_ref.shape[1]
    width = cw_ref.shape[0]
    w_main = p_ref.shape[1] - wt_ref.shape[1]
    step = pl.program_id(0)
    h_scr[...] = _rms(x_ref[...], g_ref[...], NORM_EPS).astype(BF16)

    @pl.when(step == 0)
    def _():
        raw_scr[...] = jnp.zeros_like(raw_scr)

    @pl.when(lax.rem(step + tiles_per_seq - 1, tiles_per_seq) == 0)
    def _():
        ext_scr[...] = jnp.zeros_like(ext_scr)

    def behind_halo(cols, pc):
        ext = jnp.concatenate([ext_scr[:, cols], pc], axis=0)
        ext_scr[:, cols] = pc[tm - HALO:, :]
        return ext

    def lerp(c0, n, pc):
        cs = slice(c0, c0 + n)
        shifted = pltpu.roll(behind_halo(cs, pc), 1, axis=0)[HALO:, :]
        p_ref[:, cs] = pc + (shifted - pc) * mu_ref[:, cs]

    def conv_silu(c0, n, pc):
        j = c0 - conv_lo
        cw = slice(j, j + n)
        ext = behind_halo(slice(wr + j, wr + j + n), pc)
        tap = lambda i: cw_ref[width - 1 - i:width - i, cw]
        if width == 4:
            ext1 = pltpu.roll(ext, 1, axis=0)
            far = pltpu.roll(ext * tap(2) + ext1 * tap(3), 2, axis=0)[HALO:, :]
            conv = pc * tap(0) + ext1[HALO:, :] * tap(1) + far + cb_ref[:, cw]
        else:
            conv = pc * tap(0) + cb_ref[:, cw]
            for i in range(1, width):
                conv = conv + pltpu.roll(ext, i, axis=0)[HALO:, :] * tap(i)
        p_ref[:, c0:c0 + n] = conv * jax.nn.sigmoid(conv)

    def plain(c0, n, pc):
        p_ref[:, c0:c0 + n] = pc

    convs = [(conv_silu, c) for c in _col_chunks(conv_lo, conv_lo + dc)]
    light = ([(lerp, c) for c in _col_chunks(0, wr)]
             + [(plain, c) for c in _col_chunks(wr, conv_lo) + _col_chunks(conv_lo + dc, w_main)])
    per_conv = len(light) // max(len(convs), 1)
    order = []
    for n, cv in enumerate(convs):
        order += [cv] + light[n * per_conv:(n + 1) * per_conv]
    order += light[len(convs) * per_conv:]
    order.append((plain, (w_main, wt_ref.shape[1])))
    for finish, (c0, n) in order:
        before = raw_scr[:, c0:c0 + n]
        raw_scr[:, c0:c0 + n] = _dot(h_scr[...], w_ref[:, c0:c0 + n] if c0 < w_main else wt_ref[...])
        finish(c0, n, before)


def _inproj(x, g, w_all, layer, w_tail, mu, conv_w, conv_b, seq, wr, conv_lo):
    t, d = x.shape
    tm = 512
    dc = conv_w.shape[1]
    w_main = w_all.shape[2] // LANE * LANE
    width_out = w_main + w_tail.shape[1]
    assert t % tm == 0 and seq % tm == 0 and conv_w.shape[0] <= HALO
    assert all(v % LANE == 0 for v in (wr, conv_lo, dc, width_out)) and wr <= conv_lo <= w_main - dc
    body = functools.partial(_inproj_kernel, tiles_per_seq=seq // tm, wr=wr, conv_lo=conv_lo)
    n = t // tm
    return pl.pallas_call(
        body,
        grid=(n + 1,),
        in_specs=[pl.BlockSpec((tm, d), lambda i: (jnp.minimum(i, n - 1), 0)), _resident(g.shape),
                  pl.BlockSpec((None,) + w_all.shape[1:], lambda i: (layer, 0, 0), pipeline_mode=pl.Buffered(1)),
                  _resident(w_tail.shape), _resident(mu.shape), _resident(conv_w.shape),
                  _resident(conv_b.shape)],
        out_specs=pl.BlockSpec((tm, width_out), lambda i: (jnp.maximum(i - 1, 0), 0)),
        out_shape=jax.ShapeDtypeStruct((t, width_out), F32),
        scratch_shapes=[pltpu.VMEM((tm, d), BF16), pltpu.VMEM((HALO, wr + dc), F32),
                        pltpu.VMEM((tm, width_out), F32)],
        compiler_params=pltpu.CompilerParams(
            dimension_semantics=("arbitrary",), vmem_limit_bytes=VMEM_LIMIT),
        name="in_proj",
    )(x, g, w_all, w_tail, mu, conv_w, conv_b)


def _rwkv_kernel(p_ref, w0_ref, w2_ref, a0_ref, a2_ref, g2_ref, kk_ref, ka_ref, rk_ref,
                 gng_ref, gnb_ref, tri2_ref, bdm_ref, bdm32_ref, trm_ref, eye_ref,
                 o_ref,
                 zt_scr, k_scr, a_scr, b_scr, ld_scr, y_scr, bonus_scr, gate_scr):
    n_seq, rows_seq, dr = o_ref.shape
    tl = n_seq * rows_seq

    @pl.when(pl.program_id(1) == 0)
    def _():
        zt_scr[...] = jnp.zeros_like(zt_scr)

    cols_of_p = lambda lo, hi: p_ref[:, :, lo:hi].reshape(tl, hi - lo)
    r = cols_of_p(0, dr)
    k = cols_of_p(dr, 2 * dr)
    v = cols_of_p(2 * dr, 3 * dr)
    wa = cols_of_p(3 * dr, 3 * dr + LANE)
    gd = cols_of_p(3 * dr + LANE, 3 * dr + 2 * LANE)

    wpre = w0_ref[...] + _dot(jnp.tanh(wa).astype(BF16), w2_ref[...])
    ld_scr[...] = jax.nn.sigmoid(wpre) * (-jnp.exp(-0.5))
    iclr = jax.nn.sigmoid(a0_ref[...] + _dot(wa.astype(BF16), a2_ref[...]))
    gate_scr[...] = _dot(jax.nn.sigmoid(gd).astype(BF16), g2_ref[...])

    kkv = k * kk_ref[...]
    kkn = kkv * lax.rsqrt(jnp.maximum(_head_sum(kkv * kkv, bdm_ref), 1e-24))
    kmod = k * (1.0 + (iclr - 1.0) * ka_ref[...])
    k_scr[...] = kmod
    a_scr[...] = -kkn
    b_scr[...] = kkn * iclr
    bonus_scr[...] = _head_sum(r * kmod * rk_ref[...], bdm_ref) * v

    bdmask32 = bdm32_ref[...] > 0.5
    trmask = trm_ref[...] > 0.5
    bd = lambda xb: _block_diag(xb, bdm_ref[...])
    n_q = dr // BLOCK

    probs = [(slice(s * rows_seq + c * CHUNK, s * rows_seq + (c + 1) * CHUNK),
              slice(q * BLOCK, (q + 1) * BLOCK), s * n_q + q)
             for c in range(rows_seq // CHUNK) for s in range(n_seq) for q in range(n_q)]
    tile = lambda scr: [scr[rows, cols] for rows, cols, _ in probs]
    from_p = lambda col0: [p_ref[s, c * CHUNK:(c + 1) * CHUNK, col0 + q * BLOCK:col0 + (q + 1) * BLOCK]
                           for c in range(rows_seq // CHUNK) for s in range(n_seq) for q in range(n_q)]

    ld = tile(ld_scr)
    cum = _each(lambda x: _dot(tri2_ref[...], _split2_rows(x)), ld)
    b_c = tile(b_scr)
    k_c = tile(k_scr)
    vb = _each(lambda x: x.astype(BF16), from_p(2 * dr))
    atb = _each(lambda a, cm, l: (a * jnp.exp(cm - l)).astype(BF16), tile(a_scr), cum, ld)
    rtb = _each(lambda x, cm: (x * jnp.exp(cm)).astype(BF16), from_p(0), cum)
    inv = _each(lambda cm: jnp.exp(-cm), cum)
    g_end = _each(lambda cm: jnp.exp(cm[CHUNK - 1:CHUNK, :]), cum)
    to_end = _each(lambda cm: jnp.exp(cm[CHUNK - 1:CHUNK, :] - cm), cum)
    ar = _each(lambda a, x: jnp.concatenate([a, x], axis=0), atb, rtb)
    masked = lambda x: jnp.where(trmask, x, 0.0)
    sb32 = _each(lambda a, b, i: masked(_dot_nt(a, bd((b * i).astype(BF16)))), ar, b_c, inv)
    sk = _each(lambda a, x, i: masked(_dot_nt(a, bd((x * i).astype(BF16)))).astype(BF16), ar, k_c, inv)
    kv = _each(lambda x, y: _dot(x, bd(y)), sk, vb)
    t_inv = _each(lambda x: eye_ref[...] + x[0:CHUNK], sb32)
    pw = _each(lambda x: x[0:CHUNK].astype(BF16), sb32)
    pw = _each(lambda x: _dot(x, bd(x)).astype(BF16), pw)
    for i in range(1, 5):
        res = _each(lambda t, x: _dot(jnp.concatenate([t.astype(BF16), x], axis=0), bd(x)), t_inv, pw)
        t_inv = _each(lambda t, x: t + x[0:CHUNK], t_inv, res)
        pw = _each(lambda x: x[CHUNK:].astype(BF16), res)
    t_inv = _each(lambda t, x: t + _dot(t.astype(BF16), bd(x)), t_inv, pw)
    xx = _each(lambda t, a, x: _dot(t.astype(BF16),
                                    jnp.concatenate([bd(a), bd(x[0:CHUNK].astype(BF16))], axis=1)),
               t_inv, atb, kv)
    x1b = _each(lambda x: x[:, 0:BLOCK].astype(BF16), xx)
    x2 = _each(lambda x: x[:, BLOCK:], xx)
    bgb = _each(lambda b, e: (b * e).astype(BF16), b_c, to_end)
    kgb = _each(lambda x, e: (x * e).astype(BF16), k_c, to_end)
    p_t = _each(lambda x, b: _dot_tn(x, b).astype(BF16), x1b, bgb)
    c_t = _each(lambda x, y, b, kg: _dot_tn(jnp.concatenate([x.astype(BF16), y], axis=0),
                                            jnp.concatenate([b, kg], axis=0)), x2, vb, bgb, kgb)

    zs = [zt_scr[j] for j in range(n_seq * n_q)]
    zb = []
    for (_, _, j), g, pt, ct in zip(probs, g_end, p_t, c_t):
        zb.append(zs[j].astype(BF16))
        zs[j] = zs[j] * g + jnp.where(bdmask32, _dot(zb[-1], pt) + ct, 0.0)
    for j in range(n_seq * n_q):
        zt_scr[j] = zs[j]
    xz = _each(lambda x, rt, z: _dot_nt(jnp.concatenate([x, rt], axis=0), z), x1b, rtb, zb)
    ubd = _each(lambda m, x: bd((m[0:CHUNK] + x).astype(BF16)), xz, x2)
    for (rows, cols, _), m, n_rb, u, o in zip(probs, xz, sb32, ubd, kv):
        y_scr[rows, cols] = m[CHUNK:] + _dot(n_rb[CHUNK:].astype(BF16), u) + o[CHUNK:]

    y = y_scr[...]
    inv_n = 1.0 / HEAD_DIM
    mean = _head_sum(y, bdm_ref) * inv_n
    yc = y - mean
    var = _head_sum(yc * yc, bdm_ref) * inv_n
    yn = yc * lax.rsqrt(var + RWKV_GN_EPS) * gng_ref[...] + gnb_ref[...]
    o_ref[...] = ((yn + bonus_scr[...]) * gate_scr[...]).astype(BF16).reshape(n_seq, rows_seq, dr)


def _ssd_kernel(p_ref, dtb_ref, alog_ref, dsk_ref, ng_ref, exp_ref,
                tri2_ref, ones_ref, bdm_ref, tri_ref, triu_ref,
                o_ref,
                st_scr, xs_scr, xdt_scr, a_scr, y_scr):
    n_seq, rows_seq, ds = o_ref.shape
    tl = n_seq * rows_seq
    _, d_state, _ = st_scr.shape
    n_grp = ds // BLOCK
    dc = ds + 2 * n_grp * d_state

    @pl.when(pl.program_id(1) == 0)
    def _():
        st_scr[...] = jnp.zeros_like(st_scr)

    cols_of_p = lambda lo, hi: p_ref[:, :, lo:hi].reshape(tl, hi - lo)
    xs = cols_of_p(ds, 2 * ds)
    dtr = cols_of_p(ds + dc, ds + dc + LANE) + dtb_ref[...]
    dt_c = jnp.maximum(dtr, 0.0) + jnp.log1p(jnp.exp(-jnp.abs(dtr)))
    dt = _dot(_split2_cols(dt_c), exp_ref[...])
    xs_scr[...] = xs
    xdt_scr[...] = xs * dt
    a_scr[...] = dt * (-jnp.exp(alog_ref[...]))
    bc0 = 2 * ds

    incl = tri_ref[...] > 0.5
    upper = triu_ref[...] > 0.5

    order = [(s, c, g) for c in range(rows_seq // CHUNK) for s in range(n_seq) for g in range(n_grp)]
    probs = [(slice(s * rows_seq + c * CHUNK, s * rows_seq + (c + 1) * CHUNK), g, s * n_grp + g)
             for s, c, g in order]
    gcols = lambda g: slice(g * BLOCK, (g + 1) * BLOCK)
    from_p = lambda col0: [p_ref[s, c * CHUNK:(c + 1) * CHUNK, col0 + g * d_state:col0 + (g + 1) * d_state]
                           .astype(BF16) for s, c, g in order]

    a_c = [a_scr[rows, gcols(g)] for rows, g, _ in probs]
    e1 = _each(lambda a: _dot(tri2_ref[...], _split2_rows(a)), a_c)
    e2 = _each(lambda a: _dot(ones_ref[...], _split2_rows(jnp.where(upper, a, 0.0))), a_c)
    l_w = _each(lambda x, y: jnp.exp(jnp.where(incl, x - y, -jnp.inf)), e1, e2)
    b_g = from_p(bc0)
    c_g = from_p(bc0 + n_grp * d_state)
    scores = _each(lambda c, b: _dot_nt(c, jnp.concatenate([b, b, b, b], axis=0)), c_g, b_g)
    xdt_c = [xdt_scr[rows, gcols(g)] for rows, g, _ in probs]
    y_diag = _each(lambda x, l, y: _dot((x * l).astype(BF16), _block_diag(y.astype(BF16), bdm_ref[...])),
                   scores, l_w, xdt_c)
    new = _each(lambda b, x, e: _dot_tn(b, (x * jnp.exp(e[CHUNK - 1:CHUNK, :] - e)).astype(BF16)),
                b_g, xdt_c, e1)

    sts = [st_scr[j] for j in range(n_seq * n_grp)]
    stb = []
    for (_, _, j), e, n in zip(probs, e1, new):
        stb.append(sts[j].astype(BF16))
        sts[j] = sts[j] * jnp.exp(e[CHUNK - 1:CHUNK, :]) + n
    for j in range(n_seq * n_grp):
        st_scr[j] = sts[j]
    for (rows, g, _), c, st, yd, e in zip(probs, c_g, stb, y_diag, e1):
        y_scr[rows, gcols(g)] = yd + _dot(c, st) * jnp.exp(e)

    z = cols_of_p(0, ds)
    y = (y_scr[...] + dsk_ref[...] * xs_scr[...]) * (z * jax.nn.sigmoid(z))
    o_ref[...] = _rms(y, ng_ref[...], SSM_NORM_EPS).astype(BF16).reshape(n_seq, rows_seq, ds)


MIX_SEQS = 2


def _mixers(p, batch, rwkv_params, ssd_params, consts, n_grp, d_state):
    t, wp2 = p.shape
    wp = wp2 // 2
    dr = rwkv_params["w0"].shape[1]
    ds = ssd_params["norm_g"].shape[1]
    dc = ds + 2 * n_grp * d_state
    seq = t // batch
    rows = MIX_ROWS // MIX_SEQS
    assert dr % BLOCK == 0 and wp == 3 * dr + 2 * LANE and ds == n_grp * BLOCK and ds + dc + LANE <= wp
    assert t == batch * seq and batch % MIX_SEQS == 0 and seq % rows == 0 and rows % CHUNK == 0
    r_ops = ([rwkv_params[n] for n in ["w0", "w2", "a0", "a2", "g2", "k_k", "k_a", "r_k", "gn_g", "gn_b"]]
             + [consts[n] for n in ["tri2", "bdm", "bdm32", "trm", "eye"]])
    s_ops = ([ssd_params[n] for n in ["dt_bias", "a_log", "d_skip", "norm_g", "expand"]]
             + [consts[n] for n in ["tri2", "ones", "bdm", "tri", "triu"]])
    tl = MIX_ROWS
    r_scr = ([pltpu.VMEM((MIX_SEQS * (dr // BLOCK), BLOCK, BLOCK), F32)]
             + [pltpu.VMEM((tl, dr), F32) for _ in range(7)])
    s_scr = [pltpu.VMEM((MIX_SEQS * n_grp, d_state, BLOCK), F32)] + [pltpu.VMEM((tl, ds), F32) for _ in range(4)]

    def body(pr_ref, ps_ref, *refs):
        r_in, refs = refs[:len(r_ops)], refs[len(r_ops):]
        s_in, refs = refs[:len(s_ops)], refs[len(s_ops):]
        (or_ref, os_ref), scr = refs[:2], refs[2:]
        _rwkv_kernel(pr_ref, *r_in, or_ref, *scr[:len(r_scr)])
        _ssd_kernel(ps_ref, *s_in, os_ref, *scr[len(r_scr):])

    tile = lambda col: pl.BlockSpec((MIX_SEQS, rows, wp), lambda b, l: (b, l, col))
    out = lambda w: pl.BlockSpec((MIX_SEQS, rows, w), lambda b, l: (b, l, 0))
    p3 = p.reshape(batch, seq, wp2)
    y_r, y_s = pl.pallas_call(
        body,
        grid=(batch // MIX_SEQS, seq // rows),
        in_specs=[tile(0), tile(1)] + [_resident(o.shape) for o in r_ops + s_ops],
        out_specs=[out(dr), out(ds)],
        out_shape=[jax.ShapeDtypeStruct((batch, seq, dr), BF16), jax.ShapeDtypeStruct((batch, seq, ds), BF16)],
        scratch_shapes=r_scr + s_scr,
        compiler_params=pltpu.CompilerParams(
            dimension_semantics=("arbitrary", "arbitrary"), vmem_limit_bytes=VMEM_LIMIT),
        name="mixers",
    )(p3, p3, *r_ops, *s_ops)
    return y_r.reshape(t, dr), y_s.reshape(t, ds)


def _mask_consts():
    row = jnp.arange(BLOCK)[:, None]
    col = jnp.arange(BLOCK)[None, :]
    t = jnp.arange(CHUNK)[:, None]
    s = col % CHUNK
    tri64 = (jnp.arange(CHUNK)[None, :] <= t)
    bdm = row // CHUNK == col // CHUNK
    return {
        "bdm": bdm.astype(BF16),
        "bdm32": bdm.astype(F32),
        "trm": jnp.concatenate([s < t, s <= t], axis=0).astype(F32),
        "tri": (s <= t).astype(F32),
        "triu": (s >= t).astype(F32),
        "eye": (s == t).astype(F32),
        "tri2": jnp.concatenate([tri64, tri64], axis=1).astype(BF16),
        "ones": jnp.ones((CHUNK, 2 * CHUNK), BF16),
    }


def kernel(x, norm_ffn1, ffn1_w_gate, ffn1_w_up, ffn1_w_down, norm_mix, w_in, rwkv_mu, rwkv_w0, rwkv_w2,
           rwkv_a0, rwkv_a2, rwkv_g2, rwkv_k_k, rwkv_k_a, rwkv_r_k, rwkv_gn_g, rwkv_gn_b, ssm_conv_w,
           ssm_conv_b, ssm_dt_bias, ssm_a_log, ssm_d, ssm_norm, w_out, norm_ffn2, ffn2_w_gate, ffn2_w_up,
           ffn2_w_down, norm_final):
    batch, seq, d = x.shape
    depth = norm_ffn1.shape[0]
    dr = rwkv_w0.shape[1]
    ds = ssm_norm.shape[1]
    dc = ssm_conv_w.shape[2]
    n_heads_s = ssm_a_log.shape[1]
    lora_w = rwkv_w2.shape[1]
    lora_a = rwkv_a2.shape[1]
    lora_g = rwkv_g2.shape[1]
    n_grp = ds // BLOCK
    d_state = (dc - ds) // (2 * n_grp)
    wp = 3 * dr + 2 * LANE
    assert ds // n_heads_s == HEAD_DIM and lora_w + lora_a == LANE and lora_g == LANE
    assert n_heads_s <= LANE and ds + dc + LANE <= wp and seq % MIX_ROWS == 0

    consts = _mask_consts()
    lane_head = jnp.arange(LANE)[:, None] == (jnp.arange(ds) // HEAD_DIM)[None, :]
    expand = jnp.concatenate([lane_head, lane_head], axis=0).astype(BF16)
    row = lambda a: a.reshape(1, -1)
    rep = lambda a: jnp.repeat(a, HEAD_DIM).reshape(1, -1)

    w_in_b = w_in.astype(BF16)
    n_main = w_in.shape[2] // LANE * LANE

    xt = x.reshape(batch * seq, d)
    for i in range(depth):
        bf = lambda w: w[i].astype(BF16)
        x1 = _ffn1(xt, row(norm_ffn1[i]), bf(ffn1_w_gate), bf(ffn1_w_up), bf(ffn1_w_down))

        w_tail = jnp.pad(w_in[i][:, n_main:], ((0, 0), (0, 2 * wp - w_in.shape[2]))).astype(BF16)
        p = _inproj(x1, row(norm_mix[i]), w_in_b, i, w_tail, row(rwkv_mu[i]), ssm_conv_w[i],
                    row(ssm_conv_b[i]), seq, wp, wp + ds)

        rwkv_params = {
            "w0": row(rwkv_w0[i]), "a0": row(rwkv_a0[i]),
            "w2": jnp.pad(rwkv_w2[i], ((0, lora_a), (0, 0))).astype(BF16),
            "a2": jnp.pad(rwkv_a2[i], ((lora_w, 0), (0, 0))).astype(BF16),
            "g2": rwkv_g2[i].astype(BF16),
            "k_k": row(rwkv_k_k[i]), "k_a": row(rwkv_k_a[i]), "r_k": row(rwkv_r_k[i]),
            "gn_g": row(rwkv_gn_g[i]), "gn_b": row(rwkv_gn_b[i]),
        }
        ssd_params = {
            "dt_bias": jnp.pad(ssm_dt_bias[i], (0, LANE - n_heads_s)).reshape(1, -1),
            "a_log": rep(ssm_a_log[i]), "d_skip": rep(ssm_d[i]), "norm_g": row(ssm_norm[i]),
            "expand": expand,
        }
        y_r, y_s = _mixers(p, batch, rwkv_params, ssd_params, consts, n_grp, d_state)

        wo = w_out[i].astype(BF16)
        assert i == depth - 1, "the fused final norm assumes a single layer"
        xt = _ffn2(x1, y_r, y_s, wo[:dr], wo[dr:], row(norm_ffn2[i]), bf(ffn2_w_gate), bf(ffn2_w_up),
                   bf(ffn2_w_down), row(norm_final))
    return xt.reshape(batch, seq, d)
```

```python
import functools

import jax
import jax.numpy as jnp
from jax import lax
from jax.experimental import pallas as pl
from jax.experimental.pallas import tpu as pltpu

F32 = jnp.float32
BF16 = jnp.bfloat16

NORM_EPS = 1e-6
RWKV_GN_EPS = 64e-5
SSM_NORM_EPS = 1e-5

HEAD_DIM = 64
CHUNK = 64
BLOCK = 256
LANE = 128
HALO = 8
MIX_ROWS = 512

NT_DIMS = (((1,), (1,)), ((), ()))
TN_DIMS = (((0,), (0,)), ((), ()))

VMEM_LIMIT = 56 * 1024 * 1024


def _dot(a, b):
    return jnp.dot(a, b, preferred_element_type=F32)


def _dot_nt(a, b):
    return lax.dot_general(a, b, NT_DIMS, preferred_element_type=F32)


def _dot_tn(a, b):
    return lax.dot_general(a, b, TN_DIMS, preferred_element_type=F32)


def _rms(x, g, eps):
    return x * lax.rsqrt(jnp.mean(x * x, axis=-1, keepdims=True) + eps) * g


def _split2_rows(x):
    hi = x.astype(BF16)
    lo = (x - hi.astype(F32)).astype(BF16)
    return jnp.concatenate([hi, lo], axis=0)


def _split2_cols(x):
    hi = x.astype(BF16)
    lo = (x - hi.astype(F32)).astype(BF16)
    return jnp.concatenate([hi, lo], axis=1)


def _head_sum(x, bdm_ref):
    xb = x.astype(BF16)
    return jnp.concatenate([_dot(xb[:, c:c + BLOCK], bdm_ref[...]) for c in range(0, x.shape[1], BLOCK)],
                           axis=1)


def _block_diag(xb, bdm):
    return jnp.concatenate([xb, xb, xb, xb], axis=0) * bdm


def _each(f, *lists):
    return [f(*args) for args in zip(*lists)]


def _resident(shape):
    zeros = (0,) * len(shape)
    return pl.BlockSpec(shape, lambda *_: zeros, pipeline_mode=pl.Buffered(1))


FF_CHUNK = 256
FFN_ROWS = 512


def _norm_split(x, g, h_scr, r_scr):
    h_scr[...] = (x * g).astype(BF16)
    r = lax.rsqrt(jnp.mean(x * x, axis=-1, keepdims=True) + NORM_EPS)
    r_scr[...] = jnp.broadcast_to(r, r_scr.shape)


def _swiglu(h_scr, r_scr, wg_ref, wu_ref, wd_ref, act_scr):
    d_ff = wg_ref.shape[1]
    for c in range(d_ff // FF_CHUNK):
        cs = slice(c * FF_CHUNK, (c + 1) * FF_CHUNK)
        r = jnp.concatenate([r_scr[...]] * (FF_CHUNK // LANE), axis=1)
        gate = _dot(h_scr[...], wg_ref[:, cs]) * r
        up = _dot(h_scr[...], wu_ref[:, cs]) * r
        act_scr[:, cs] = (gate * jax.nn.sigmoid(gate) * up).astype(BF16)
    return _dot(act_scr[...], wd_ref[...])


STAGE_ROWS = 512


def _stage_weights(w_hbms, layer, w_scrs, stage_scr, sems):
    pieces = [(k, r0, min(STAGE_ROWS, w.shape[0] - r0))
              for k, w in enumerate(w_scrs) for r0 in range(0, w.shape[0], STAGE_ROWS)]

    def view(c):
        k, _, rows = pieces[c]
        return stage_scr.at[c % 2, pl.ds(0, rows), pl.ds(0, w_scrs[k].shape[1])]

    def dma(c):
        k, r0, rows = pieces[c]
        return pltpu.make_async_copy(w_hbms[k].at[layer, pl.ds(r0, rows), :], view(c), sems.at[c % 2])

    dma(0).start()
    for c, (k, r0, rows) in enumerate(pieces):
        if c + 1 < len(pieces):
            dma(c + 1).start()
        dma(c).wait()
        w_scrs[k][r0:r0 + rows, :] = view(c)[...].astype(BF16)


def _ffn1_kernel(x_ref, g_ref, wg_hbm, wu_hbm, wd_hbm, o_ref,
                 h_scr, r_scr, act_scr, wg_scr, wu_scr, wd_scr, stage_scr, sems, *, layer):
    @pl.when(pl.program_id(0) == 0)
    def _():
        _stage_weights((wg_hbm, wu_hbm, wd_hbm), layer, (wg_scr, wu_scr, wd_scr), stage_scr, sems)

    _norm_split(x_ref[...], g_ref[...], h_scr, r_scr)
    o_ref[...] = x_ref[...] + 0.5 * _swiglu(h_scr, r_scr, wg_scr, wu_scr, wd_scr, act_scr)


def _ffn2_kernel(x_ref, yr_ref, ys_ref, wo_hbm, g_ref, wg_hbm, wu_hbm, wd_hbm, gf_ref, o_ref,
                 h_scr, r_scr, act_scr, x_scr, wo_scr, wg_scr, wu_scr, wd_scr, stage_scr, sems, *, layer):
    @pl.when(pl.program_id(0) == 0)
    def _():
        _stage_weights((wo_hbm, wg_hbm, wu_hbm, wd_hbm), layer, (wo_scr, wg_scr, wu_scr, wd_scr),
                       stage_scr, sems)

    dr = yr_ref.shape[1]
    x2 = x_ref[...] + _dot(yr_ref[...], wo_scr[0:dr, :]) + _dot(ys_ref[...], wo_scr[dr:, :])
    x_scr[...] = x2
    _norm_split(x2, g_ref[...], h_scr, r_scr)
    y = _swiglu(h_scr, r_scr, wg_scr, wu_scr, wd_scr, act_scr)
    o_ref[...] = _rms(x_scr[...] + 0.5 * y, gf_ref[...], NORM_EPS)


def _weight_scratch(*weights):
    widest = max(w.shape[2] for w in weights)
    return ([pltpu.VMEM(w.shape[1:], BF16) for w in weights]
            + [pltpu.VMEM((2, STAGE_ROWS, widest), F32), pltpu.SemaphoreType.DMA((2,))])


def _ffn1(x, g, wg, wu, wd, layer):
    t, d = x.shape
    d_ff = wg.shape[2]
    tm = FFN_ROWS
    assert t % tm == 0 and d_ff % FF_CHUNK == 0
    row = lambda i: (i, 0)
    hbm = pl.BlockSpec(memory_space=pl.ANY)
    return pl.pallas_call(
        functools.partial(_ffn1_kernel, layer=layer),
        grid=(t // tm,),
        in_specs=[pl.BlockSpec((tm, d), row), _resident(g.shape), hbm, hbm, hbm],
        out_specs=pl.BlockSpec((tm, d), row),
        out_shape=jax.ShapeDtypeStruct((t, d), F32),
        scratch_shapes=[pltpu.VMEM((tm, d), BF16), pltpu.VMEM((tm, LANE), F32), pltpu.VMEM((tm, d_ff), BF16)]
        + _weight_scratch(wg, wu, wd),
        compiler_params=pltpu.CompilerParams(
            dimension_semantics=("arbitrary",), vmem_limit_bytes=VMEM_LIMIT),
        name="ffn1",
    )(x, g, wg, wu, wd)


def _ffn2(x, yr, ys, wo, g, wg, wu, wd, gf, layer):
    t, d = x.shape
    d_ff = wg.shape[2]
    dm = yr.shape[1]
    tm = FFN_ROWS
    assert t % tm == 0 and d_ff % FF_CHUNK == 0 and wo.shape[1] == 2 * dm
    row = lambda i: (i, 0)
    hbm = pl.BlockSpec(memory_space=pl.ANY)
    return pl.pallas_call(
        functools.partial(_ffn2_kernel, layer=layer),
        grid=(t // tm,),
        in_specs=[pl.BlockSpec((tm, d), row), pl.BlockSpec((tm, dm), row), pl.BlockSpec((tm, dm), row),
                  hbm, _resident(g.shape), hbm, hbm, hbm, _resident(gf.shape)],
        out_specs=pl.BlockSpec((tm, d), row),
        out_shape=jax.ShapeDtypeStruct((t, d), F32),
        scratch_shapes=[pltpu.VMEM((tm, d), BF16), pltpu.VMEM((tm, LANE), F32), pltpu.VMEM((tm, d_ff), BF16),
                        pltpu.VMEM((tm, d), F32)] + _weight_scratch(wo, wg, wu, wd),
        compiler_params=pltpu.CompilerParams(
            dimension_semantics=("arbitrary",), vmem_limit_bytes=VMEM_LIMIT),
        name="ffn2",
    )(x, yr, ys, wo, g, wg, wu, wd, gf)


IN_CHUNK = 512


def _col_chunks(lo, hi):
    return [(c0, min(IN_CHUNK, hi - c0)) for c0 in range(lo, hi, IN_CHUNK)]


def _inproj_kernel(x_ref, g_ref, w_ref, wt_ref, mu_ref, cw_ref, cb_ref, p_ref, h_scr, ext_scr, raw_scr,
                   *, tiles_per_seq, wr, conv_lo):
    tm = x_ref.shape[0]
    dc = cw_ref.shape[1]
    width = cw_ref.shape[0]
    w_main = p_ref.shape[1] - wt_ref.shape[1]
    step = pl.program_id(0)
    h_scr[...] = _rms(x_ref[...], g_ref[...], NORM_EPS).astype(BF16)

    @pl.when(step == 0)
    def _():
        raw_scr[...] = jnp.zeros_like(raw_scr)

    @pl.when(lax.rem(step + tiles_per_seq - 1, tiles_per_seq) == 0)
    def _():
        ext_scr[...] = jnp.zeros_like(ext_scr)

    def behind_halo(cols, pc):
        ext = jnp.concatenate([ext_scr[:, cols], pc], axis=0)
        ext_scr[:, cols] = pc[tm - HALO:, :]
        return ext

    def lerp(c0, n, pc):
        cs = slice(c0, c0 + n)
        shifted = pltpu.roll(behind_halo(cs, pc), 1, axis=0)[HALO:, :]
        p_ref[:, cs] = pc + (shifted - pc) * mu_ref[:, cs]

    def conv_silu(c0, n, pc):
        j = c0 - conv_lo
        cw = slice(j, j + n)
        ext = behind_halo(slice(wr + j, wr + j + n), pc)
        tap = lambda i: cw_ref[width - 1 - i:width - i, cw]
        if width == 4:
            ext1 = pltpu.roll(ext, 1, axis=0)
            far = pltpu.roll(ext * tap(2) + ext1 * tap(3), 2, axis=0)[HALO:, :]
            conv = pc * tap(0) + ext1[HALO:, :] * tap(1) + far + cb_ref[:, cw]
        else:
            conv = pc * tap(0) + cb_ref[:, cw]
            for i in range(1, width):
                conv = conv + pltpu.roll(ext, i, axis=0)[HALO:, :] * tap(i)
        p_ref[:, c0:c0 + n] = conv * jax.nn.sigmoid(conv)

    def plain(c0, n, pc):
        p_ref[:, c0:c0 + n] = pc

    convs = [(conv_silu, c) for c in _col_chunks(conv_lo, conv_lo + dc)]
    light = ([(lerp, c) for c in _col_chunks(0, wr)]
             + [(plain, c) for c in _col_chunks(wr, conv_lo) + _col_chunks(conv_lo + dc, w_main)])
    per_conv = len(light) // max(len(convs), 1)
    order = []
    for n, cv in enumerate(convs):
        order += [cv] + light[n * per_conv:(n + 1) * per_conv]
    order += light[len(convs) * per_conv:]
    order.append((plain, (w_main, wt_ref.shape[1])))
    for finish, (c0, n) in order:
        before = raw_scr[:, c0:c0 + n]
        raw_scr[:, c0:c0 + n] = _dot(h_scr[...], w_ref[:, c0:c0 + n] if c0 < w_main else wt_ref[...])
        finish(c0, n, before)


def _inproj(x, g, w_all, layer, w_tail, mu, conv_w, conv_b, seq, wr, conv_lo):
    t, d = x.shape
    tm = 512
    dc = conv_w.shape[1]
    w_main = w_all.shape[2] // LANE * LANE
    width_out = w_main + w_tail.shape[1]
    assert t % tm == 0 and seq % tm == 0 and conv_w.shape[0] <= HALO
    assert all(v % LANE == 0 for v in (wr, conv_lo, dc, width_out)) and wr <= conv_lo <= w_main - dc
    body = functools.partial(_inproj_kernel, tiles_per_seq=seq // tm, wr=wr, conv_lo=conv_lo)
    n = t // tm
    return pl.pallas_call(
        body,
        grid=(n + 1,),
        in_specs=[pl.BlockSpec((tm, d), lambda i: (jnp.minimum(i, n - 1), 0)), _resident(g.shape),
                  pl.BlockSpec((None,) + w_all.shape[1:], lambda i: (layer, 0, 0), pipeline_mode=pl.Buffered(1)),
                  _resident(w_tail.shape), _resident(mu.shape), _resident(conv_w.shape),
                  _resident(conv_b.shape)],
        out_specs=pl.BlockSpec((tm, width_out), lambda i: (jnp.maximum(i - 1, 0), 0)),
        out_shape=jax.ShapeDtypeStruct((t, width_out), F32),
        scratch_shapes=[pltpu.VMEM((tm, d), BF16), pltpu.VMEM((HALO, wr + dc), F32),
                        pltpu.VMEM((tm, width_out), F32)],
        compiler_params=pltpu.CompilerParams(
            dimension_semantics=("arbitrary",), vmem_limit_bytes=VMEM_LIMIT),
        name="in_proj",
    )(x, g, w_all, w_tail, mu, conv_w, conv_b)


def _rwkv_kernel(p_ref, w0_ref, w2_ref, a0_ref, a2_ref, g2_ref, kk_ref, ka_ref, rk_ref,
                 gng_ref, gnb_ref, tri2_ref, bdm_ref, bdm32_ref, trm_ref, eye_ref,
                 o_ref,
                 zt_scr, k_scr, a_scr, b_scr, ld_scr, y_scr, bonus_scr, gate_scr):
    n_seq, rows_seq, dr = o_ref.shape
    tl = n_seq * rows_seq

    @pl.when(pl.program_id(1) == 0)
    def _():
        zt_scr[...] = jnp.zeros_like(zt_scr)

    cols_of_p = lambda lo, hi: p_ref[:, :, lo:hi].reshape(tl, hi - lo)
    r = cols_of_p(0, dr)
    k = cols_of_p(dr, 2 * dr)
    v = cols_of_p(2 * dr, 3 * dr)
    wa = cols_of_p(3 * dr, 3 * dr + LANE)
    gd = cols_of_p(3 * dr + LANE, 3 * dr + 2 * LANE)

    wpre = w0_ref[...] + _dot(jnp.tanh(wa).astype(BF16), w2_ref[...])
    ld_scr[...] = jax.nn.sigmoid(wpre) * (-jnp.exp(-0.5))
    iclr = jax.nn.sigmoid(a0_ref[...] + _dot(wa.astype(BF16), a2_ref[...]))
    gate_scr[...] = _dot(jax.nn.sigmoid(gd).astype(BF16), g2_ref[...])

    kkv = k * kk_ref[...]
    kkn = kkv * lax.rsqrt(jnp.maximum(_head_sum(kkv * kkv, bdm_ref), 1e-24))
    kmod = k * (1.0 + (iclr - 1.0) * ka_ref[...])
    k_scr[...] = kmod
    a_scr[...] = -kkn
    b_scr[...] = kkn * iclr
    bonus_scr[...] = _head_sum(r * kmod * rk_ref[...], bdm_ref) * v

    bdmask32 = bdm32_ref[...] > 0.5
    trmask = trm_ref[...] > 0.5
    bd = lambda xb: _block_diag(xb, bdm_ref[...])
    n_q = dr // BLOCK

    probs = [(slice(s * rows_seq + c * CHUNK, s * rows_seq + (c + 1) * CHUNK),
              slice(q * BLOCK, (q + 1) * BLOCK), s * n_q + q)
             for c in range(rows_seq // CHUNK) for s in range(n_seq) for q in range(n_q)]
    tile = lambda scr: [scr[rows, cols] for rows, cols, _ in probs]
    from_p = lambda col0: [p_ref[s, c * CHUNK:(c + 1) * CHUNK, col0 + q * BLOCK:col0 + (q + 1) * BLOCK]
                           for c in range(rows_seq // CHUNK) for s in range(n_seq) for q in range(n_q)]

    ld = tile(ld_scr)
    cum = _each(lambda x: _dot(tri2_ref[...], _split2_rows(x)), ld)
    b_c = tile(b_scr)
    k_c = tile(k_scr)
    vb = _each(lambda x: x.astype(BF16), from_p(2 * dr))
    atb = _each(lambda a, cm, l: (a * jnp.exp(cm - l)).astype(BF16), tile(a_scr), cum, ld)
    rtb = _each(lambda x, cm: (x * jnp.exp(cm)).astype(BF16), from_p(0), cum)
    inv = _each(lambda cm: jnp.exp(-cm), cum)
    g_end = _each(lambda cm: jnp.exp(cm[CHUNK - 1:CHUNK, :]), cum)
    to_end = _each(lambda cm: jnp.exp(cm[CHUNK - 1:CHUNK, :] - cm), cum)
    ar = _each(lambda a, x: jnp.concatenate([a, x], axis=0), atb, rtb)
    masked = lambda x: jnp.where(trmask, x, 0.0)
    sb32 = _each(lambda a, b, i: masked(_dot_nt(a, bd((b * i).astype(BF16)))), ar, b_c, inv)
    sk = _each(lambda a, x, i: masked(_dot_nt(a, bd((x * i).astype(BF16)))).astype(BF16), ar, k_c, inv)
    kv = _each(lambda x, y: _dot(x, bd(y)), sk, vb)
    t_inv = _each(lambda x: eye_ref[...] + x[0:CHUNK], sb32)
    pw = _each(lambda x: x[0:CHUNK].astype(BF16), sb32)
    pw = _each(lambda x: _dot(x, bd(x)).astype(BF16), pw)
    for i in range(1, 5):
        res = _each(lambda t, x: _dot(jnp.concatenate([t.astype(BF16), x], axis=0), bd(x)), t_inv, pw)
        t_inv = _each(lambda t, x: t + x[0:CHUNK], t_inv, res)
        pw = _each(lambda x: x[CHUNK:].astype(BF16), res)
    t_inv = _each(lambda t, x: t + _dot(t.astype(BF16), bd(x)), t_inv, pw)
    xx = _each(lambda t, a, x: _dot(t.astype(BF16),
                                    jnp.concatenate([bd(a), bd(x[0:CHUNK].astype(BF16))], axis=1)),
               t_inv, atb, kv)
    x1b = _each(lambda x: x[:, 0:BLOCK].astype(BF16), xx)
    x2 = _each(lambda x: x[:, BLOCK:], xx)
    bgb = _each(lambda b, e: (b * e).astype(BF16), b_c, to_end)
    kgb = _each(lambda x, e: (x * e).astype(BF16), k_c, to_end)
    p_t = _each(lambda x, b: _dot_tn(x, b).astype(BF16), x1b, bgb)
    c_t = _each(lambda x, y, b, kg: _dot_tn(jnp.concatenate([x.astype(BF16), y], axis=0),
                                            jnp.concatenate([b, kg], axis=0)), x2, vb, bgb, kgb)

    zs = [zt_scr[j] for j in range(n_seq * n_q)]
    zb = []
    for (_, _, j), g, pt, ct in zip(probs, g_end, p_t, c_t):
        zb.append(zs[j].astype(BF16))
        zs[j] = zs[j] * g + jnp.where(bdmask32, _dot(zb[-1], pt) + ct, 0.0)
    for j in range(n_seq * n_q):
        zt_scr[j] = zs[j]
    xz = _each(lambda x, rt, z: _dot_nt(jnp.concatenate([x, rt], axis=0), z), x1b, rtb, zb)
    ubd = _each(lambda m, x: bd((m[0:CHUNK] + x).astype(BF16)), xz, x2)
    for (rows, cols, _), m, n_rb, u, o in zip(probs, xz, sb32, ubd, kv):
        y_scr[rows, cols] = m[CHUNK:] + _dot(n_rb[CHUNK:].astype(BF16), u) + o[CHUNK:]

    y = y_scr[...]
    inv_n = 1.0 / HEAD_DIM
    mean = _head_sum(y, bdm_ref) * inv_n
    yc = y - mean
    var = _head_sum(yc * yc, bdm_ref) * inv_n
    yn = yc * lax.rsqrt(var + RWKV_GN_EPS) * gng_ref[...] + gnb_ref[...]
    o_ref[...] = ((yn + bonus_scr[...]) * gate_scr[...]).astype(BF16).reshape(n_seq, rows_seq, dr)


def _ssd_kernel(p_ref, dtb_ref, alog_ref, dsk_ref, ng_ref, exp_ref,
                tri2_ref, ones_ref, bdm_ref, tri_ref, triu_ref,
                o_ref,
                st_scr, xs_scr, xdt_scr, a_scr, y_scr):
    n_seq, rows_seq, ds = o_ref.shape
    tl = n_seq * rows_seq
    _, d_state, _ = st_scr.shape
    n_grp = ds // BLOCK
    dc = ds + 2 * n_grp * d_state

    @pl.when(pl.program_id(1) == 0)
    def _():
        st_scr[...] = jnp.zeros_like(st_scr)

    cols_of_p = lambda lo, hi: p_ref[:, :, lo:hi].reshape(tl, hi - lo)
    xs = cols_of_p(ds, 2 * ds)
    dtr = cols_of_p(ds + dc, ds + dc + LANE) + dtb_ref[...]
    dt_c = jnp.maximum(dtr, 0.0) + jnp.log1p(jnp.exp(-jnp.abs(dtr)))
    dt = _dot(_split2_cols(dt_c), exp_ref[...])
    xs_scr[...] = xs
    xdt_scr[...] = xs * dt
    a_scr[...] = dt * (-jnp.exp(alog_ref[...]))
    bc0 = 2 * ds

    incl = tri_ref[...] > 0.5
    upper = triu_ref[...] > 0.5

    order = [(s, c, g) for c in range(rows_seq // CHUNK) for s in range(n_seq) for g in range(n_grp)]
    probs = [(slice(s * rows_seq + c * CHUNK, s * rows_seq + (c + 1) * CHUNK), g, s * n_grp + g)
             for s, c, g in order]
    gcols = lambda g: slice(g * BLOCK, (g + 1) * BLOCK)
    from_p = lambda col0: [p_ref[s, c * CHUNK:(c + 1) * CHUNK, col0 + g * d_state:col0 + (g + 1) * d_state]
                           .astype(BF16) for s, c, g in order]

    a_c = [a_scr[rows, gcols(g)] for rows, g, _ in probs]
    e1 = _each(lambda a: _dot(tri2_ref[...], _split2_rows(a)), a_c)
    e2 = _each(lambda a: _dot(ones_ref[...], _split2_rows(jnp.where(upper, a, 0.0))), a_c)
    l_w = _each(lambda x, y: jnp.exp(jnp.where(incl, x - y, -jnp.inf)), e1, e2)
    b_g = from_p(bc0)
    c_g = from_p(bc0 + n_grp * d_state)
    scores = _each(lambda c, b: _dot_nt(c, jnp.concatenate([b, b, b, b], axis=0)), c_g, b_g)
    xdt_c = [xdt_scr[rows, gcols(g)] for rows, g, _ in probs]
    y_diag = _each(lambda x, l, y: _dot((x * l).astype(BF16), _block_diag(y.astype(BF16), bdm_ref[...])),
                   scores, l_w, xdt_c)
    new = _each(lambda b, x, e: _dot_tn(b, (x * jnp.exp(e[CHUNK - 1:CHUNK, :] - e)).astype(BF16)),
                b_g, xdt_c, e1)

    sts = [st_scr[j] for j in range(n_seq * n_grp)]
    stb = []
    for (_, _, j), e, n in zip(probs, e1, new):
        stb.append(sts[j].astype(BF16))
        sts[j] = sts[j] * jnp.exp(e[CHUNK - 1:CHUNK, :]) + n
    for j in range(n_seq * n_grp):
        st_scr[j] = sts[j]
    for (rows, g, _), c, st, yd, e in zip(probs, c_g, stb, y_diag, e1):
        y_scr[rows, gcols(g)] = yd + _dot(c, st) * jnp.exp(e)

    z = cols_of_p(0, ds)
    y = (y_scr[...] + dsk_ref[...] * xs_scr[...]) * (z * jax.nn.sigmoid(z))
    o_ref[...] = _rms(y, ng_ref[...], SSM_NORM_EPS).astype(BF16).reshape(n_seq, rows_seq, ds)


MIX_SEQS = 2


def _mixers(p, batch, rwkv_params, ssd_params, consts, n_grp, d_state):
    t, wp2 = p.shape
    wp = wp2 // 2
    dr = rwkv_params["w0"].shape[1]
    ds = ssd_params["norm_g"].shape[1]
    dc = ds + 2 * n_grp * d_state
    seq = t // batch
    rows = MIX_ROWS // MIX_SEQS
    assert dr % BLOCK == 0 and wp == 3 * dr + 2 * LANE and ds == n_grp * BLOCK and ds + dc + LANE <= wp
    assert t == batch * seq and batch % MIX_SEQS == 0 and seq % rows == 0 and rows % CHUNK == 0
    r_ops = ([rwkv_params[n] for n in ["w0", "w2", "a0", "a2", "g2", "k_k", "k_a", "r_k", "gn_g", "gn_b"]]
             + [consts[n] for n in ["tri2", "bdm", "bdm32", "trm", "eye"]])
    s_ops = ([ssd_params[n] for n in ["dt_bias", "a_log", "d_skip", "norm_g", "expand"]]
             + [consts[n] for n in ["tri2", "ones", "bdm", "tri", "triu"]])
    tl = MIX_ROWS
    r_scr = ([pltpu.VMEM((MIX_SEQS * (dr // BLOCK), BLOCK, BLOCK), F32)]
             + [pltpu.VMEM((tl, dr), F32) for _ in range(7)])
    s_scr = [pltpu.VMEM((MIX_SEQS * n_grp, d_state, BLOCK), F32)] + [pltpu.VMEM((tl, ds), F32) for _ in range(4)]

    def body(pr_ref, ps_ref, *refs):
        r_in, refs = refs[:len(r_ops)], refs[len(r_ops):]
        s_in, refs = refs[:len(s_ops)], refs[len(s_ops):]
        (or_ref, os_ref), scr = refs[:2], refs[2:]
        _rwkv_kernel(pr_ref, *r_in, or_ref, *scr[:len(r_scr)])
        _ssd_kernel(ps_ref, *s_in, os_ref, *scr[len(r_scr):])

    tile = lambda col: pl.BlockSpec((MIX_SEQS, rows, wp), lambda b, l: (b, l, col))
    out = lambda w: pl.BlockSpec((MIX_SEQS, rows, w), lambda b, l: (b, l, 0))
    p3 = p.reshape(batch, seq, wp2)
    y_r, y_s = pl.pallas_call(
        body,
        grid=(batch // MIX_SEQS, seq // rows),
        in_specs=[tile(0), tile(1)] + [_resident(o.shape) for o in r_ops + s_ops],
        out_specs=[out(dr), out(ds)],
        out_shape=[jax.ShapeDtypeStruct((batch, seq, dr), BF16), jax.ShapeDtypeStruct((batch, seq, ds), BF16)],
        scratch_shapes=r_scr + s_scr,
        compiler_params=pltpu.CompilerParams(
            dimension_semantics=("arbitrary", "arbitrary"), vmem_limit_bytes=VMEM_LIMIT),
        name="mixers",
    )(p3, p3, *r_ops, *s_ops)
    return y_r.reshape(t, dr), y_s.reshape(t, ds)


def _mask_consts():
    row = jnp.arange(BLOCK)[:, None]
    col = jnp.arange(BLOCK)[None, :]
    t = jnp.arange(CHUNK)[:, None]
    s = col % CHUNK
    tri64 = (jnp.arange(CHUNK)[None, :] <= t)
    bdm = row // CHUNK == col // CHUNK
    return {
        "bdm": bdm.astype(BF16),
        "bdm32": bdm.astype(F32),
        "trm": jnp.concatenate([s < t, s <= t], axis=0).astype(F32),
        "tri": (s <= t).astype(F32),
        "triu": (s >= t).astype(F32),
        "eye": (s == t).astype(F32),
        "tri2": jnp.concatenate([tri64, tri64], axis=1).astype(BF16),
        "ones": jnp.ones((CHUNK, 2 * CHUNK), BF16),
    }


def kernel(x, norm_ffn1, ffn1_w_gate, ffn1_w_up, ffn1_w_down, norm_mix, w_in, rwkv_mu, rwkv_w0, rwkv_w2,
           rwkv_a0, rwkv_a2, rwkv_g2, rwkv_k_k, rwkv_k_a, rwkv_r_k, rwkv_gn_g, rwkv_gn_b, ssm_conv_w,
           ssm_conv_b, ssm_dt_bias, ssm_a_log, ssm_d, ssm_norm, w_out, norm_ffn2, ffn2_w_gate, ffn2_w_up,
           ffn2_w_down, norm_final):
    batch, seq, d = x.shape
    depth = norm_ffn1.shape[0]
    dr = rwkv_w0.shape[1]
    ds = ssm_norm.shape[1]
    dc = ssm_conv_w.shape[2]
    n_heads_s = ssm_a_log.shape[1]
    lora_w = rwkv_w2.shape[1]
    lora_a = rwkv_a2.shape[1]
    lora_g = rwkv_g2.shape[1]
    n_grp = ds // BLOCK
    d_state = (dc - ds) // (2 * n_grp)
    wp = 3 * dr + 2 * LANE
    assert ds // n_heads_s == HEAD_DIM and lora_w + lora_a == LANE and lora_g == LANE
    assert n_heads_s <= LANE and ds + dc + LANE <= wp and seq % MIX_ROWS == 0

    consts = _mask_consts()
    lane_head = jnp.arange(LANE)[:, None] == (jnp.arange(ds) // HEAD_DIM)[None, :]
    expand = jnp.concatenate([lane_head, lane_head], axis=0).astype(BF16)
    row = lambda a: a.reshape(1, -1)
    rep = lambda a: jnp.repeat(a, HEAD_DIM).reshape(1, -1)

    w_in_b = w_in.astype(BF16)
    n_main = w_in.shape[2] // LANE * LANE

    xt = x.reshape(batch * seq, d)
    for i in range(depth):
        x1 = _ffn1(xt, row(norm_ffn1[i]), ffn1_w_gate, ffn1_w_up, ffn1_w_down, i)

        w_tail = jnp.pad(w_in[i][:, n_main:], ((0, 0), (0, 2 * wp - w_in.shape[2]))).astype(BF16)
        p = _inproj(x1, row(norm_mix[i]), w_in_b, i, w_tail, row(rwkv_mu[i]), ssm_conv_w[i],
                    row(ssm_conv_b[i]), seq, wp, wp + ds)

        rwkv_params = {
            "w0": row(rwkv_w0[i]), "a0": row(rwkv_a0[i]),
            "w2": jnp.pad(rwkv_w2[i], ((0, lora_a), (0, 0))).astype(BF16),
            "a2": jnp.pad(rwkv_a2[i], ((lora_w, 0), (0, 0))).astype(BF16),
            "g2": rwkv_g2[i].astype(BF16),
            "k_k": row(rwkv_k_k[i]), "k_a": row(rwkv_k_a[i]), "r_k": row(rwkv_r_k[i]),
            "gn_g": row(rwkv_gn_g[i]), "gn_b": row(rwkv_gn_b[i]),
        }
        ssd_params = {
            "dt_bias": jnp.pad(ssm_dt_bias[i], (0, LANE - n_heads_s)).reshape(1, -1),
            "a_log": rep(ssm_a_log[i]), "d_skip": rep(ssm_d[i]), "norm_g": row(ssm_norm[i]),
            "expand": expand,
        }
        y_r, y_s = _mixers(p, batch, rwkv_params, ssd_params, consts, n_grp, d_state)

        assert i == depth - 1, "the fused final norm assumes a single layer"
        xt = _ffn2(x1, y_r, y_s, w_out, row(norm_ffn2[i]), ffn2_w_gate, ffn2_w_up, ffn2_w_down,
                   row(norm_final), i)
    return xt.reshape(batch, seq, d)
```

```python
import functools

import jax
import jax.numpy as jnp
from jax import lax
from jax.experimental import pallas as pl
from jax.experimental.pallas import tpu as pltpu

F32 = jnp.float32
BF16 = jnp.bfloat16

NORM_EPS = 1e-6
RWKV_GN_EPS = 64e-5
SSM_NORM_EPS = 1e-5

HEAD_DIM = 64
CHUNK = 64
BLOCK = 256
LANE = 128
HALO = 8
MIX_ROWS = 512

NT_DIMS = (((1,), (1,)), ((), ()))
TN_DIMS = (((0,), (0,)), ((), ()))

VMEM_LIMIT = 56 * 1024 * 1024


def _dot(a, b):
    return jnp.dot(a, b, preferred_element_type=F32)


def _dot_nt(a, b):
    return lax.dot_general(a, b, NT_DIMS, preferred_element_type=F32)


def _dot_tn(a, b):
    return lax.dot_general(a, b, TN_DIMS, preferred_element_type=F32)


def _rms(x, g, eps):
    return x * lax.rsqrt(jnp.mean(x * x, axis=-1, keepdims=True) + eps) * g


def _split2_rows(x):
    hi = x.astype(BF16)
    lo = (x - hi.astype(F32)).astype(BF16)
    return jnp.concatenate([hi, lo], axis=0)


def _split2_cols(x):
    hi = x.astype(BF16)
    lo = (x - hi.astype(F32)).astype(BF16)
    return jnp.concatenate([hi, lo], axis=1)


def _head_sum(x, bdm_ref):
    xb = x.astype(BF16)
    return jnp.concatenate([_dot(xb[:, c:c + BLOCK], bdm_ref[...]) for c in range(0, x.shape[1], BLOCK)],
                           axis=1)


def _block_diag(xb, bdm):
    return jnp.concatenate([xb, xb, xb, xb], axis=0) * bdm


def _each(f, *lists):
    return [f(*args) for args in zip(*lists)]


def _resident(shape):
    zeros = (0,) * len(shape)
    return pl.BlockSpec(shape, lambda *_: zeros, pipeline_mode=pl.Buffered(1))


FF_CHUNK = 256
FFN_ROWS = 512


def _norm_split(x, g, h_scr, r_scr):
    h_scr[...] = (x * g).astype(BF16)
    r = lax.rsqrt(jnp.mean(x * x, axis=-1, keepdims=True) + NORM_EPS)
    r_scr[...] = jnp.broadcast_to(r, r_scr.shape)


def _swiglu(h_scr, r_scr, wg_ref, wu_ref, wd_ref, act_scr):
    d_ff = wg_ref.shape[1]
    for c in range(d_ff // FF_CHUNK):
        cs = slice(c * FF_CHUNK, (c + 1) * FF_CHUNK)
        r = jnp.concatenate([r_scr[...]] * (FF_CHUNK // LANE), axis=1)
        gate = _dot(h_scr[...], wg_ref[:, cs]) * r
        up = _dot(h_scr[...], wu_ref[:, cs]) * r
        act_scr[:, cs] = (gate * jax.nn.sigmoid(gate) * up).astype(BF16)
    return _dot(act_scr[...], wd_ref[...])


STAGE_ROWS = 512


def _stage_weights(w_hbms, layer, w_scrs, stage_scr, sems):
    step = stage_scr.shape[1]
    pieces = [(k, r0, min(step, w.shape[1] - r0))
              for k, w in enumerate(w_hbms) for r0 in range(0, w.shape[1], step)]

    def view(c):
        k, _, rows = pieces[c]
        return stage_scr.at[c % 2, pl.ds(0, rows), pl.ds(0, w_hbms[k].shape[2])]

    def dma(c):
        k, r0, rows = pieces[c]
        return pltpu.make_async_copy(w_hbms[k].at[layer, pl.ds(r0, rows), :], view(c), sems.at[c % 2])

    dma(0).start()
    for c, (k, r0, rows) in enumerate(pieces):
        if c + 1 < len(pieces):
            dma(c + 1).start()
        dma(c).wait()
        w_scrs[k][r0:r0 + rows, 0:w_hbms[k].shape[2]] = view(c)[...].astype(BF16)


def _ffn1_kernel(x_ref, g_ref, wg_hbm, wu_hbm, wd_hbm, o_ref,
                 h_scr, r_scr, act_scr, wg_scr, wu_scr, wd_scr, stage_scr, sems, *, layer):
    @pl.when(pl.program_id(0) == 0)
    def _():
        _stage_weights((wg_hbm, wu_hbm, wd_hbm), layer, (wg_scr, wu_scr, wd_scr), stage_scr, sems)

    _norm_split(x_ref[...], g_ref[...], h_scr, r_scr)
    o_ref[...] = x_ref[...] + 0.5 * _swiglu(h_scr, r_scr, wg_scr, wu_scr, wd_scr, act_scr)


def _ffn2_kernel(x_ref, yr_ref, ys_ref, wo_hbm, g_ref, wg_hbm, wu_hbm, wd_hbm, gf_ref, o_ref,
                 h_scr, r_scr, act_scr, x_scr, wo_scr, wg_scr, wu_scr, wd_scr, stage_scr, sems, *, layer):
    @pl.when(pl.program_id(0) == 0)
    def _():
        _stage_weights((wo_hbm, wg_hbm, wu_hbm, wd_hbm), layer, (wo_scr, wg_scr, wu_scr, wd_scr),
                       stage_scr, sems)

    dr = yr_ref.shape[1]
    x2 = x_ref[...] + _dot(yr_ref[...], wo_scr[0:dr, :]) + _dot(ys_ref[...], wo_scr[dr:, :])
    x_scr[...] = x2
    _norm_split(x2, g_ref[...], h_scr, r_scr)
    y = _swiglu(h_scr, r_scr, wg_scr, wu_scr, wd_scr, act_scr)
    o_ref[...] = _rms(x_scr[...] + 0.5 * y, gf_ref[...], NORM_EPS)


def _weight_scratch(*weights):
    widest = max(w.shape[2] for w in weights)
    return ([pltpu.VMEM(w.shape[1:], BF16) for w in weights]
            + [pltpu.VMEM((2, STAGE_ROWS, widest), F32), pltpu.SemaphoreType.DMA((2,))])


def _ffn1(x, g, wg, wu, wd, layer):
    t, d = x.shape
    d_ff = wg.shape[2]
    tm = FFN_ROWS
    assert t % tm == 0 and d_ff % FF_CHUNK == 0
    row = lambda i: (i, 0)
    hbm = pl.BlockSpec(memory_space=pl.ANY)
    return pl.pallas_call(
        functools.partial(_ffn1_kernel, layer=layer),
        grid=(t // tm,),
        in_specs=[pl.BlockSpec((tm, d), row), _resident(g.shape), hbm, hbm, hbm],
        out_specs=pl.BlockSpec((tm, d), row),
        out_shape=jax.ShapeDtypeStruct((t, d), F32),
        scratch_shapes=[pltpu.VMEM((tm, d), BF16), pltpu.VMEM((tm, LANE), F32), pltpu.VMEM((tm, d_ff), BF16)]
        + _weight_scratch(wg, wu, wd),
        compiler_params=pltpu.CompilerParams(
            dimension_semantics=("arbitrary",), vmem_limit_bytes=VMEM_LIMIT),
        name="ffn1",
    )(x, g, wg, wu, wd)


def _ffn2(x, yr, ys, wo, g, wg, wu, wd, gf, layer):
    t, d = x.shape
    d_ff = wg.shape[2]
    dm = yr.shape[1]
    tm = FFN_ROWS
    assert t % tm == 0 and d_ff % FF_CHUNK == 0 and wo.shape[1] == 2 * dm
    row = lambda i: (i, 0)
    hbm = pl.BlockSpec(memory_space=pl.ANY)
    return pl.pallas_call(
        functools.partial(_ffn2_kernel, layer=layer),
        grid=(t // tm,),
        in_specs=[pl.BlockSpec((tm, d), row), pl.BlockSpec((tm, dm), row), pl.BlockSpec((tm, dm), row),
                  hbm, _resident(g.shape), hbm, hbm, hbm, _resident(gf.shape)],
        out_specs=pl.BlockSpec((tm, d), row),
        out_shape=jax.ShapeDtypeStruct((t, d), F32),
        scratch_shapes=[pltpu.VMEM((tm, d), BF16), pltpu.VMEM((tm, LANE), F32), pltpu.VMEM((tm, d_ff), BF16),
                        pltpu.VMEM((tm, d), F32)] + _weight_scratch(wo, wg, wu, wd),
        compiler_params=pltpu.CompilerParams(
            dimension_semantics=("arbitrary",), vmem_limit_bytes=VMEM_LIMIT),
        name="ffn2",
    )(x, yr, ys, wo, g, wg, wu, wd, gf)


IN_CHUNK = 512


def _col_chunks(lo, hi):
    return [(c0, min(IN_CHUNK, hi - c0)) for c0 in range(lo, hi, IN_CHUNK)]


def _inproj_kernel(x_ref, g_ref, w_hbm, mu_ref, cw_ref, cb_ref, p_ref, h_scr, ext_scr, raw_scr,
                   w_scr, stage_scr, sems, *, layer, tiles_per_seq, wr, conv_lo):
    tm = x_ref.shape[0]
    dc = cw_ref.shape[1]
    width = cw_ref.shape[0]
    step = pl.program_id(0)
    h_scr[...] = _rms(x_ref[...], g_ref[...], NORM_EPS).astype(BF16)

    @pl.when(step == 0)
    def _():
        raw_scr[...] = jnp.zeros_like(raw_scr)
        pad_lo = w_hbm.shape[2] // LANE * LANE
        w_scr[:, pad_lo:] = jnp.zeros((w_scr.shape[0], w_scr.shape[1] - pad_lo), BF16)
        _stage_weights((w_hbm,), layer, (w_scr,), stage_scr, sems)

    @pl.when(lax.rem(step + tiles_per_seq - 1, tiles_per_seq) == 0)
    def _():
        ext_scr[...] = jnp.zeros_like(ext_scr)

    def behind_halo(cols, pc):
        ext = jnp.concatenate([ext_scr[:, cols], pc], axis=0)
        ext_scr[:, cols] = pc[tm - HALO:, :]
        return ext

    def lerp(c0, n, pc):
        cs = slice(c0, c0 + n)
        shifted = pltpu.roll(behind_halo(cs, pc), 1, axis=0)[HALO:, :]
        p_ref[:, cs] = pc + (shifted - pc) * mu_ref[:, cs]

    def conv_silu(c0, n, pc):
        j = c0 - conv_lo
        cw = slice(j, j + n)
        ext = behind_halo(slice(wr + j, wr + j + n), pc)
        tap = lambda i: cw_ref[width - 1 - i:width - i, cw]
        if width == 4:
            ext1 = pltpu.roll(ext, 1, axis=0)
            far = pltpu.roll(ext * tap(2) + ext1 * tap(3), 2, axis=0)[HALO:, :]
            conv = pc * tap(0) + ext1[HALO:, :] * tap(1) + far + cb_ref[:, cw]
        else:
            conv = pc * tap(0) + cb_ref[:, cw]
            for i in range(1, width):
                conv = conv + pltpu.roll(ext, i, axis=0)[HALO:, :] * tap(i)
        p_ref[:, c0:c0 + n] = conv * jax.nn.sigmoid(conv)

    def plain(c0, n, pc):
        p_ref[:, c0:c0 + n] = pc

    convs = [(conv_silu, c) for c in _col_chunks(conv_lo, conv_lo + dc)]
    light = ([(lerp, c) for c in _col_chunks(0, wr)]
             + [(plain, c) for c in _col_chunks(wr, conv_lo) + _col_chunks(conv_lo + dc, p_ref.shape[1])])
    per_conv = len(light) // max(len(convs), 1)
    order = []
    for n, cv in enumerate(convs):
        order += [cv] + light[n * per_conv:(n + 1) * per_conv]
    order += light[len(convs) * per_conv:]
    for finish, (c0, n) in order:
        before = raw_scr[:, c0:c0 + n]
        raw_scr[:, c0:c0 + n] = _dot(h_scr[...], w_scr[:, c0:c0 + n])
        finish(c0, n, before)


IN_STAGE_ROWS = 256


def _inproj(x, g, w_all, layer, width_out, mu, conv_w, conv_b, seq, wr, conv_lo):
    t, d = x.shape
    tm = 512
    dc = conv_w.shape[1]
    n_cols = w_all.shape[2]
    assert t % tm == 0 and seq % tm == 0 and conv_w.shape[0] <= HALO
    assert all(v % LANE == 0 for v in (wr, conv_lo, dc, width_out)) and wr <= conv_lo <= n_cols - dc
    assert n_cols <= width_out
    body = functools.partial(_inproj_kernel, layer=layer, tiles_per_seq=seq // tm, wr=wr, conv_lo=conv_lo)
    n = t // tm
    return pl.pallas_call(
        body,
        grid=(n + 1,),
        in_specs=[pl.BlockSpec((tm, d), lambda i: (jnp.minimum(i, n - 1), 0)), _resident(g.shape),
                  pl.BlockSpec(memory_space=pl.ANY), _resident(mu.shape), _resident(conv_w.shape),
                  _resident(conv_b.shape)],
        out_specs=pl.BlockSpec((tm, width_out), lambda i: (jnp.maximum(i - 1, 0), 0)),
        out_shape=jax.ShapeDtypeStruct((t, width_out), F32),
        scratch_shapes=[pltpu.VMEM((tm, d), BF16), pltpu.VMEM((HALO, wr + dc), F32),
                        pltpu.VMEM((tm, width_out), F32), pltpu.VMEM((d, width_out), BF16),
                        pltpu.VMEM((2, IN_STAGE_ROWS, n_cols), F32), pltpu.SemaphoreType.DMA((2,))],
        compiler_params=pltpu.CompilerParams(
            dimension_semantics=("arbitrary",), vmem_limit_bytes=VMEM_LIMIT),
        name="in_proj",
    )(x, g, w_all, mu, conv_w, conv_b)


def _rwkv_kernel(p_ref, w0_ref, w2_ref, a0_ref, a2_ref, g2_ref, kk_ref, ka_ref, rk_ref,
                 gng_ref, gnb_ref, tri2_ref, bdm_ref, bdm32_ref, trm_ref, eye_ref,
                 o_ref,
                 zt_scr, k_scr, a_scr, b_scr, ld_scr, y_scr, bonus_scr, gate_scr):
    n_seq, rows_seq, dr = o_ref.shape
    tl = n_seq * rows_seq

    @pl.when(pl.program_id(1) == 0)
    def _():
        zt_scr[...] = jnp.zeros_like(zt_scr)

    cols_of_p = lambda lo, hi: p_ref[:, :, lo:hi].reshape(tl, hi - lo)
    r = cols_of_p(0, dr)
    k = cols_of_p(dr, 2 * dr)
    v = cols_of_p(2 * dr, 3 * dr)
    wa = cols_of_p(3 * dr, 3 * dr + LANE)
    gd = cols_of_p(3 * dr + LANE, 3 * dr + 2 * LANE)

    wpre = w0_ref[...] + _dot(jnp.tanh(wa).astype(BF16), w2_ref[...])
    ld_scr[...] = jax.nn.sigmoid(wpre) * (-jnp.exp(-0.5))
    iclr = jax.nn.sigmoid(a0_ref[...] + _dot(wa.astype(BF16), a2_ref[...]))
    gate_scr[...] = _dot(jax.nn.sigmoid(gd).astype(BF16), g2_ref[...])

    kkv = k * kk_ref[...]
    kkn = kkv * lax.rsqrt(jnp.maximum(_head_sum(kkv * kkv, bdm_ref), 1e-24))
    kmod = k * (1.0 + (iclr - 1.0) * ka_ref[...])
    k_scr[...] = kmod
    a_scr[...] = -kkn
    b_scr[...] = kkn * iclr
    bonus_scr[...] = _head_sum(r * kmod * rk_ref[...], bdm_ref) * v

    bdmask32 = bdm32_ref[...] > 0.5
    trmask = trm_ref[...] > 0.5
    bd = lambda xb: _block_diag(xb, bdm_ref[...])
    n_q = dr // BLOCK

    probs = [(slice(s * rows_seq + c * CHUNK, s * rows_seq + (c + 1) * CHUNK),
              slice(q * BLOCK, (q + 1) * BLOCK), s * n_q + q)
             for c in range(rows_seq // CHUNK) for s in range(n_seq) for q in range(n_q)]
    tile = lambda scr: [scr[rows, cols] for rows, cols, _ in probs]
    from_p = lambda col0: [p_ref[s, c * CHUNK:(c + 1) * CHUNK, col0 + q * BLOCK:col0 + (q + 1) * BLOCK]
                           for c in range(rows_seq // CHUNK) for s in range(n_seq) for q in range(n_q)]

    ld = tile(ld_scr)
    cum = _each(lambda x: _dot(tri2_ref[...], _split2_rows(x)), ld)
    b_c = tile(b_scr)
    k_c = tile(k_scr)
    vb = _each(lambda x: x.astype(BF16), from_p(2 * dr))
    atb = _each(lambda a, cm, l: (a * jnp.exp(cm - l)).astype(BF16), tile(a_scr), cum, ld)
    rtb = _each(lambda x, cm: (x * jnp.exp(cm)).astype(BF16), from_p(0), cum)
    inv = _each(lambda cm: jnp.exp(-cm), cum)
    g_end = _each(lambda cm: jnp.exp(cm[CHUNK - 1:CHUNK, :]), cum)
    to_end = _each(lambda cm: jnp.exp(cm[CHUNK - 1:CHUNK, :] - cm), cum)
    ar = _each(lambda a, x: jnp.concatenate([a, x], axis=0), atb, rtb)
    masked = lambda x: jnp.where(trmask, x, 0.0)
    sb32 = _each(lambda a, b, i: masked(_dot_nt(a, bd((b * i).astype(BF16)))), ar, b_c, inv)
    sk = _each(lambda a, x, i: masked(_dot_nt(a, bd((x * i).astype(BF16)))).astype(BF16), ar, k_c, inv)
    kv = _each(lambda x, y: _dot(x, bd(y)), sk, vb)
    t_inv = _each(lambda x: eye_ref[...] + x[0:CHUNK], sb32)
    pw = _each(lambda x: x[0:CHUNK].astype(BF16), sb32)
    pw = _each(lambda x: _dot(x, bd(x)).astype(BF16), pw)
    for i in range(1, 5):
        res = _each(lambda t, x: _dot(jnp.concatenate([t.astype(BF16), x], axis=0), bd(x)), t_inv, pw)
        t_inv = _each(lambda t, x: t + x[0:CHUNK], t_inv, res)
        pw = _each(lambda x: x[CHUNK:].astype(BF16), res)
    t_inv = _each(lambda t, x: t + _dot(t.astype(BF16), bd(x)), t_inv, pw)
    xx = _each(lambda t, a, x: _dot(t.astype(BF16),
                                    jnp.concatenate([bd(a), bd(x[0:CHUNK].astype(BF16))], axis=1)),
               t_inv, atb, kv)
    x1b = _each(lambda x: x[:, 0:BLOCK].astype(BF16), xx)
    x2 = _each(lambda x: x[:, BLOCK:], xx)
    bgb = _each(lambda b, e: (b * e).astype(BF16), b_c, to_end)
    kgb = _each(lambda x, e: (x * e).astype(BF16), k_c, to_end)
    p_t = _each(lambda x, b: _dot_tn(x, b).astype(BF16), x1b, bgb)
    c_t = _each(lambda x, y, b, kg: _dot_tn(jnp.concatenate([x.astype(BF16), y], axis=0),
                                            jnp.concatenate([b, kg], axis=0)), x2, vb, bgb, kgb)

    zs = [zt_scr[j] for j in range(n_seq * n_q)]
    zb = []
    for (_, _, j), g, pt, ct in zip(probs, g_end, p_t, c_t):
        zb.append(zs[j].astype(BF16))
        zs[j] = zs[j] * g + jnp.where(bdmask32, _dot(zb[-1], pt) + ct, 0.0)
    for j in range(n_seq * n_q):
        zt_scr[j] = zs[j]
    xz = _each(lambda x, rt, z: _dot_nt(jnp.concatenate([x, rt], axis=0), z), x1b, rtb, zb)
    ubd = _each(lambda m, x: bd((m[0:CHUNK] + x).astype(BF16)), xz, x2)
    for (rows, cols, _), m, n_rb, u, o in zip(probs, xz, sb32, ubd, kv):
        y_scr[rows, cols] = m[CHUNK:] + _dot(n_rb[CHUNK:].astype(BF16), u) + o[CHUNK:]

    y = y_scr[...]
    inv_n = 1.0 / HEAD_DIM
    mean = _head_sum(y, bdm_ref) * inv_n
    yc = y - mean
    var = _head_sum(yc * yc, bdm_ref) * inv_n
    yn = yc * lax.rsqrt(var + RWKV_GN_EPS) * gng_ref[...] + gnb_ref[...]
    o_ref[...] = ((yn + bonus_scr[...]) * gate_scr[...]).astype(BF16).reshape(n_seq, rows_seq, dr)


def _ssd_kernel(p_ref, dtb_ref, alog_ref, dsk_ref, ng_ref, exp_ref,
                tri2_ref, ones_ref, bdm_ref, tri_ref, triu_ref,
                o_ref,
                st_scr, xs_scr, xdt_scr, a_scr, y_scr):
    n_seq, rows_seq, ds = o_ref.shape
    tl = n_seq * rows_seq
    _, d_state, _ = st_scr.shape
    n_grp = ds // BLOCK
    dc = ds + 2 * n_grp * d_state

    @pl.when(pl.program_id(1) == 0)
    def _():
        st_scr[...] = jnp.zeros_like(st_scr)

    cols_of_p = lambda lo, hi: p_ref[:, :, lo:hi].reshape(tl, hi - lo)
    xs = cols_of_p(ds, 2 * ds)
    dtr = cols_of_p(ds + dc, ds + dc + LANE) + dtb_ref[...]
    dt_c = jnp.maximum(dtr, 0.0) + jnp.log1p(jnp.exp(-jnp.abs(dtr)))
    dt = _dot(_split2_cols(dt_c), exp_ref[...])
    xs_scr[...] = xs
    xdt_scr[...] = xs * dt
    a_scr[...] = dt * (-jnp.exp(alog_ref[...]))
    bc0 = 2 * ds

    incl = tri_ref[...] > 0.5
    upper = triu_ref[...] > 0.5

    order = [(s, c, g) for c in range(rows_seq // CHUNK) for s in range(n_seq) for g in range(n_grp)]
    probs = [(slice(s * rows_seq + c * CHUNK, s * rows_seq + (c + 1) * CHUNK), g, s * n_grp + g)
             for s, c, g in order]
    gcols = lambda g: slice(g * BLOCK, (g + 1) * BLOCK)
    from_p = lambda col0: [p_ref[s, c * CHUNK:(c + 1) * CHUNK, col0 + g * d_state:col0 + (g + 1) * d_state]
                           .astype(BF16) for s, c, g in order]

    a_c = [a_scr[rows, gcols(g)] for rows, g, _ in probs]
    e1 = _each(lambda a: _dot(tri2_ref[...], _split2_rows(a)), a_c)
    e2 = _each(lambda a: _dot(ones_ref[...], _split2_rows(jnp.where(upper, a, 0.0))), a_c)
    l_w = _each(lambda x, y: jnp.exp(jnp.where(incl, x - y, -jnp.inf)), e1, e2)
    b_g = from_p(bc0)
    c_g = from_p(bc0 + n_grp * d_state)
    scores = _each(lambda c, b: _dot_nt(c, jnp.concatenate([b, b, b, b], axis=0)), c_g, b_g)
    xdt_c = [xdt_scr[rows, gcols(g)] for rows, g, _ in probs]
    y_diag = _each(lambda x, l, y: _dot((x * l).astype(BF16), _block_diag(y.astype(BF16), bdm_ref[...])),
                   scores, l_w, xdt_c)
    new = _each(lambda b, x, e: _dot_tn(b, (x * jnp.exp(e[CHUNK - 1:CHUNK, :] - e)).astype(BF16)),
                b_g, xdt_c, e1)

    sts = [st_scr[j] for j in range(n_seq * n_grp)]
    stb = []
    for (_, _, j), e, n in zip(probs, e1, new):
        stb.append(sts[j].astype(BF16))
        sts[j] = sts[j] * jnp.exp(e[CHUNK - 1:CHUNK, :]) + n
    for j in range(n_seq * n_grp):
        st_scr[j] = sts[j]
    for (rows, g, _), c, st, yd, e in zip(probs, c_g, stb, y_diag, e1):
        y_scr[rows, gcols(g)] = yd + _dot(c, st) * jnp.exp(e)

    z = cols_of_p(0, ds)
    y = (y_scr[...] + dsk_ref[...] * xs_scr[...]) * (z * jax.nn.sigmoid(z))
    o_ref[...] = _rms(y, ng_ref[...], SSM_NORM_EPS).astype(BF16).reshape(n_seq, rows_seq, ds)


MIX_SEQS = 2


def _mixers(p, batch, rwkv_params, ssd_params, consts, n_grp, d_state):
    t, wp2 = p.shape
    wp = wp2 // 2
    dr = rwkv_params["w0"].shape[1]
    ds = ssd_params["norm_g"].shape[1]
    dc = ds + 2 * n_grp * d_state
    seq = t // batch
    rows = MIX_ROWS // MIX_SEQS
    assert dr % BLOCK == 0 and wp == 3 * dr + 2 * LANE and ds == n_grp * BLOCK and ds + dc + LANE <= wp
    assert t == batch * seq and batch % MIX_SEQS == 0 and seq % rows == 0 and rows % CHUNK == 0
    r_ops = ([rwkv_params[n] for n in ["w0", "w2", "a0", "a2", "g2", "k_k", "k_a", "r_k", "gn_g", "gn_b"]]
             + [consts[n] for n in ["tri2", "bdm", "bdm32", "trm", "eye"]])
    s_ops = ([ssd_params[n] for n in ["dt_bias", "a_log", "d_skip", "norm_g", "expand"]]
             + [consts[n] for n in ["tri2", "ones", "bdm", "tri", "triu"]])
    tl = MIX_ROWS
    r_scr = ([pltpu.VMEM((MIX_SEQS * (dr // BLOCK), BLOCK, BLOCK), F32)]
             + [pltpu.VMEM((tl, dr), F32) for _ in range(7)])
    s_scr = [pltpu.VMEM((MIX_SEQS * n_grp, d_state, BLOCK), F32)] + [pltpu.VMEM((tl, ds), F32) for _ in range(4)]

    def body(pr_ref, ps_ref, *refs):
        r_in, refs = refs[:len(r_ops)], refs[len(r_ops):]
        s_in, refs = refs[:len(s_ops)], refs[len(s_ops):]
        (or_ref, os_ref), scr = refs[:2], refs[2:]
        _rwkv_kernel(pr_ref, *r_in, or_ref, *scr[:len(r_scr)])
        _ssd_kernel(ps_ref, *s_in, os_ref, *scr[len(r_scr):])

    tile = lambda col: pl.BlockSpec((MIX_SEQS, rows, wp), lambda b, l: (b, l, col))
    out = lambda w: pl.BlockSpec((MIX_SEQS, rows, w), lambda b, l: (b, l, 0))
    p3 = p.reshape(batch, seq, wp2)
    y_r, y_s = pl.pallas_call(
        body,
        grid=(batch // MIX_SEQS, seq // rows),
        in_specs=[tile(0), tile(1)] + [_resident(o.shape) for o in r_ops + s_ops],
        out_specs=[out(dr), out(ds)],
        out_shape=[jax.ShapeDtypeStruct((batch, seq, dr), BF16), jax.ShapeDtypeStruct((batch, seq, ds), BF16)],
        scratch_shapes=r_scr + s_scr,
        compiler_params=pltpu.CompilerParams(
            dimension_semantics=("arbitrary", "arbitrary"), vmem_limit_bytes=VMEM_LIMIT),
        name="mixers",
    )(p3, p3, *r_ops, *s_ops)
    return y_r.reshape(t, dr), y_s.reshape(t, ds)


def _mask_consts():
    row = jnp.arange(BLOCK)[:, None]
    col = jnp.arange(BLOCK)[None, :]
    t = jnp.arange(CHUNK)[:, None]
    s = col % CHUNK
    tri64 = (jnp.arange(CHUNK)[None, :] <= t)
    bdm = row // CHUNK == col // CHUNK
    return {
        "bdm": bdm.astype(BF16),
        "bdm32": bdm.astype(F32),
        "trm": jnp.concatenate([s < t, s <= t], axis=0).astype(F32),
        "tri": (s <= t).astype(F32),
        "triu": (s >= t).astype(F32),
        "eye": (s == t).astype(F32),
        "tri2": jnp.concatenate([tri64, tri64], axis=1).astype(BF16),
        "ones": jnp.ones((CHUNK, 2 * CHUNK), BF16),
    }


def kernel(x, norm_ffn1, ffn1_w_gate, ffn1_w_up, ffn1_w_down, norm_mix, w_in, rwkv_mu, rwkv_w0, rwkv_w2,
           rwkv_a0, rwkv_a2, rwkv_g2, rwkv_k_k, rwkv_k_a, rwkv_r_k, rwkv_gn_g, rwkv_gn_b, ssm_conv_w,
           ssm_conv_b, ssm_dt_bias, ssm_a_log, ssm_d, ssm_norm, w_out, norm_ffn2, ffn2_w_gate, ffn2_w_up,
           ffn2_w_down, norm_final):
    batch, seq, d = x.shape
    depth = norm_ffn1.shape[0]
    dr = rwkv_w0.shape[1]
    ds = ssm_norm.shape[1]
    dc = ssm_conv_w.shape[2]
    n_heads_s = ssm_a_log.shape[1]
    lora_w = rwkv_w2.shape[1]
    lora_a = rwkv_a2.shape[1]
    lora_g = rwkv_g2.shape[1]
    n_grp = ds // BLOCK
    d_state = (dc - ds) // (2 * n_grp)
    wp = 3 * dr + 2 * LANE
    assert ds // n_heads_s == HEAD_DIM and lora_w + lora_a == LANE and lora_g == LANE
    assert n_heads_s <= LANE and ds + dc + LANE <= wp and seq % MIX_ROWS == 0

    consts = _mask_consts()
    lane_head = jnp.arange(LANE)[:, None] == (jnp.arange(ds) // HEAD_DIM)[None, :]
    expand = jnp.concatenate([lane_head, lane_head], axis=0).astype(BF16)
    row = lambda a: a.reshape(1, -1)
    rep = lambda a: jnp.repeat(a, HEAD_DIM).reshape(1, -1)

    xt = x.reshape(batch * seq, d)
    for i in range(depth):
        x1 = _ffn1(xt, row(norm_ffn1[i]), ffn1_w_gate, ffn1_w_up, ffn1_w_down, i)

        p = _inproj(x1, row(norm_mix[i]), w_in, i, 2 * wp, row(rwkv_mu[i]), ssm_conv_w[i],
                    row(ssm_conv_b[i]), seq, wp, wp + ds)

        rwkv_params = {
            "w0": row(rwkv_w0[i]), "a0": row(rwkv_a0[i]),
            "w2": jnp.pad(rwkv_w2[i], ((0, lora_a), (0, 0))).astype(BF16),
            "a2": jnp.pad(rwkv_a2[i], ((lora_w, 0), (0, 0))).astype(BF16),
            "g2": rwkv_g2[i].astype(BF16),
            "k_k": row(rwkv_k_k[i]), "k_a": row(rwkv_k_a[i]), "r_k": row(rwkv_r_k[i]),
            "gn_g": row(rwkv_gn_g[i]), "gn_b": row(rwkv_gn_b[i]),
        }
        ssd_params = {
            "dt_bias": jnp.pad(ssm_dt_bias[i], (0, LANE - n_heads_s)).reshape(1, -1),
            "a_log": rep(ssm_a_log[i]), "d_skip": rep(ssm_d[i]), "norm_g": row(ssm_norm[i]),
            "expand": expand,
        }
        y_r, y_s = _mixers(p, batch, rwkv_params, ssd_params, consts, n_grp, d_state)

        assert i == depth - 1, "the fused final norm assumes a single layer"
        xt = _ffn2(x1, y_r, y_s, w_out, row(norm_ffn2[i]), ffn2_w_gate, ffn2_w_up, ffn2_w_down,
                   row(norm_final), i)
    return xt.reshape(batch, seq, d)
```

```python
import functools

import jax
import jax.numpy as jnp
from jax import lax
from jax.experimental import pallas as pl
from jax.experimental.pallas import tpu as pltpu

F32 = jnp.float32
BF16 = jnp.bfloat16

NORM_EPS = 1e-6
RWKV_GN_EPS = 64e-5
SSM_NORM_EPS = 1e-5

HEAD_DIM = 64
CHUNK = 64
BLOCK = 256
LANE = 128
BF16_ROWS = 16
HALO = 8
MIX_ROWS = 512

NT_DIMS = (((1,), (1,)), ((), ()))
TN_DIMS = (((0,), (0,)), ((), ()))

VMEM_LIMIT = 56 * 1024 * 1024


def _dot(a, b):
    return jnp.dot(a, b, preferred_element_type=F32)


def _dot_nt(a, b):
    return lax.dot_general(a, b, NT_DIMS, preferred_element_type=F32)


def _dot_tn(a, b):
    return lax.dot_general(a, b, TN_DIMS, preferred_element_type=F32)


def _rms(x, g, eps):
    return x * lax.rsqrt(jnp.mean(x * x, axis=-1, keepdims=True) + eps) * g


def _split2_rows(x):
    hi = x.astype(BF16)
    lo = (x - hi.astype(F32)).astype(BF16)
    return jnp.concatenate([hi, lo], axis=0)


def _split2_cols(x):
    hi = x.astype(BF16)
    lo = (x - hi.astype(F32)).astype(BF16)
    return jnp.concatenate([hi, lo], axis=1)


def _head_sum(x, bdm_ref):
    xb = x.astype(BF16)
    return jnp.concatenate([_dot(xb[:, c:c + BLOCK], bdm_ref[...]) for c in range(0, x.shape[1], BLOCK)],
                           axis=1)


def _block_diag(xb, bdm):
    return jnp.concatenate([xb, xb, xb, xb], axis=0) * bdm


def _each(f, *lists):
    return [f(*args) for args in zip(*lists)]


def _resident(shape):
    zeros = (0,) * len(shape)
    return pl.BlockSpec(shape, lambda *_: zeros, pipeline_mode=pl.Buffered(1))


FF_CHUNK = 256
FFN_ROWS = 512


def _norm_split(x, g, h_scr, r_scr):
    h_scr[...] = (x * g).astype(BF16)
    r = lax.rsqrt(jnp.mean(x * x, axis=-1, keepdims=True) + NORM_EPS)
    r_scr[...] = jnp.broadcast_to(r, r_scr.shape)


def _swiglu(h_scr, r_scr, wg_ref, wu_ref, wd_ref, act_scr):
    d_ff = wg_ref.shape[1]
    for c in range(d_ff // FF_CHUNK):
        cs = slice(c * FF_CHUNK, (c + 1) * FF_CHUNK)
        r = jnp.concatenate([r_scr[...]] * (FF_CHUNK // LANE), axis=1)
        gate = _dot(h_scr[...], wg_ref[:, cs]) * r
        up = _dot(h_scr[...], wu_ref[:, cs]) * r
        act_scr[:, cs] = (gate * jax.nn.sigmoid(gate) * up).astype(BF16)
    return _dot(act_scr[...], wd_ref[...])


STAGE_ROWS = 512


def _stage_weights(w_hbms, layer, w_scrs, stage_scr, sems):
    step = stage_scr.shape[1]
    pieces = [(k, r0, min(step, w.shape[1] - r0))
              for k, w in enumerate(w_hbms) for r0 in range(0, w.shape[1], step)]

    def view(c):
        k, _, rows = pieces[c]
        return stage_scr.at[c % 2, pl.ds(0, rows), pl.ds(0, w_hbms[k].shape[2])]

    def dma(c):
        k, r0, rows = pieces[c]
        return pltpu.make_async_copy(w_hbms[k].at[layer, pl.ds(r0, rows), :], view(c), sems.at[c % 2])

    dma(0).start()
    for c, (k, r0, rows) in enumerate(pieces):
        if c + 1 < len(pieces):
            dma(c + 1).start()
        dma(c).wait()
        w_scrs[k][r0:r0 + rows, 0:w_hbms[k].shape[2]] = view(c)[...].astype(BF16)


def _ffn1_kernel(x_ref, g_ref, wg_hbm, wu_hbm, wd_hbm, o_ref,
                 h_scr, r_scr, act_scr, wg_scr, wu_scr, wd_scr, stage_scr, sems, *, layer):
    @pl.when(pl.program_id(0) == 0)
    def _():
        _stage_weights((wg_hbm, wu_hbm, wd_hbm), layer, (wg_scr, wu_scr, wd_scr), stage_scr, sems)

    _norm_split(x_ref[...], g_ref[...], h_scr, r_scr)
    o_ref[...] = x_ref[...] + 0.5 * _swiglu(h_scr, r_scr, wg_scr, wu_scr, wd_scr, act_scr)


def _ffn2_kernel(x_ref, yr_ref, ys_ref, wo_hbm, g_ref, wg_hbm, wu_hbm, wd_hbm, gf_ref, o_ref,
                 h_scr, r_scr, act_scr, x_scr, wo_scr, wg_scr, wu_scr, wd_scr, stage_scr, sems, *, layer):
    @pl.when(pl.program_id(0) == 0)
    def _():
        _stage_weights((wo_hbm, wg_hbm, wu_hbm, wd_hbm), layer, (wo_scr, wg_scr, wu_scr, wd_scr),
                       stage_scr, sems)

    dr = yr_ref.shape[1]
    x2 = x_ref[...] + _dot(yr_ref[...], wo_scr[0:dr, :]) + _dot(ys_ref[...], wo_scr[dr:, :])
    x_scr[...] = x2
    _norm_split(x2, g_ref[...], h_scr, r_scr)
    y = _swiglu(h_scr, r_scr, wg_scr, wu_scr, wd_scr, act_scr)
    o_ref[...] = _rms(x_scr[...] + 0.5 * y, gf_ref[...], NORM_EPS)


def _weight_scratch(*weights):
    widest = max(w.shape[2] for w in weights)
    return ([pltpu.VMEM(w.shape[1:], BF16) for w in weights]
            + [pltpu.VMEM((2, STAGE_ROWS, widest), F32), pltpu.SemaphoreType.DMA((2,))])


def _ffn1(x, g, wg, wu, wd, layer):
    t, d = x.shape
    d_ff = wg.shape[2]
    tm = FFN_ROWS
    assert t % tm == 0 and d_ff % FF_CHUNK == 0
    row = lambda i: (i, 0)
    hbm = pl.BlockSpec(memory_space=pl.ANY)
    return pl.pallas_call(
        functools.partial(_ffn1_kernel, layer=layer),
        grid=(t // tm,),
        in_specs=[pl.BlockSpec((tm, d), row), _resident(g.shape), hbm, hbm, hbm],
        out_specs=pl.BlockSpec((tm, d), row),
        out_shape=jax.ShapeDtypeStruct((t, d), F32),
        scratch_shapes=[pltpu.VMEM((tm, d), BF16), pltpu.VMEM((tm, LANE), F32), pltpu.VMEM((tm, d_ff), BF16)]
        + _weight_scratch(wg, wu, wd),
        compiler_params=pltpu.CompilerParams(
            dimension_semantics=("arbitrary",), vmem_limit_bytes=VMEM_LIMIT),
        name="ffn1",
    )(x, g, wg, wu, wd)


def _ffn2(x, yr, ys, wo, g, wg, wu, wd, gf, layer):
    t, d = x.shape
    d_ff = wg.shape[2]
    dm = yr.shape[1]
    tm = FFN_ROWS
    assert t % tm == 0 and d_ff % FF_CHUNK == 0 and wo.shape[1] == 2 * dm
    row = lambda i: (i, 0)
    hbm = pl.BlockSpec(memory_space=pl.ANY)
    return pl.pallas_call(
        functools.partial(_ffn2_kernel, layer=layer),
        grid=(t // tm,),
        in_specs=[pl.BlockSpec((tm, d), row), pl.BlockSpec((tm, dm), row), pl.BlockSpec((tm, dm), row),
                  hbm, _resident(g.shape), hbm, hbm, hbm, _resident(gf.shape)],
        out_specs=pl.BlockSpec((tm, d), row),
        out_shape=jax.ShapeDtypeStruct((t, d), F32),
        scratch_shapes=[pltpu.VMEM((tm, d), BF16), pltpu.VMEM((tm, LANE), F32), pltpu.VMEM((tm, d_ff), BF16),
                        pltpu.VMEM((tm, d), F32)] + _weight_scratch(wo, wg, wu, wd),
        compiler_params=pltpu.CompilerParams(
            dimension_semantics=("arbitrary",), vmem_limit_bytes=VMEM_LIMIT),
        name="ffn2",
    )(x, yr, ys, wo, g, wg, wu, wd, gf)


IN_CHUNK = 512


def _col_chunks(lo, hi):
    return [(c0, min(IN_CHUNK, hi - c0)) for c0 in range(lo, hi, IN_CHUNK)]


def _inproj_kernel(x_ref, g_ref, wt_hbm, mu_ref, cw_ref, cb_ref, p_ref, h_scr, ext_scr, raw_scr,
                   wt_scr, stage_scr, sems, *, layer, tiles_per_seq, wr, conv_lo):
    tm = x_ref.shape[0]
    dc = cw_ref.shape[1]
    width = cw_ref.shape[0]
    step = pl.program_id(0)
    h_scr[...] = _rms(x_ref[...], g_ref[...], NORM_EPS).astype(BF16)

    @pl.when(step == 0)
    def _():
        raw_scr[...] = jnp.zeros_like(raw_scr)
        pad_lo = wt_hbm.shape[1] // BF16_ROWS * BF16_ROWS
        wt_scr[pad_lo:, :] = jnp.zeros((wt_scr.shape[0] - pad_lo, wt_scr.shape[1]), BF16)
        _stage_weights((wt_hbm,), layer, (wt_scr,), stage_scr, sems)

    @pl.when(lax.rem(step + tiles_per_seq - 1, tiles_per_seq) == 0)
    def _():
        ext_scr[...] = jnp.zeros_like(ext_scr)

    def behind_halo(cols, pc):
        ext = jnp.concatenate([ext_scr[:, cols], pc], axis=0)
        ext_scr[:, cols] = pc[tm - HALO:, :]
        return ext

    def lerp(c0, n, pc):
        cs = slice(c0, c0 + n)
        shifted = pltpu.roll(behind_halo(cs, pc), 1, axis=0)[HALO:, :]
        p_ref[:, cs] = pc + (shifted - pc) * mu_ref[:, cs]

    def conv_silu(c0, n, pc):
        j = c0 - conv_lo
        cw = slice(j, j + n)
        ext = behind_halo(slice(wr + j, wr + j + n), pc)
        tap = lambda i: cw_ref[width - 1 - i:width - i, cw]
        if width == 4:
            ext1 = pltpu.roll(ext, 1, axis=0)
            far = pltpu.roll(ext * tap(2) + ext1 * tap(3), 2, axis=0)[HALO:, :]
            conv = pc * tap(0) + ext1[HALO:, :] * tap(1) + far + cb_ref[:, cw]
        else:
            conv = pc * tap(0) + cb_ref[:, cw]
            for i in range(1, width):
                conv = conv + pltpu.roll(ext, i, axis=0)[HALO:, :] * tap(i)
        p_ref[:, c0:c0 + n] = conv * jax.nn.sigmoid(conv)

    def plain(c0, n, pc):
        p_ref[:, c0:c0 + n] = pc

    convs = [(conv_silu, c) for c in _col_chunks(conv_lo, conv_lo + dc)]
    light = ([(lerp, c) for c in _col_chunks(0, wr)]
             + [(plain, c) for c in _col_chunks(wr, conv_lo) + _col_chunks(conv_lo + dc, p_ref.shape[1])])
    per_conv = len(light) // max(len(convs), 1)
    order = []
    for n, cv in enumerate(convs):
        order += [cv] + light[n * per_conv:(n + 1) * per_conv]
    order += light[len(convs) * per_conv:]
    for finish, (c0, n) in order:
        before = raw_scr[:, c0:c0 + n]
        raw_scr[:, c0:c0 + n] = _dot_nt(h_scr[...], wt_scr[c0:c0 + n, :])
        finish(c0, n, before)


IN_STAGE_ROWS = 1024


def _inproj(x, g, wt_all, layer, width_out, mu, conv_w, conv_b, seq, wr, conv_lo):
    t, d = x.shape
    tm = 512
    dc = conv_w.shape[1]
    n_cols = wt_all.shape[1]
    assert t % tm == 0 and seq % tm == 0 and conv_w.shape[0] <= HALO
    assert all(v % LANE == 0 for v in (wr, conv_lo, dc, width_out)) and wr <= conv_lo <= n_cols - dc
    assert n_cols <= width_out
    body = functools.partial(_inproj_kernel, layer=layer, tiles_per_seq=seq // tm, wr=wr, conv_lo=conv_lo)
    n = t // tm
    return pl.pallas_call(
        body,
        grid=(n + 1,),
        in_specs=[pl.BlockSpec((tm, d), lambda i: (jnp.minimum(i, n - 1), 0)), _resident(g.shape),
                  pl.BlockSpec(memory_space=pl.ANY), _resident(mu.shape), _resident(conv_w.shape),
                  _resident(conv_b.shape)],
        out_specs=pl.BlockSpec((tm, width_out), lambda i: (jnp.maximum(i - 1, 0), 0)),
        out_shape=jax.ShapeDtypeStruct((t, width_out), F32),
        scratch_shapes=[pltpu.VMEM((tm, d), BF16), pltpu.VMEM((HALO, wr + dc), F32),
                        pltpu.VMEM((tm, width_out), F32), pltpu.VMEM((width_out, d), BF16),
                        pltpu.VMEM((2, IN_STAGE_ROWS, d), F32), pltpu.SemaphoreType.DMA((2,))],
        compiler_params=pltpu.CompilerParams(
            dimension_semantics=("arbitrary",), vmem_limit_bytes=VMEM_LIMIT),
        name="in_proj",
    )(x, g, wt_all, mu, conv_w, conv_b)


def _rwkv_kernel(p_ref, w0_ref, w2_ref, a0_ref, a2_ref, g2_ref, kk_ref, ka_ref, rk_ref,
                 gng_ref, gnb_ref, tri2_ref, bdm_ref, bdm32_ref, trm_ref, eye_ref,
                 o_ref,
                 zt_scr, k_scr, a_scr, b_scr, ld_scr, y_scr, bonus_scr, gate_scr):
    n_seq, rows_seq, dr = o_ref.shape
    tl = n_seq * rows_seq

    @pl.when(pl.program_id(1) == 0)
    def _():
        zt_scr[...] = jnp.zeros_like(zt_scr)

    cols_of_p = lambda lo, hi: p_ref[:, :, lo:hi].reshape(tl, hi - lo)
    r = cols_of_p(0, dr)
    k = cols_of_p(dr, 2 * dr)
    v = cols_of_p(2 * dr, 3 * dr)
    wa = cols_of_p(3 * dr, 3 * dr + LANE)
    gd = cols_of_p(3 * dr + LANE, 3 * dr + 2 * LANE)

    wpre = w0_ref[...] + _dot(jnp.tanh(wa).astype(BF16), w2_ref[...])
    ld_scr[...] = jax.nn.sigmoid(wpre) * (-jnp.exp(-0.5))
    iclr = jax.nn.sigmoid(a0_ref[...] + _dot(wa.astype(BF16), a2_ref[...]))
    gate_scr[...] = _dot(jax.nn.sigmoid(gd).astype(BF16), g2_ref[...])

    kkv = k * kk_ref[...]
    kkn = kkv * lax.rsqrt(jnp.maximum(_head_sum(kkv * kkv, bdm_ref), 1e-24))
    kmod = k * (1.0 + (iclr - 1.0) * ka_ref[...])
    k_scr[...] = kmod
    a_scr[...] = -kkn
    b_scr[...] = kkn * iclr
    bonus_scr[...] = _head_sum(r * kmod * rk_ref[...], bdm_ref) * v

    bdmask32 = bdm32_ref[...] > 0.5
    trmask = trm_ref[...] > 0.5
    bd = lambda xb: _block_diag(xb, bdm_ref[...])
    n_q = dr // BLOCK

    probs = [(slice(s * rows_seq + c * CHUNK, s * rows_seq + (c + 1) * CHUNK),
              slice(q * BLOCK, (q + 1) * BLOCK), s * n_q + q)
             for c in range(rows_seq // CHUNK) for s in range(n_seq) for q in range(n_q)]
    tile = lambda scr: [scr[rows, cols] for rows, cols, _ in probs]
    from_p = lambda col0: [p_ref[s, c * CHUNK:(c + 1) * CHUNK, col0 + q * BLOCK:col0 + (q + 1) * BLOCK]
                           for c in range(rows_seq // CHUNK) for s in range(n_seq) for q in range(n_q)]

    ld = tile(ld_scr)
    cum = _each(lambda x: _dot(tri2_ref[...], _split2_rows(x)), ld)
    b_c = tile(b_scr)
    k_c = tile(k_scr)
    vb = _each(lambda x: x.astype(BF16), from_p(2 * dr))
    atb = _each(lambda a, cm, l: (a * jnp.exp(cm - l)).astype(BF16), tile(a_scr), cum, ld)
    rtb = _each(lambda x, cm: (x * jnp.exp(cm)).astype(BF16), from_p(0), cum)
    inv = _each(lambda cm: jnp.exp(-cm), cum)
    g_end = _each(lambda cm: jnp.exp(cm[CHUNK - 1:CHUNK, :]), cum)
    to_end = _each(lambda cm: jnp.exp(cm[CHUNK - 1:CHUNK, :] - cm), cum)
    ar = _each(lambda a, x: jnp.concatenate([a, x], axis=0), atb, rtb)
    masked = lambda x: jnp.where(trmask, x, 0.0)
    sb32 = _each(lambda a, b, i: masked(_dot_nt(a, bd((b * i).astype(BF16)))), ar, b_c, inv)
    sk = _each(lambda a, x, i: masked(_dot_nt(a, bd((x * i).astype(BF16)))).astype(BF16), ar, k_c, inv)
    kv = _each(lambda x, y: _dot(x, bd(y)), sk, vb)
    t_inv = _each(lambda x: eye_ref[...] + x[0:CHUNK], sb32)
    pw = _each(lambda x: x[0:CHUNK].astype(BF16), sb32)
    pw = _each(lambda x: _dot(x, bd(x)).astype(BF16), pw)
    for i in range(1, 5):
        res = _each(lambda t, x: _dot(jnp.concatenate([t.astype(BF16), x], axis=0), bd(x)), t_inv, pw)
        t_inv = _each(lambda t, x: t + x[0:CHUNK], t_inv, res)
        pw = _each(lambda x: x[CHUNK:].astype(BF16), res)
    t_inv = _each(lambda t, x: t + _dot(t.astype(BF16), bd(x)), t_inv, pw)
    xx = _each(lambda t, a, x: _dot(t.astype(BF16),
                                    jnp.concatenate([bd(a), bd(x[0:CHUNK].astype(BF16))], axis=1)),
               t_inv, atb, kv)
    x1b = _each(lambda x: x[:, 0:BLOCK].astype(BF16), xx)
    x2 = _each(lambda x: x[:, BLOCK:], xx)
    bgb = _each(lambda b, e: (b * e).astype(BF16), b_c, to_end)
    kgb = _each(lambda x, e: (x * e).astype(BF16), k_c, to_end)
    p_t = _each(lambda x, b: _dot_tn(x, b).astype(BF16), x1b, bgb)
    c_t = _each(lambda x, y, b, kg: _dot_tn(jnp.concatenate([x.astype(BF16), y], axis=0),
                                            jnp.concatenate([b, kg], axis=0)), x2, vb, bgb, kgb)

    zs = [zt_scr[j] for j in range(n_seq * n_q)]
    zb = []
    for (_, _, j), g, pt, ct in zip(probs, g_end, p_t, c_t):
        zb.append(zs[j].astype(BF16))
        zs[j] = zs[j] * g + jnp.where(bdmask32, _dot(zb[-1], pt) + ct, 0.0)
    for j in range(n_seq * n_q):
        zt_scr[j] = zs[j]
    xz = _each(lambda x, rt, z: _dot_nt(jnp.concatenate([x, rt], axis=0), z), x1b, rtb, zb)
    ubd = _each(lambda m, x: bd((m[0:CHUNK] + x).astype(BF16)), xz, x2)
    for (rows, cols, _), m, n_rb, u, o in zip(probs, xz, sb32, ubd, kv):
        y_scr[rows, cols] = m[CHUNK:] + _dot(n_rb[CHUNK:].astype(BF16), u) + o[CHUNK:]

    y = y_scr[...]
    inv_n = 1.0 / HEAD_DIM
    mean = _head_sum(y, bdm_ref) * inv_n
    yc = y - mean
    var = _head_sum(yc * yc, bdm_ref) * inv_n
    yn = yc * lax.rsqrt(var + RWKV_GN_EPS) * gng_ref[...] + gnb_ref[...]
    o_ref[...] = ((yn + bonus_scr[...]) * gate_scr[...]).astype(BF16).reshape(n_seq, rows_seq, dr)


def _ssd_kernel(p_ref, dtb_ref, alog_ref, dsk_ref, ng_ref, exp_ref,
                tri2_ref, ones_ref, bdm_ref, tri_ref, triu_ref,
                o_ref,
                st_scr, xs_scr, xdt_scr, a_scr, y_scr):
    n_seq, rows_seq, ds = o_ref.shape
    tl = n_seq * rows_seq
    _, d_state, _ = st_scr.shape
    n_grp = ds // BLOCK
    dc = ds + 2 * n_grp * d_state

    @pl.when(pl.program_id(1) == 0)
    def _():
        st_scr[...] = jnp.zeros_like(st_scr)

    cols_of_p = lambda lo, hi: p_ref[:, :, lo:hi].reshape(tl, hi - lo)
    xs = cols_of_p(ds, 2 * ds)
    dtr = cols_of_p(ds + dc, ds + dc + LANE) + dtb_ref[...]
    dt_c = jnp.maximum(dtr, 0.0) + jnp.log1p(jnp.exp(-jnp.abs(dtr)))
    dt = _dot(_split2_cols(dt_c), exp_ref[...])
    xs_scr[...] = xs
    xdt_scr[...] = xs * dt
    a_scr[...] = dt * (-jnp.exp(alog_ref[...]))
    bc0 = 2 * ds

    incl = tri_ref[...] > 0.5
    upper = triu_ref[...] > 0.5

    order = [(s, c, g) for c in range(rows_seq // CHUNK) for s in range(n_seq) for g in range(n_grp)]
    probs = [(slice(s * rows_seq + c * CHUNK, s * rows_seq + (c + 1) * CHUNK), g, s * n_grp + g)
             for s, c, g in order]
    gcols = lambda g: slice(g * BLOCK, (g + 1) * BLOCK)
    from_p = lambda col0: [p_ref[s, c * CHUNK:(c + 1) * CHUNK, col0 + g * d_state:col0 + (g + 1) * d_state]
                           .astype(BF16) for s, c, g in order]

    a_c = [a_scr[rows, gcols(g)] for rows, g, _ in probs]
    e1 = _each(lambda a: _dot(tri2_ref[...], _split2_rows(a)), a_c)
    e2 = _each(lambda a: _dot(ones_ref[...], _split2_rows(jnp.where(upper, a, 0.0))), a_c)
    l_w = _each(lambda x, y: jnp.exp(jnp.where(incl, x - y, -jnp.inf)), e1, e2)
    b_g = from_p(bc0)
    c_g = from_p(bc0 + n_grp * d_state)
    scores = _each(lambda c, b: _dot_nt(c, jnp.concatenate([b, b, b, b], axis=0)), c_g, b_g)
    xdt_c = [xdt_scr[rows, gcols(g)] for rows, g, _ in probs]
    y_diag = _each(lambda x, l, y: _dot((x * l).astype(BF16), _block_diag(y.astype(BF16), bdm_ref[...])),
                   scores, l_w, xdt_c)
    new = _each(lambda b, x, e: _dot_tn(b, (x * jnp.exp(e[CHUNK - 1:CHUNK, :] - e)).astype(BF16)),
                b_g, xdt_c, e1)

    sts = [st_scr[j] for j in range(n_seq * n_grp)]
    stb = []
    for (_, _, j), e, n in zip(probs, e1, new):
        stb.append(sts[j].astype(BF16))
        sts[j] = sts[j] * jnp.exp(e[CHUNK - 1:CHUNK, :]) + n
    for j in range(n_seq * n_grp):
        st_scr[j] = sts[j]
    for (rows, g, _), c, st, yd, e in zip(probs, c_g, stb, y_diag, e1):
        y_scr[rows, gcols(g)] = yd + _dot(c, st) * jnp.exp(e)

    z = cols_of_p(0, ds)
    y = (y_scr[...] + dsk_ref[...] * xs_scr[...]) * (z * jax.nn.sigmoid(z))
    o_ref[...] = _rms(y, ng_ref[...], SSM_NORM_EPS).astype(BF16).reshape(n_seq, rows_seq, ds)


MIX_SEQS = 2


def _mixers(p, batch, rwkv_params, ssd_params, consts, n_grp, d_state):
    t, wp2 = p.shape
    wp = wp2 // 2
    dr = rwkv_params["w0"].shape[1]
    ds = ssd_params["norm_g"].shape[1]
    dc = ds + 2 * n_grp * d_state
    seq = t // batch
    rows = MIX_ROWS // MIX_SEQS
    assert dr % BLOCK == 0 and wp == 3 * dr + 2 * LANE and ds == n_grp * BLOCK and ds + dc + LANE <= wp
    assert t == batch * seq and batch % MIX_SEQS == 0 and seq % rows == 0 and rows % CHUNK == 0
    r_ops = ([rwkv_params[n] for n in ["w0", "w2", "a0", "a2", "g2", "k_k", "k_a", "r_k", "gn_g", "gn_b"]]
             + [consts[n] for n in ["tri2", "bdm", "bdm32", "trm", "eye"]])
    s_ops = ([ssd_params[n] for n in ["dt_bias", "a_log", "d_skip", "norm_g", "expand"]]
             + [consts[n] for n in ["tri2", "ones", "bdm", "tri", "triu"]])
    tl = MIX_ROWS
    r_scr = ([pltpu.VMEM((MIX_SEQS * (dr // BLOCK), BLOCK, BLOCK), F32)]
             + [pltpu.VMEM((tl, dr), F32) for _ in range(7)])
    s_scr = [pltpu.VMEM((MIX_SEQS * n_grp, d_state, BLOCK), F32)] + [pltpu.VMEM((tl, ds), F32) for _ in range(4)]

    def body(pr_ref, ps_ref, *refs):
        r_in, refs = refs[:len(r_ops)], refs[len(r_ops):]
        s_in, refs = refs[:len(s_ops)], refs[len(s_ops):]
        (or_ref, os_ref), scr = refs[:2], refs[2:]
        _rwkv_kernel(pr_ref, *r_in, or_ref, *scr[:len(r_scr)])
        _ssd_kernel(ps_ref, *s_in, os_ref, *scr[len(r_scr):])

    tile = lambda col: pl.BlockSpec((MIX_SEQS, rows, wp), lambda b, l: (b, l, col))
    out = lambda w: pl.BlockSpec((MIX_SEQS, rows, w), lambda b, l: (b, l, 0))
    p3 = p.reshape(batch, seq, wp2)
    y_r, y_s = pl.pallas_call(
        body,
        grid=(batch // MIX_SEQS, seq // rows),
        in_specs=[tile(0), tile(1)] + [_resident(o.shape) for o in r_ops + s_ops],
        out_specs=[out(dr), out(ds)],
        out_shape=[jax.ShapeDtypeStruct((batch, seq, dr), BF16), jax.ShapeDtypeStruct((batch, seq, ds), BF16)],
        scratch_shapes=r_scr + s_scr,
        compiler_params=pltpu.CompilerParams(
            dimension_semantics=("arbitrary", "arbitrary"), vmem_limit_bytes=VMEM_LIMIT),
        name="mixers",
    )(p3, p3, *r_ops, *s_ops)
    return y_r.reshape(t, dr), y_s.reshape(t, ds)


def _mask_consts():
    row = jnp.arange(BLOCK)[:, None]
    col = jnp.arange(BLOCK)[None, :]
    t = jnp.arange(CHUNK)[:, None]
    s = col % CHUNK
    tri64 = (jnp.arange(CHUNK)[None, :] <= t)
    bdm = row // CHUNK == col // CHUNK
    return {
        "bdm": bdm.astype(BF16),
        "bdm32": bdm.astype(F32),
        "trm": jnp.concatenate([s < t, s <= t], axis=0).astype(F32),
        "tri": (s <= t).astype(F32),
        "triu": (s >= t).astype(F32),
        "eye": (s == t).astype(F32),
        "tri2": jnp.concatenate([tri64, tri64], axis=1).astype(BF16),
        "ones": jnp.ones((CHUNK, 2 * CHUNK), BF16),
    }


def kernel(x, norm_ffn1, ffn1_w_gate, ffn1_w_up, ffn1_w_down, norm_mix, w_in, rwkv_mu, rwkv_w0, rwkv_w2,
           rwkv_a0, rwkv_a2, rwkv_g2, rwkv_k_k, rwkv_k_a, rwkv_r_k, rwkv_gn_g, rwkv_gn_b, ssm_conv_w,
           ssm_conv_b, ssm_dt_bias, ssm_a_log, ssm_d, ssm_norm, w_out, norm_ffn2, ffn2_w_gate, ffn2_w_up,
           ffn2_w_down, norm_final):
    batch, seq, d = x.shape
    depth = norm_ffn1.shape[0]
    dr = rwkv_w0.shape[1]
    ds = ssm_norm.shape[1]
    dc = ssm_conv_w.shape[2]
    n_heads_s = ssm_a_log.shape[1]
    lora_w = rwkv_w2.shape[1]
    lora_a = rwkv_a2.shape[1]
    lora_g = rwkv_g2.shape[1]
    n_grp = ds // BLOCK
    d_state = (dc - ds) // (2 * n_grp)
    wp = 3 * dr + 2 * LANE
    assert ds // n_heads_s == HEAD_DIM and lora_w + lora_a == LANE and lora_g == LANE
    assert n_heads_s <= LANE and ds + dc + LANE <= wp and seq % MIX_ROWS == 0

    consts = _mask_consts()
    lane_head = jnp.arange(LANE)[:, None] == (jnp.arange(ds) // HEAD_DIM)[None, :]
    expand = jnp.concatenate([lane_head, lane_head], axis=0).astype(BF16)
    row = lambda a: a.reshape(1, -1)
    rep = lambda a: jnp.repeat(a, HEAD_DIM).reshape(1, -1)

    xt = x.reshape(batch * seq, d)
    for i in range(depth):
        x1 = _ffn1(xt, row(norm_ffn1[i]), ffn1_w_gate, ffn1_w_up, ffn1_w_down, i)

        p = _inproj(x1, row(norm_mix[i]), jnp.swapaxes(w_in, 1, 2), i, 2 * wp, row(rwkv_mu[i]), ssm_conv_w[i],
                    row(ssm_conv_b[i]), seq, wp, wp + ds)

        rwkv_params = {
            "w0": row(rwkv_w0[i]), "a0": row(rwkv_a0[i]),
            "w2": jnp.pad(rwkv_w2[i], ((0, lora_a), (0, 0))).astype(BF16),
            "a2": jnp.pad(rwkv_a2[i], ((lora_w, 0), (0, 0))).astype(BF16),
            "g2": rwkv_g2[i].astype(BF16),
            "k_k": row(rwkv_k_k[i]), "k_a": row(rwkv_k_a[i]), "r_k": row(rwkv_r_k[i]),
            "gn_g": row(rwkv_gn_g[i]), "gn_b": row(rwkv_gn_b[i]),
        }
        ssd_params = {
            "dt_bias": jnp.pad(ssm_dt_bias[i], (0, LANE - n_heads_s)).reshape(1, -1),
            "a_log": rep(ssm_a_log[i]), "d_skip": rep(ssm_d[i]), "norm_g": row(ssm_norm[i]),
            "expand": expand,
        }
        y_r, y_s = _mixers(p, batch, rwkv_params, ssd_params, consts, n_grp, d_state)

        assert i == depth - 1, "the fused final norm assumes a single layer"
        xt = _ffn2(x1, y_r, y_s, w_out, row(norm_ffn2[i]), ffn2_w_gate, ffn2_w_up, ffn2_w_down,
                   row(norm_final), i)
    return xt.reshape(batch, seq, d)
```

```python
import functools

import jax
import jax.numpy as jnp
import numpy as np
from jax import lax
from jax.experimental import pallas as pl
from jax.experimental.pallas import tpu as pltpu

F32 = jnp.float32
BF16 = jnp.bfloat16

NORM_EPS = 1e-6
RWKV_GN_EPS = 64e-5
SSM_NORM_EPS = 1e-5

HEAD_DIM = 64
CHUNK = 64
BLOCK = 256
LANE = 128
BF16_ROWS = 16
HALO = 8
MIX_ROWS = 512

NT_DIMS = (((1,), (1,)), ((), ()))
TN_DIMS = (((0,), (0,)), ((), ()))

VMEM_LIMIT = 56 * 1024 * 1024


def _dot(a, b):
    return jnp.dot(a, b, preferred_element_type=F32)


def _dot_nt(a, b):
    return lax.dot_general(a, b, NT_DIMS, preferred_element_type=F32)


def _dot_tn(a, b):
    return lax.dot_general(a, b, TN_DIMS, preferred_element_type=F32)


def _rms(x, g, eps):
    return x * lax.rsqrt(jnp.mean(x * x, axis=-1, keepdims=True) + eps) * g


def _split2_rows(x):
    hi = x.astype(BF16)
    lo = (x - hi.astype(F32)).astype(BF16)
    return jnp.concatenate([hi, lo], axis=0)


def _split2_cols(x):
    hi = x.astype(BF16)
    lo = (x - hi.astype(F32)).astype(BF16)
    return jnp.concatenate([hi, lo], axis=1)


def _head_sum(x, bdm_ref):
    xb = x.astype(BF16)
    return jnp.concatenate([_dot(xb[:, c:c + BLOCK], bdm_ref[...]) for c in range(0, x.shape[1], BLOCK)],
                           axis=1)


def _block_diag(xb, bdm):
    return jnp.concatenate([xb, xb, xb, xb], axis=0) * bdm


def _each(f, *lists):
    return [f(*args) for args in zip(*lists)]


def _resident(shape):
    zeros = (0,) * len(shape)
    return pl.BlockSpec(shape, lambda *_: zeros, pipeline_mode=pl.Buffered(1))


FF_CHUNK = 256
FFN_ROWS = 512


def _norm_split(x, g, h_scr, r_scr):
    h_scr[...] = (x * g).astype(BF16)
    r = lax.rsqrt(jnp.mean(x * x, axis=-1, keepdims=True) + NORM_EPS)
    r_scr[...] = jnp.broadcast_to(r, r_scr.shape)


def _swiglu(h_scr, r_scr, wg_ref, wu_ref, wd_ref, act_scr):
    d_ff = wg_ref.shape[1]
    for c in range(d_ff // FF_CHUNK):
        cs = slice(c * FF_CHUNK, (c + 1) * FF_CHUNK)
        r = jnp.concatenate([r_scr[...]] * (FF_CHUNK // LANE), axis=1)
        gate = _dot(h_scr[...], wg_ref[:, cs]) * r
        up = _dot(h_scr[...], wu_ref[:, cs]) * r
        act_scr[:, cs] = (gate * jax.nn.sigmoid(gate) * up).astype(BF16)
    return _dot(act_scr[...], wd_ref[...])


STAGE_ROWS = 512


def _stage_weights(w_hbms, layer, w_scrs, stage_scr, sems):
    step = stage_scr.shape[1]
    pieces = [(k, r0, min(step, w.shape[1] - r0))
              for k, w in enumerate(w_hbms) for r0 in range(0, w.shape[1], step)]

    def view(c):
        k, _, rows = pieces[c]
        return stage_scr.at[c % 2, pl.ds(0, rows), pl.ds(0, w_hbms[k].shape[2])]

    def dma(c):
        k, r0, rows = pieces[c]
        return pltpu.make_async_copy(w_hbms[k].at[layer, pl.ds(r0, rows), :], view(c), sems.at[c % 2])

    dma(0).start()
    for c, (k, r0, rows) in enumerate(pieces):
        if c + 1 < len(pieces):
            dma(c + 1).start()
        dma(c).wait()
        w_scrs[k][r0:r0 + rows, 0:w_hbms[k].shape[2]] = view(c)[...].astype(BF16)


def _ffn1_kernel(x_ref, g_ref, wg_hbm, wu_hbm, wd_hbm, o_ref,
                 h_scr, r_scr, act_scr, wg_scr, wu_scr, wd_scr, stage_scr, sems, *, layer):
    @pl.when(pl.program_id(0) == 0)
    def _():
        _stage_weights((wg_hbm, wu_hbm, wd_hbm), layer, (wg_scr, wu_scr, wd_scr), stage_scr, sems)

    _norm_split(x_ref[...], g_ref[...], h_scr, r_scr)
    o_ref[...] = x_ref[...] + 0.5 * _swiglu(h_scr, r_scr, wg_scr, wu_scr, wd_scr, act_scr)


def _ffn2_kernel(x_ref, yr_ref, ys_ref, wo_hbm, g_ref, wg_hbm, wu_hbm, wd_hbm, gf_ref, o_ref,
                 h_scr, r_scr, act_scr, x_scr, wo_scr, wg_scr, wu_scr, wd_scr, stage_scr, sems, *, layer):
    @pl.when(pl.program_id(0) == 0)
    def _():
        _stage_weights((wo_hbm, wg_hbm, wu_hbm, wd_hbm), layer, (wo_scr, wg_scr, wu_scr, wd_scr),
                       stage_scr, sems)

    dr = yr_ref.shape[1]
    x2 = x_ref[...] + _dot(yr_ref[...], wo_scr[0:dr, :]) + _dot(ys_ref[...], wo_scr[dr:, :])
    x_scr[...] = x2
    _norm_split(x2, g_ref[...], h_scr, r_scr)
    y = _swiglu(h_scr, r_scr, wg_scr, wu_scr, wd_scr, act_scr)
    o_ref[...] = _rms(x_scr[...] + 0.5 * y, gf_ref[...], NORM_EPS)


def _weight_scratch(*weights):
    widest = max(w.shape[2] for w in weights)
    return ([pltpu.VMEM(w.shape[1:], BF16) for w in weights]
            + [pltpu.VMEM((2, STAGE_ROWS, widest), F32), pltpu.SemaphoreType.DMA((2,))])


def _ffn1(x, g, wg, wu, wd, layer):
    t, d = x.shape
    d_ff = wg.shape[2]
    tm = FFN_ROWS
    assert t % tm == 0 and d_ff % FF_CHUNK == 0
    row = lambda i: (i, 0)
    hbm = pl.BlockSpec(memory_space=pl.ANY)
    return pl.pallas_call(
        functools.partial(_ffn1_kernel, layer=layer),
        grid=(t // tm,),
        in_specs=[pl.BlockSpec((tm, d), row), _resident(g.shape), hbm, hbm, hbm],
        out_specs=pl.BlockSpec((tm, d), row),
        out_shape=jax.ShapeDtypeStruct((t, d), F32),
        scratch_shapes=[pltpu.VMEM((tm, d), BF16), pltpu.VMEM((tm, LANE), F32), pltpu.VMEM((tm, d_ff), BF16)]
        + _weight_scratch(wg, wu, wd),
        compiler_params=pltpu.CompilerParams(
            dimension_semantics=("arbitrary",), vmem_limit_bytes=VMEM_LIMIT),
        name="ffn1",
    )(x, g, wg, wu, wd)


def _ffn2(x, yr, ys, wo, g, wg, wu, wd, gf, layer):
    t, d = x.shape
    d_ff = wg.shape[2]
    dm = yr.shape[1]
    tm = FFN_ROWS
    assert t % tm == 0 and d_ff % FF_CHUNK == 0 and wo.shape[1] == 2 * dm
    row = lambda i: (i, 0)
    hbm = pl.BlockSpec(memory_space=pl.ANY)
    return pl.pallas_call(
        functools.partial(_ffn2_kernel, layer=layer),
        grid=(t // tm,),
        in_specs=[pl.BlockSpec((tm, d), row), pl.BlockSpec((tm, dm), row), pl.BlockSpec((tm, dm), row),
                  hbm, _resident(g.shape), hbm, hbm, hbm, _resident(gf.shape)],
        out_specs=pl.BlockSpec((tm, d), row),
        out_shape=jax.ShapeDtypeStruct((t, d), F32),
        scratch_shapes=[pltpu.VMEM((tm, d), BF16), pltpu.VMEM((tm, LANE), F32), pltpu.VMEM((tm, d_ff), BF16),
                        pltpu.VMEM((tm, d), F32)] + _weight_scratch(wo, wg, wu, wd),
        compiler_params=pltpu.CompilerParams(
            dimension_semantics=("arbitrary",), vmem_limit_bytes=VMEM_LIMIT),
        name="ffn2",
    )(x, yr, ys, wo, g, wg, wu, wd, gf)


IN_CHUNK = 512


def _col_chunks(lo, hi):
    return [(c0, min(IN_CHUNK, hi - c0)) for c0 in range(lo, hi, IN_CHUNK)]


def _inproj_kernel(x_ref, g_ref, wt_hbm, mu_ref, cw_ref, cb_ref, p_ref, h_scr, ext_scr, raw_scr,
                   wt_scr, stage_scr, sems, *, layer, tiles_per_seq, wr, conv_lo):
    tm = x_ref.shape[0]
    dc = cw_ref.shape[1]
    width = cw_ref.shape[0]
    step = pl.program_id(0)
    h_scr[...] = _rms(x_ref[...], g_ref[...], NORM_EPS).astype(BF16)

    @pl.when(step == 0)
    def _():
        raw_scr[...] = jnp.zeros_like(raw_scr)
        pad_lo = wt_hbm.shape[1] // BF16_ROWS * BF16_ROWS
        wt_scr[pad_lo:, :] = jnp.zeros((wt_scr.shape[0] - pad_lo, wt_scr.shape[1]), BF16)
        _stage_weights((wt_hbm,), layer, (wt_scr,), stage_scr, sems)

    @pl.when(lax.rem(step + tiles_per_seq - 1, tiles_per_seq) == 0)
    def _():
        ext_scr[...] = jnp.zeros_like(ext_scr)

    def behind_halo(cols, pc):
        ext = jnp.concatenate([ext_scr[:, cols], pc], axis=0)
        ext_scr[:, cols] = pc[tm - HALO:, :]
        return ext

    def lerp(c0, n, pc):
        cs = slice(c0, c0 + n)
        shifted = pltpu.roll(behind_halo(cs, pc), 1, axis=0)[HALO:, :]
        p_ref[:, cs] = pc + (shifted - pc) * mu_ref[:, cs]

    def conv_silu(c0, n, pc):
        j = c0 - conv_lo
        cw = slice(j, j + n)
        ext = behind_halo(slice(wr + j, wr + j + n), pc)
        tap = lambda i: cw_ref[width - 1 - i:width - i, cw]
        if width == 4:
            ext1 = pltpu.roll(ext, 1, axis=0)
            far = pltpu.roll(ext * tap(2) + ext1 * tap(3), 2, axis=0)[HALO:, :]
            conv = pc * tap(0) + ext1[HALO:, :] * tap(1) + far + cb_ref[:, cw]
        else:
            conv = pc * tap(0) + cb_ref[:, cw]
            for i in range(1, width):
                conv = conv + pltpu.roll(ext, i, axis=0)[HALO:, :] * tap(i)
        p_ref[:, c0:c0 + n] = conv * jax.nn.sigmoid(conv)

    def plain(c0, n, pc):
        p_ref[:, c0:c0 + n] = pc

    convs = [(conv_silu, c) for c in _col_chunks(conv_lo, conv_lo + dc)]
    light = ([(lerp, c) for c in _col_chunks(0, wr)]
             + [(plain, c) for c in _col_chunks(wr, conv_lo) + _col_chunks(conv_lo + dc, p_ref.shape[1])])
    per_conv = len(light) // max(len(convs), 1)
    order = []
    for n, cv in enumerate(convs):
        order += [cv] + light[n * per_conv:(n + 1) * per_conv]
    order += light[len(convs) * per_conv:]
    for finish, (c0, n) in order:
        before = raw_scr[:, c0:c0 + n]
        raw_scr[:, c0:c0 + n] = _dot_nt(h_scr[...], wt_scr[c0:c0 + n, :])
        finish(c0, n, before)


IN_STAGE_ROWS = 1024


def _inproj(x, g, wt_all, layer, width_out, mu, conv_w, conv_b, seq, wr, conv_lo):
    t, d = x.shape
    tm = 512
    dc = conv_w.shape[1]
    n_cols = wt_all.shape[1]
    assert t % tm == 0 and seq % tm == 0 and conv_w.shape[0] <= HALO
    assert all(v % LANE == 0 for v in (wr, conv_lo, dc, width_out)) and wr <= conv_lo <= n_cols - dc
    assert n_cols <= width_out
    body = functools.partial(_inproj_kernel, layer=layer, tiles_per_seq=seq // tm, wr=wr, conv_lo=conv_lo)
    n = t // tm
    return pl.pallas_call(
        body,
        grid=(n + 1,),
        in_specs=[pl.BlockSpec((tm, d), lambda i: (jnp.minimum(i, n - 1), 0)), _resident(g.shape),
                  pl.BlockSpec(memory_space=pl.ANY), _resident(mu.shape), _resident(conv_w.shape),
                  _resident(conv_b.shape)],
        out_specs=pl.BlockSpec((tm, width_out), lambda i: (jnp.maximum(i - 1, 0), 0)),
        out_shape=jax.ShapeDtypeStruct((t, width_out), F32),
        scratch_shapes=[pltpu.VMEM((tm, d), BF16), pltpu.VMEM((HALO, wr + dc), F32),
                        pltpu.VMEM((tm, width_out), F32), pltpu.VMEM((width_out, d), BF16),
                        pltpu.VMEM((2, IN_STAGE_ROWS, d), F32), pltpu.SemaphoreType.DMA((2,))],
        compiler_params=pltpu.CompilerParams(
            dimension_semantics=("arbitrary",), vmem_limit_bytes=VMEM_LIMIT),
        name="in_proj",
    )(x, g, wt_all, mu, conv_w, conv_b)


def _rwkv_kernel(p_ref, w0_ref, w2_ref, a0_ref, a2_ref, g2_ref, kk_ref, ka_ref, rk_ref,
                 gng_ref, gnb_ref, tri2_ref, bdm_ref, bdm32_ref, trm_ref, eye_ref,
                 o_ref,
                 zt_scr, k_scr, a_scr, b_scr, ld_scr, y_scr, bonus_scr, gate_scr):
    n_seq, rows_seq, dr = o_ref.shape
    tl = n_seq * rows_seq

    @pl.when(pl.program_id(1) == 0)
    def _():
        zt_scr[...] = jnp.zeros_like(zt_scr)

    cols_of_p = lambda lo, hi: p_ref[:, :, lo:hi].reshape(tl, hi - lo)
    r = cols_of_p(0, dr)
    k = cols_of_p(dr, 2 * dr)
    v = cols_of_p(2 * dr, 3 * dr)
    wa = cols_of_p(3 * dr, 3 * dr + LANE)
    gd = cols_of_p(3 * dr + LANE, 3 * dr + 2 * LANE)

    wpre = w0_ref[...] + _dot(jnp.tanh(wa).astype(BF16), w2_ref[...])
    ld_scr[...] = jax.nn.sigmoid(wpre) * (-jnp.exp(-0.5))
    iclr = jax.nn.sigmoid(a0_ref[...] + _dot(wa.astype(BF16), a2_ref[...]))
    gate_scr[...] = _dot(jax.nn.sigmoid(gd).astype(BF16), g2_ref[...])

    kkv = k * kk_ref[...]
    kkn = kkv * lax.rsqrt(jnp.maximum(_head_sum(kkv * kkv, bdm_ref), 1e-24))
    kmod = k * (1.0 + (iclr - 1.0) * ka_ref[...])
    k_scr[...] = kmod
    a_scr[...] = -kkn
    b_scr[...] = kkn * iclr
    bonus_scr[...] = _head_sum(r * kmod * rk_ref[...], bdm_ref) * v

    bdmask32 = bdm32_ref[...] > 0.5
    trmask = trm_ref[...] > 0.5
    bd = lambda xb: _block_diag(xb, bdm_ref[...])
    n_q = dr // BLOCK

    probs = [(slice(s * rows_seq + c * CHUNK, s * rows_seq + (c + 1) * CHUNK),
              slice(q * BLOCK, (q + 1) * BLOCK), s * n_q + q)
             for c in range(rows_seq // CHUNK) for s in range(n_seq) for q in range(n_q)]
    tile = lambda scr: [scr[rows, cols] for rows, cols, _ in probs]
    from_p = lambda col0: [p_ref[s, c * CHUNK:(c + 1) * CHUNK, col0 + q * BLOCK:col0 + (q + 1) * BLOCK]
                           for c in range(rows_seq // CHUNK) for s in range(n_seq) for q in range(n_q)]

    ld = tile(ld_scr)
    cum = _each(lambda x: _dot(tri2_ref[...], _split2_rows(x)), ld)
    b_c = tile(b_scr)
    k_c = tile(k_scr)
    vb = _each(lambda x: x.astype(BF16), from_p(2 * dr))
    atb = _each(lambda a, cm, l: (a * jnp.exp(cm - l)).astype(BF16), tile(a_scr), cum, ld)
    rtb = _each(lambda x, cm: (x * jnp.exp(cm)).astype(BF16), from_p(0), cum)
    inv = _each(lambda cm: jnp.exp(-cm), cum)
    g_end = _each(lambda cm: jnp.exp(cm[CHUNK - 1:CHUNK, :]), cum)
    to_end = _each(lambda cm: jnp.exp(cm[CHUNK - 1:CHUNK, :] - cm), cum)
    ar = _each(lambda a, x: jnp.concatenate([a, x], axis=0), atb, rtb)
    masked = lambda x: jnp.where(trmask, x, 0.0)
    sb32 = _each(lambda a, b, i: masked(_dot_nt(a, bd((b * i).astype(BF16)))), ar, b_c, inv)
    sk = _each(lambda a, x, i: masked(_dot_nt(a, bd((x * i).astype(BF16)))).astype(BF16), ar, k_c, inv)
    kv = _each(lambda x, y: _dot(x, bd(y)), sk, vb)
    t_inv = _each(lambda x: eye_ref[...] + x[0:CHUNK], sb32)
    pw = _each(lambda x: x[0:CHUNK].astype(BF16), sb32)
    pw = _each(lambda x: _dot(x, bd(x)).astype(BF16), pw)
    for i in range(1, 5):
        res = _each(lambda t, x: _dot(jnp.concatenate([t.astype(BF16), x], axis=0), bd(x)), t_inv, pw)
        t_inv = _each(lambda t, x: t + x[0:CHUNK], t_inv, res)
        pw = _each(lambda x: x[CHUNK:].astype(BF16), res)
    t_inv = _each(lambda t, x: t + _dot(t.astype(BF16), bd(x)), t_inv, pw)
    xx = _each(lambda t, a, x: _dot(t.astype(BF16),
                                    jnp.concatenate([bd(a), bd(x[0:CHUNK].astype(BF16))], axis=1)),
               t_inv, atb, kv)
    x1b = _each(lambda x: x[:, 0:BLOCK].astype(BF16), xx)
    x2 = _each(lambda x: x[:, BLOCK:], xx)
    bgb = _each(lambda b, e: (b * e).astype(BF16), b_c, to_end)
    kgb = _each(lambda x, e: (x * e).astype(BF16), k_c, to_end)
    p_t = _each(lambda x, b: _dot_tn(x, b).astype(BF16), x1b, bgb)
    c_t = _each(lambda x, y, b, kg: _dot_tn(jnp.concatenate([x.astype(BF16), y], axis=0),
                                            jnp.concatenate([b, kg], axis=0)), x2, vb, bgb, kgb)

    zs = [zt_scr[j] for j in range(n_seq * n_q)]
    zb = []
    for (_, _, j), g, pt, ct in zip(probs, g_end, p_t, c_t):
        zb.append(zs[j].astype(BF16))
        zs[j] = zs[j] * g + jnp.where(bdmask32, _dot(zb[-1], pt) + ct, 0.0)
    for j in range(n_seq * n_q):
        zt_scr[j] = zs[j]
    xz = _each(lambda x, rt, z: _dot_nt(jnp.concatenate([x, rt], axis=0), z), x1b, rtb, zb)
    ubd = _each(lambda m, x: bd((m[0:CHUNK] + x).astype(BF16)), xz, x2)
    for (rows, cols, _), m, n_rb, u, o in zip(probs, xz, sb32, ubd, kv):
        y_scr[rows, cols] = m[CHUNK:] + _dot(n_rb[CHUNK:].astype(BF16), u) + o[CHUNK:]

    y = y_scr[...]
    inv_n = 1.0 / HEAD_DIM
    mean = _head_sum(y, bdm_ref) * inv_n
    yc = y - mean
    var = _head_sum(yc * yc, bdm_ref) * inv_n
    yn = yc * lax.rsqrt(var + RWKV_GN_EPS) * gng_ref[...] + gnb_ref[...]
    o_ref[...] = ((yn + bonus_scr[...]) * gate_scr[...]).astype(BF16).reshape(n_seq, rows_seq, dr)


def _ssd_kernel(p_ref, dtb_ref, alog_ref, dsk_ref, ng_ref, exp_ref,
                tri2_ref, ones_ref, bdm_ref, tri_ref, triu_ref,
                o_ref,
                st_scr, xs_scr, xdt_scr, a_scr, y_scr):
    n_seq, rows_seq, ds = o_ref.shape
    tl = n_seq * rows_seq
    _, d_state, _ = st_scr.shape
    n_grp = ds // BLOCK
    dc = ds + 2 * n_grp * d_state

    @pl.when(pl.program_id(1) == 0)
    def _():
        st_scr[...] = jnp.zeros_like(st_scr)

    cols_of_p = lambda lo, hi: p_ref[:, :, lo:hi].reshape(tl, hi - lo)
    xs = cols_of_p(ds, 2 * ds)
    dtr = cols_of_p(ds + dc, ds + dc + LANE) + dtb_ref[...]
    dt_c = jnp.maximum(dtr, 0.0) + jnp.log1p(jnp.exp(-jnp.abs(dtr)))
    dt = _dot(_split2_cols(dt_c), exp_ref[...])
    xs_scr[...] = xs
    xdt_scr[...] = xs * dt
    a_scr[...] = dt * (-jnp.exp(alog_ref[...]))
    bc0 = 2 * ds

    incl = tri_ref[...] > 0.5
    upper = triu_ref[...] > 0.5

    order = [(s, c, g) for c in range(rows_seq // CHUNK) for s in range(n_seq) for g in range(n_grp)]
    probs = [(slice(s * rows_seq + c * CHUNK, s * rows_seq + (c + 1) * CHUNK), g, s * n_grp + g)
             for s, c, g in order]
    gcols = lambda g: slice(g * BLOCK, (g + 1) * BLOCK)
    from_p = lambda col0: [p_ref[s, c * CHUNK:(c + 1) * CHUNK, col0 + g * d_state:col0 + (g + 1) * d_state]
                           .astype(BF16) for s, c, g in order]

    a_c = [a_scr[rows, gcols(g)] for rows, g, _ in probs]
    e1 = _each(lambda a: _dot(tri2_ref[...], _split2_rows(a)), a_c)
    e2 = _each(lambda a: _dot(ones_ref[...], _split2_rows(jnp.where(upper, a, 0.0))), a_c)
    l_w = _each(lambda x, y: jnp.exp(jnp.where(incl, x - y, -jnp.inf)), e1, e2)
    b_g = from_p(bc0)
    c_g = from_p(bc0 + n_grp * d_state)
    scores = _each(lambda c, b: _dot_nt(c, jnp.concatenate([b, b, b, b], axis=0)), c_g, b_g)
    xdt_c = [xdt_scr[rows, gcols(g)] for rows, g, _ in probs]
    y_diag = _each(lambda x, l, y: _dot((x * l).astype(BF16), _block_diag(y.astype(BF16), bdm_ref[...])),
                   scores, l_w, xdt_c)
    new = _each(lambda b, x, e: _dot_tn(b, (x * jnp.exp(e[CHUNK - 1:CHUNK, :] - e)).astype(BF16)),
                b_g, xdt_c, e1)

    sts = [st_scr[j] for j in range(n_seq * n_grp)]
    stb = []
    for (_, _, j), e, n in zip(probs, e1, new):
        stb.append(sts[j].astype(BF16))
        sts[j] = sts[j] * jnp.exp(e[CHUNK - 1:CHUNK, :]) + n
    for j in range(n_seq * n_grp):
        st_scr[j] = sts[j]
    for (rows, g, _), c, st, yd, e in zip(probs, c_g, stb, y_diag, e1):
        y_scr[rows, gcols(g)] = yd + _dot(c, st) * jnp.exp(e)

    z = cols_of_p(0, ds)
    y = (y_scr[...] + dsk_ref[...] * xs_scr[...]) * (z * jax.nn.sigmoid(z))
    o_ref[...] = _rms(y, ng_ref[...], SSM_NORM_EPS).astype(BF16).reshape(n_seq, rows_seq, ds)


MIX_SEQS = 2


def _mixers(p, batch, rwkv_params, ssd_params, consts, n_grp, d_state):
    t, wp2 = p.shape
    wp = wp2 // 2
    dr = rwkv_params["w0"].shape[1]
    ds = ssd_params["norm_g"].shape[1]
    dc = ds + 2 * n_grp * d_state
    seq = t // batch
    rows = MIX_ROWS // MIX_SEQS
    assert dr % BLOCK == 0 and wp == 3 * dr + 2 * LANE and ds == n_grp * BLOCK and ds + dc + LANE <= wp
    assert t == batch * seq and batch % MIX_SEQS == 0 and seq % rows == 0 and rows % CHUNK == 0
    r_ops = ([rwkv_params[n] for n in ["w0", "w2", "a0", "a2", "g2", "k_k", "k_a", "r_k", "gn_g", "gn_b"]]
             + [consts[n] for n in ["tri2", "bdm", "bdm32", "trm", "eye"]])
    s_ops = ([ssd_params[n] for n in ["dt_bias", "a_log", "d_skip", "norm_g", "expand"]]
             + [consts[n] for n in ["tri2", "ones", "bdm", "tri", "triu"]])
    tl = MIX_ROWS
    r_scr = ([pltpu.VMEM((MIX_SEQS * (dr // BLOCK), BLOCK, BLOCK), F32)]
             + [pltpu.VMEM((tl, dr), F32) for _ in range(7)])
    s_scr = [pltpu.VMEM((MIX_SEQS * n_grp, d_state, BLOCK), F32)] + [pltpu.VMEM((tl, ds), F32) for _ in range(4)]

    def body(pr_ref, ps_ref, *refs):
        r_in, refs = refs[:len(r_ops)], refs[len(r_ops):]
        s_in, refs = refs[:len(s_ops)], refs[len(s_ops):]
        (or_ref, os_ref), scr = refs[:2], refs[2:]
        _rwkv_kernel(pr_ref, *r_in, or_ref, *scr[:len(r_scr)])
        _ssd_kernel(ps_ref, *s_in, os_ref, *scr[len(r_scr):])

    tile = lambda col: pl.BlockSpec((MIX_SEQS, rows, wp), lambda b, l: (b, l, col))
    out = lambda w: pl.BlockSpec((MIX_SEQS, rows, w), lambda b, l: (b, l, 0))
    p3 = p.reshape(batch, seq, wp2)
    y_r, y_s = pl.pallas_call(
        body,
        grid=(batch // MIX_SEQS, seq // rows),
        in_specs=[tile(0), tile(1)] + [_resident(o.shape) for o in r_ops + s_ops],
        out_specs=[out(dr), out(ds)],
        out_shape=[jax.ShapeDtypeStruct((batch, seq, dr), BF16), jax.ShapeDtypeStruct((batch, seq, ds), BF16)],
        scratch_shapes=r_scr + s_scr,
        compiler_params=pltpu.CompilerParams(
            dimension_semantics=("arbitrary", "arbitrary"), vmem_limit_bytes=VMEM_LIMIT),
        name="mixers",
    )(p3, p3, *r_ops, *s_ops)
    return y_r.reshape(t, dr), y_s.reshape(t, ds)


def _mask_consts():
    row = np.arange(BLOCK)[:, None]
    col = np.arange(BLOCK)[None, :]
    t = np.arange(CHUNK)[:, None]
    s = col % CHUNK
    tri64 = (np.arange(CHUNK)[None, :] <= t)
    bdm = row // CHUNK == col // CHUNK
    return {
        "bdm": bdm.astype(BF16),
        "bdm32": bdm.astype(F32),
        "trm": np.concatenate([s < t, s <= t], axis=0).astype(F32),
        "tri": (s <= t).astype(F32),
        "triu": (s >= t).astype(F32),
        "eye": (s == t).astype(F32),
        "tri2": np.concatenate([tri64, tri64], axis=1).astype(BF16),
        "ones": np.ones((CHUNK, 2 * CHUNK), BF16),
    }


def kernel(x, norm_ffn1, ffn1_w_gate, ffn1_w_up, ffn1_w_down, norm_mix, w_in, rwkv_mu, rwkv_w0, rwkv_w2,
           rwkv_a0, rwkv_a2, rwkv_g2, rwkv_k_k, rwkv_k_a, rwkv_r_k, rwkv_gn_g, rwkv_gn_b, ssm_conv_w,
           ssm_conv_b, ssm_dt_bias, ssm_a_log, ssm_d, ssm_norm, w_out, norm_ffn2, ffn2_w_gate, ffn2_w_up,
           ffn2_w_down, norm_final):
    batch, seq, d = x.shape
    depth = norm_ffn1.shape[0]
    dr = rwkv_w0.shape[1]
    ds = ssm_norm.shape[1]
    dc = ssm_conv_w.shape[2]
    n_heads_s = ssm_a_log.shape[1]
    lora_w = rwkv_w2.shape[1]
    lora_a = rwkv_a2.shape[1]
    lora_g = rwkv_g2.shape[1]
    n_grp = ds // BLOCK
    d_state = (dc - ds) // (2 * n_grp)
    wp = 3 * dr + 2 * LANE
    assert ds // n_heads_s == HEAD_DIM and lora_w + lora_a == LANE and lora_g == LANE
    assert n_heads_s <= LANE and ds + dc + LANE <= wp and seq % MIX_ROWS == 0

    consts = _mask_consts()
    lane_head = np.arange(LANE)[:, None] == (np.arange(ds) // HEAD_DIM)[None, :]
    expand = np.concatenate([lane_head, lane_head], axis=0).astype(BF16)
    row = lambda a: a.reshape(1, -1)
    rep = lambda a: jnp.repeat(a, HEAD_DIM).reshape(1, -1)

    xt = x.reshape(batch * seq, d)
    for i in range(depth):
        x1 = _ffn1(xt, row(norm_ffn1[i]), ffn1_w_gate, ffn1_w_up, ffn1_w_down, i)

        p = _inproj(x1, row(norm_mix[i]), jnp.swapaxes(w_in, 1, 2), i, 2 * wp, row(rwkv_mu[i]), ssm_conv_w[i],
                    row(ssm_conv_b[i]), seq, wp, wp + ds)

        rwkv_params = {
            "w0": row(rwkv_w0[i]), "a0": row(rwkv_a0[i]),
            "w2": jnp.pad(rwkv_w2[i], ((0, lora_a), (0, 0))).astype(BF16),
            "a2": jnp.pad(rwkv_a2[i], ((lora_w, 0), (0, 0))).astype(BF16),
            "g2": rwkv_g2[i].astype(BF16),
            "k_k": row(rwkv_k_k[i]), "k_a": row(rwkv_k_a[i]), "r_k": row(rwkv_r_k[i]),
            "gn_g": row(rwkv_gn_g[i]), "gn_b": row(rwkv_gn_b[i]),
        }
        ssd_params = {
            "dt_bias": jnp.pad(ssm_dt_bias[i], (0, LANE - n_heads_s)).reshape(1, -1),
            "a_log": rep(ssm_a_log[i]), "d_skip": rep(ssm_d[i]), "norm_g": row(ssm_norm[i]),
            "expand": expand,
        }
        y_r, y_s = _mixers(p, batch, rwkv_params, ssd_params, consts, n_grp, d_state)

        assert i == depth - 1, "the fused final norm assumes a single layer"
        xt = _ffn2(x1, y_r, y_s, w_out, row(norm_ffn2[i]), ffn2_w_gate, ffn2_w_up, ffn2_w_down,
                   row(norm_final), i)
    return xt.reshape(batch, seq, d)
```

```python
import functools

import jax
import jax.numpy as jnp
import numpy as np
from jax import lax
from jax.experimental import pallas as pl
from jax.experimental.pallas import tpu as pltpu

F32 = jnp.float32
BF16 = jnp.bfloat16

NORM_EPS = 1e-6
RWKV_GN_EPS = 64e-5
SSM_NORM_EPS = 1e-5

HEAD_DIM = 64
CHUNK = 64
BLOCK = 256
LANE = 128
BF16_ROWS = 16
HALO = 8
MIX_ROWS = 512

NT_DIMS = (((1,), (1,)), ((), ()))
TN_DIMS = (((0,), (0,)), ((), ()))

VMEM_LIMIT = 56 * 1024 * 1024


def _dot(a, b):
    return jnp.dot(a, b, preferred_element_type=F32)


def _dot_nt(a, b):
    return lax.dot_general(a, b, NT_DIMS, preferred_element_type=F32)


def _dot_tn(a, b):
    return lax.dot_general(a, b, TN_DIMS, preferred_element_type=F32)


def _rms(x, g, eps):
    return x * lax.rsqrt(jnp.mean(x * x, axis=-1, keepdims=True) + eps) * g


def _split2_rows(x):
    hi = x.astype(BF16)
    lo = (x - hi.astype(F32)).astype(BF16)
    return jnp.concatenate([hi, lo], axis=0)


def _split2_cols(x):
    hi = x.astype(BF16)
    lo = (x - hi.astype(F32)).astype(BF16)
    return jnp.concatenate([hi, lo], axis=1)


def _head_sum(x, bdm_ref):
    xb = x.astype(BF16)
    return jnp.concatenate([_dot(xb[:, c:c + BLOCK], bdm_ref[...]) for c in range(0, x.shape[1], BLOCK)],
                           axis=1)


def _block_diag(xb, bdm):
    return jnp.concatenate([xb, xb, xb, xb], axis=0) * bdm


def _each(f, *lists):
    return [f(*args) for args in zip(*lists)]


def _resident(shape):
    zeros = (0,) * len(shape)
    return pl.BlockSpec(shape, lambda *_: zeros, pipeline_mode=pl.Buffered(1))


FF_CHUNK = 256
FFN_ROWS = 512


def _norm_split(x, g, h_scr, r_scr):
    h_scr[...] = (x * g).astype(BF16)
    r = lax.rsqrt(jnp.mean(x * x, axis=-1, keepdims=True) + NORM_EPS)
    r_scr[...] = jnp.broadcast_to(r, r_scr.shape)


def _swiglu(h_scr, r_scr, wg_ref, wu_ref, wd_ref, act_scr):
    d_ff = wg_ref.shape[1]
    for c in range(d_ff // FF_CHUNK):
        cs = slice(c * FF_CHUNK, (c + 1) * FF_CHUNK)
        r = jnp.concatenate([r_scr[...]] * (FF_CHUNK // LANE), axis=1)
        gate = _dot(h_scr[...], wg_ref[:, cs]) * r
        up = _dot(h_scr[...], wu_ref[:, cs]) * r
        act_scr[:, cs] = (gate * jax.nn.sigmoid(gate) * up).astype(BF16)
    return _dot(act_scr[...], wd_ref[...])


STAGE_ROWS = 512


def _stage_weights(w_hbms, layer, w_scrs, stage_scr, sems):
    step = stage_scr.shape[1]
    pieces = [(k, r0, min(step, w.shape[1] - r0))
              for k, w in enumerate(w_hbms) for r0 in range(0, w.shape[1], step)]

    def view(c):
        k, _, rows = pieces[c]
        return stage_scr.at[c % 2, pl.ds(0, rows), pl.ds(0, w_hbms[k].shape[2])]

    def dma(c):
        k, r0, rows = pieces[c]
        return pltpu.make_async_copy(w_hbms[k].at[layer, pl.ds(r0, rows), :], view(c), sems.at[c % 2])

    dma(0).start()
    for c, (k, r0, rows) in enumerate(pieces):
        if c + 1 < len(pieces):
            dma(c + 1).start()
        dma(c).wait()
        w_scrs[k][r0:r0 + rows, 0:w_hbms[k].shape[2]] = view(c)[...].astype(BF16)


def _ffn1_kernel(x_ref, g_ref, wg_hbm, wu_hbm, wd_hbm, o_ref,
                 h_scr, r_scr, act_scr, wg_scr, wu_scr, wd_scr, stage_scr, sems, *, layer):
    @pl.when(pl.program_id(0) == 0)
    def _():
        _stage_weights((wg_hbm, wu_hbm, wd_hbm), layer, (wg_scr, wu_scr, wd_scr), stage_scr, sems)

    _norm_split(x_ref[...], g_ref[...], h_scr, r_scr)
    o_ref[...] = x_ref[...] + 0.5 * _swiglu(h_scr, r_scr, wg_scr, wu_scr, wd_scr, act_scr)


def _ffn2_kernel(x_ref, yr_ref, ys_ref, wo_hbm, g_ref, wg_hbm, wu_hbm, wd_hbm, gf_ref, o_ref,
                 h_scr, r_scr, act_scr, x_scr, wo_scr, wg_scr, wu_scr, wd_scr, stage_scr, sems, *, layer):
    @pl.when(pl.program_id(0) == 0)
    def _():
        _stage_weights((wo_hbm, wg_hbm, wu_hbm, wd_hbm), layer, (wo_scr, wg_scr, wu_scr, wd_scr),
                       stage_scr, sems)

    dr = yr_ref.shape[1]
    x2 = x_ref[...] + _dot(yr_ref[...], wo_scr[0:dr, :]) + _dot(ys_ref[...], wo_scr[dr:, :])
    x_scr[...] = x2
    _norm_split(x2, g_ref[...], h_scr, r_scr)
    y = _swiglu(h_scr, r_scr, wg_scr, wu_scr, wd_scr, act_scr)
    o_ref[...] = _rms(x_scr[...] + 0.5 * y, gf_ref[...], NORM_EPS)


def _weight_scratch(*weights):
    widest = max(w.shape[2] for w in weights)
    return ([pltpu.VMEM(w.shape[1:], BF16) for w in weights]
            + [pltpu.VMEM((2, STAGE_ROWS, widest), F32), pltpu.SemaphoreType.DMA((2,))])


def _ffn1(x, g, wg, wu, wd, layer):
    t, d = x.shape
    d_ff = wg.shape[2]
    tm = FFN_ROWS
    assert t % tm == 0 and d_ff % FF_CHUNK == 0
    row = lambda i: (i, 0)
    hbm = pl.BlockSpec(memory_space=pl.ANY)
    return pl.pallas_call(
        functools.partial(_ffn1_kernel, layer=layer),
        grid=(t // tm,),
        in_specs=[pl.BlockSpec((tm, d), row), _resident(g.shape), hbm, hbm, hbm],
        out_specs=pl.BlockSpec((tm, d), row),
        out_shape=jax.ShapeDtypeStruct((t, d), F32),
        scratch_shapes=[pltpu.VMEM((tm, d), BF16), pltpu.VMEM((tm, LANE), F32), pltpu.VMEM((tm, d_ff), BF16)]
        + _weight_scratch(wg, wu, wd),
        compiler_params=pltpu.CompilerParams(
            dimension_semantics=("arbitrary",), vmem_limit_bytes=VMEM_LIMIT),
        name="ffn1",
    )(x, g, wg, wu, wd)


def _ffn2(x, yr, ys, wo, g, wg, wu, wd, gf, layer):
    t, d = x.shape
    d_ff = wg.shape[2]
    dm = yr.shape[1]
    tm = FFN_ROWS
    assert t % tm == 0 and d_ff % FF_CHUNK == 0 and wo.shape[1] == 2 * dm
    row = lambda i: (i, 0)
    hbm = pl.BlockSpec(memory_space=pl.ANY)
    return pl.pallas_call(
        functools.partial(_ffn2_kernel, layer=layer),
        grid=(t // tm,),
        in_specs=[pl.BlockSpec((tm, d), row), pl.BlockSpec((tm, dm), row), pl.BlockSpec((tm, dm), row),
                  hbm, _resident(g.shape), hbm, hbm, hbm, _resident(gf.shape)],
        out_specs=pl.BlockSpec((tm, d), row),
        out_shape=jax.ShapeDtypeStruct((t, d), F32),
        scratch_shapes=[pltpu.VMEM((tm, d), BF16), pltpu.VMEM((tm, LANE), F32), pltpu.VMEM((tm, d_ff), BF16),
                        pltpu.VMEM((tm, d), F32)] + _weight_scratch(wo, wg, wu, wd),
        compiler_params=pltpu.CompilerParams(
            dimension_semantics=("arbitrary",), vmem_limit_bytes=VMEM_LIMIT),
        name="ffn2",
    )(x, yr, ys, wo, g, wg, wu, wd, gf)


IN_CHUNK = 512


def _col_chunks(lo, hi):
    return [(c0, min(IN_CHUNK, hi - c0)) for c0 in range(lo, hi, IN_CHUNK)]


def _inproj_kernel(x_ref, g_ref, wt_hbm, mu_ref, cw_ref, cb_ref, p_ref, h_scr, ext_scr, raw_scr,
                   wt_scr, stage_scr, sems, *, layer, tiles_per_seq, wr, conv_lo):
    tm = x_ref.shape[0]
    dc = cw_ref.shape[1]
    width = cw_ref.shape[0]
    step = pl.program_id(0)
    last = pl.num_programs(0) - 1

    @pl.when(step == 0)
    def _():
        pad_lo = wt_hbm.shape[1] // BF16_ROWS * BF16_ROWS
        wt_scr[pad_lo:, :] = jnp.zeros((wt_scr.shape[0] - pad_lo, wt_scr.shape[1]), BF16)
        _stage_weights((wt_hbm,), layer, (wt_scr,), stage_scr, sems)

    @pl.when(lax.rem(step + tiles_per_seq - 1, tiles_per_seq) == 0)
    def _():
        ext_scr[...] = jnp.zeros_like(ext_scr)

    def behind_halo(cols, pc):
        ext = jnp.concatenate([ext_scr[:, cols], pc], axis=0)
        ext_scr[:, cols] = pc[tm - HALO:, :]
        return ext

    def lerp(c0, n, pc):
        cs = slice(c0, c0 + n)
        shifted = pltpu.roll(behind_halo(cs, pc), 1, axis=0)[HALO:, :]
        p_ref[:, cs] = pc + (shifted - pc) * mu_ref[:, cs]

    def conv_silu(c0, n, pc):
        j = c0 - conv_lo
        cw = slice(j, j + n)
        ext = behind_halo(slice(wr + j, wr + j + n), pc)
        tap = lambda i: cw_ref[width - 1 - i:width - i, cw]
        if width == 4:
            ext1 = pltpu.roll(ext, 1, axis=0)
            far = pltpu.roll(ext * tap(2) + ext1 * tap(3), 2, axis=0)[HALO:, :]
            conv = pc * tap(0) + ext1[HALO:, :] * tap(1) + far + cb_ref[:, cw]
        else:
            conv = pc * tap(0) + cb_ref[:, cw]
            for i in range(1, width):
                conv = conv + pltpu.roll(ext, i, axis=0)[HALO:, :] * tap(i)
        p_ref[:, c0:c0 + n] = conv * jax.nn.sigmoid(conv)

    def plain(c0, n, pc):
        p_ref[:, c0:c0 + n] = pc

    convs = [(conv_silu, c) for c in _col_chunks(conv_lo, conv_lo + dc)]
    light = ([(lerp, c) for c in _col_chunks(0, wr)]
             + [(plain, c) for c in _col_chunks(wr, conv_lo) + _col_chunks(conv_lo + dc, p_ref.shape[1])])
    per_conv = len(light) // max(len(convs), 1)
    order = []
    for n, cv in enumerate(convs):
        order += [cv] + light[n * per_conv:(n + 1) * per_conv]
    order += light[len(convs) * per_conv:]

    def emit(project, finish_previous):
        if project:
            h_scr[...] = _rms(x_ref[...], g_ref[...], NORM_EPS).astype(BF16)
        for finish, (c0, n) in order:
            if finish_previous:
                before = raw_scr[:, c0:c0 + n]
            if project:
                raw_scr[:, c0:c0 + n] = _dot_nt(h_scr[...], wt_scr[c0:c0 + n, :])
            if finish_previous:
                finish(c0, n, before)

    pl.when(step == 0)(lambda: emit(True, False))
    pl.when(jnp.logical_and(step > 0, step < last))(lambda: emit(True, True))
    pl.when(step == last)(lambda: emit(False, True))


IN_STAGE_ROWS = 1024


def _inproj(x, g, wt_all, layer, width_out, mu, conv_w, conv_b, seq, wr, conv_lo):
    t, d = x.shape
    tm = 512
    dc = conv_w.shape[1]
    n_cols = wt_all.shape[1]
    assert t % tm == 0 and seq % tm == 0 and conv_w.shape[0] <= HALO
    assert all(v % LANE == 0 for v in (wr, conv_lo, dc, width_out)) and wr <= conv_lo <= n_cols - dc
    assert n_cols <= width_out
    body = functools.partial(_inproj_kernel, layer=layer, tiles_per_seq=seq // tm, wr=wr, conv_lo=conv_lo)
    n = t // tm
    return pl.pallas_call(
        body,
        grid=(n + 1,),
        in_specs=[pl.BlockSpec((tm, d), lambda i: (jnp.minimum(i, n - 1), 0)), _resident(g.shape),
                  pl.BlockSpec(memory_space=pl.ANY), _resident(mu.shape), _resident(conv_w.shape),
                  _resident(conv_b.shape)],
        out_specs=pl.BlockSpec((tm, width_out), lambda i: (jnp.maximum(i - 1, 0), 0)),
        out_shape=jax.ShapeDtypeStruct((t, width_out), F32),
        scratch_shapes=[pltpu.VMEM((tm, d), BF16), pltpu.VMEM((HALO, wr + dc), F32),
                        pltpu.VMEM((tm, width_out), F32), pltpu.VMEM((width_out, d), BF16),
                        pltpu.VMEM((2, IN_STAGE_ROWS, d), F32), pltpu.SemaphoreType.DMA((2,))],
        compiler_params=pltpu.CompilerParams(
            dimension_semantics=("arbitrary",), vmem_limit_bytes=VMEM_LIMIT),
        name="in_proj",
    )(x, g, wt_all, mu, conv_w, conv_b)


def _rwkv_kernel(p_ref, w0_ref, w2_ref, a0_ref, a2_ref, g2_ref, kk_ref, ka_ref, rk_ref,
                 gng_ref, gnb_ref, tri2_ref, bdm_ref, bdm32_ref, trm_ref, eye_ref,
                 o_ref,
                 zt_scr, k_scr, a_scr, b_scr, ld_scr, y_scr, bonus_scr, gate_scr):
    n_seq, rows_seq, dr = o_ref.shape
    tl = n_seq * rows_seq

    @pl.when(pl.program_id(1) == 0)
    def _():
        zt_scr[...] = jnp.zeros_like(zt_scr)

    cols_of_p = lambda lo, hi: p_ref[:, :, lo:hi].reshape(tl, hi - lo)
    r = cols_of_p(0, dr)
    k = cols_of_p(dr, 2 * dr)
    v = cols_of_p(2 * dr, 3 * dr)
    wa = cols_of_p(3 * dr, 3 * dr + LANE)
    gd = cols_of_p(3 * dr + LANE, 3 * dr + 2 * LANE)

    wpre = w0_ref[...] + _dot(jnp.tanh(wa).astype(BF16), w2_ref[...])
    ld_scr[...] = jax.nn.sigmoid(wpre) * (-jnp.exp(-0.5))
    iclr = jax.nn.sigmoid(a0_ref[...] + _dot(wa.astype(BF16), a2_ref[...]))
    gate_scr[...] = _dot(jax.nn.sigmoid(gd).astype(BF16), g2_ref[...])

    kkv = k * kk_ref[...]
    kkn = kkv * lax.rsqrt(jnp.maximum(_head_sum(kkv * kkv, bdm_ref), 1e-24))
    kmod = k * (1.0 + (iclr - 1.0) * ka_ref[...])
    k_scr[...] = kmod
    a_scr[...] = -kkn
    b_scr[...] = kkn * iclr
    bonus_scr[...] = _head_sum(r * kmod * rk_ref[...], bdm_ref) * v

    bdmask32 = bdm32_ref[...] > 0.5
    trmask = trm_ref[...] > 0.5
    bd = lambda xb: _block_diag(xb, bdm_ref[...])
    n_q = dr // BLOCK

    probs = [(slice(s * rows_seq + c * CHUNK, s * rows_seq + (c + 1) * CHUNK),
              slice(q * BLOCK, (q + 1) * BLOCK), s * n_q + q)
             for c in range(rows_seq // CHUNK) for s in range(n_seq) for q in range(n_q)]
    tile = lambda scr: [scr[rows, cols] for rows, cols, _ in probs]
    from_p = lambda col0: [p_ref[s, c * CHUNK:(c + 1) * CHUNK, col0 + q * BLOCK:col0 + (q + 1) * BLOCK]
                           for c in range(rows_seq // CHUNK) for s in range(n_seq) for q in range(n_q)]

    ld = tile(ld_scr)
    cum = _each(lambda x: _dot(tri2_ref[...], _split2_rows(x)), ld)
    b_c = tile(b_scr)
    k_c = tile(k_scr)
    vb = _each(lambda x: x.astype(BF16), from_p(2 * dr))
    atb = _each(lambda a, cm, l: (a * jnp.exp(cm - l)).astype(BF16), tile(a_scr), cum, ld)
    rtb = _each(lambda x, cm: (x * jnp.exp(cm)).astype(BF16), from_p(0), cum)
    inv = _each(lambda cm: jnp.exp(-cm), cum)
    g_end = _each(lambda cm: jnp.exp(cm[CHUNK - 1:CHUNK, :]), cum)
    to_end = _each(lambda cm: jnp.exp(cm[CHUNK - 1:CHUNK, :] - cm), cum)
    ar = _each(lambda a, x: jnp.concatenate([a, x], axis=0), atb, rtb)
    masked = lambda x: jnp.where(trmask, x, 0.0)
    sb32 = _each(lambda a, b, i: masked(_dot_nt(a, bd((b * i).astype(BF16)))), ar, b_c, inv)
    sk = _each(lambda a, x, i: masked(_dot_nt(a, bd((x * i).astype(BF16)))).astype(BF16), ar, k_c, inv)
    kv = _each(lambda x, y: _dot(x, bd(y)), sk, vb)
    t_inv = _each(lambda x: eye_ref[...] + x[0:CHUNK], sb32)
    pw = _each(lambda x: x[0:CHUNK].astype(BF16), sb32)
    pw = _each(lambda x: _dot(x, bd(x)).astype(BF16), pw)
    for i in range(1, 5):
        res = _each(lambda t, x: _dot(jnp.concatenate([t.astype(BF16), x], axis=0), bd(x)), t_inv, pw)
        t_inv = _each(lambda t, x: t + x[0:CHUNK], t_inv, res)
        pw = _each(lambda x: x[CHUNK:].astype(BF16), res)
    t_inv = _each(lambda t, x: t + _dot(t.astype(BF16), bd(x)), t_inv, pw)
    xx = _each(lambda t, a, x: _dot(t.astype(BF16),
                                    jnp.concatenate([bd(a), bd(x[0:CHUNK].astype(BF16))], axis=1)),
               t_inv, atb, kv)
    x1b = _each(lambda x: x[:, 0:BLOCK].astype(BF16), xx)
    x2 = _each(lambda x: x[:, BLOCK:], xx)
    bgb = _each(lambda b, e: (b * e).astype(BF16), b_c, to_end)
    kgb = _each(lambda x, e: (x * e).astype(BF16), k_c, to_end)
    p_t = _each(lambda x, b: _dot_tn(x, b).astype(BF16), x1b, bgb)
    c_t = _each(lambda x, y, b, kg: _dot_tn(jnp.concatenate([x.astype(BF16), y], axis=0),
                                            jnp.concatenate([b, kg], axis=0)), x2, vb, bgb, kgb)

    zs = [zt_scr[j] for j in range(n_seq * n_q)]
    zb = []
    for (_, _, j), g, pt, ct in zip(probs, g_end, p_t, c_t):
        zb.append(zs[j].astype(BF16))
        zs[j] = zs[j] * g + jnp.where(bdmask32, _dot(zb[-1], pt) + ct, 0.0)
    for j in range(n_seq * n_q):
        zt_scr[j] = zs[j]
    xz = _each(lambda x, rt, z: _dot_nt(jnp.concatenate([x, rt], axis=0), z), x1b, rtb, zb)
    ubd = _each(lambda m, x: bd((m[0:CHUNK] + x).astype(BF16)), xz, x2)
    for (rows, cols, _), m, n_rb, u, o in zip(probs, xz, sb32, ubd, kv):
        y_scr[rows, cols] = m[CHUNK:] + _dot(n_rb[CHUNK:].astype(BF16), u) + o[CHUNK:]

    y = y_scr[...]
    inv_n = 1.0 / HEAD_DIM
    mean = _head_sum(y, bdm_ref) * inv_n
    yc = y - mean
    var = _head_sum(yc * yc, bdm_ref) * inv_n
    yn = yc * lax.rsqrt(var + RWKV_GN_EPS) * gng_ref[...] + gnb_ref[...]
    o_ref[...] = ((yn + bonus_scr[...]) * gate_scr[...]).astype(BF16).reshape(n_seq, rows_seq, dr)


def _ssd_kernel(p_ref, dtb_ref, alog_ref, dsk_ref, ng_ref, exp_ref,
                tri2_ref, ones_ref, bdm_ref, tri_ref, triu_ref,
                o_ref,
                st_scr, xs_scr, xdt_scr, a_scr, y_scr):
    n_seq, rows_seq, ds = o_ref.shape
    tl = n_seq * rows_seq
    _, d_state, _ = st_scr.shape
    n_grp = ds // BLOCK
    dc = ds + 2 * n_grp * d_state

    @pl.when(pl.program_id(1) == 0)
    def _():
        st_scr[...] = jnp.zeros_like(st_scr)

    cols_of_p = lambda lo, hi: p_ref[:, :, lo:hi].reshape(tl, hi - lo)
    xs = cols_of_p(ds, 2 * ds)
    dtr = cols_of_p(ds + dc, ds + dc + LANE) + dtb_ref[...]
    dt_c = jnp.maximum(dtr, 0.0) + jnp.log1p(jnp.exp(-jnp.abs(dtr)))
    dt = _dot(_split2_cols(dt_c), exp_ref[...])
    xs_scr[...] = xs
    xdt_scr[...] = xs * dt
    a_scr[...] = dt * (-jnp.exp(alog_ref[...]))
    bc0 = 2 * ds

    incl = tri_ref[...] > 0.5
    upper = triu_ref[...] > 0.5

    order = [(s, c, g) for c in range(rows_seq // CHUNK) for s in range(n_seq) for g in range(n_grp)]
    probs = [(slice(s * rows_seq + c * CHUNK, s * rows_seq + (c + 1) * CHUNK), g, s * n_grp + g)
             for s, c, g in order]
    gcols = lambda g: slice(g * BLOCK, (g + 1) * BLOCK)
    from_p = lambda col0: [p_ref[s, c * CHUNK:(c + 1) * CHUNK, col0 + g * d_state:col0 + (g + 1) * d_state]
                           .astype(BF16) for s, c, g in order]

    a_c = [a_scr[rows, gcols(g)] for rows, g, _ in probs]
    e1 = _each(lambda a: _dot(tri2_ref[...], _split2_rows(a)), a_c)
    e2 = _each(lambda a: _dot(ones_ref[...], _split2_rows(jnp.where(upper, a, 0.0))), a_c)
    l_w = _each(lambda x, y: jnp.exp(jnp.where(incl, x - y, -jnp.inf)), e1, e2)
    b_g = from_p(bc0)
    c_g = from_p(bc0 + n_grp * d_state)
    scores = _each(lambda c, b: _dot_nt(c, jnp.concatenate([b, b, b, b], axis=0)), c_g, b_g)
    xdt_c = [xdt_scr[rows, gcols(g)] for rows, g, _ in probs]
    y_diag = _each(lambda x, l, y: _dot((x * l).astype(BF16), _block_diag(y.astype(BF16), bdm_ref[...])),
                   scores, l_w, xdt_c)
    new = _each(lambda b, x, e: _dot_tn(b, (x * jnp.exp(e[CHUNK - 1:CHUNK, :] - e)).astype(BF16)),
                b_g, xdt_c, e1)

    sts = [st_scr[j] for j in range(n_seq * n_grp)]
    stb = []
    for (_, _, j), e, n in zip(probs, e1, new):
        stb.append(sts[j].astype(BF16))
        sts[j] = sts[j] * jnp.exp(e[CHUNK - 1:CHUNK, :]) + n
    for j in range(n_seq * n_grp):
        st_scr[j] = sts[j]
    for (rows, g, _), c, st, yd, e in zip(probs, c_g, stb, y_diag, e1):
        y_scr[rows, gcols(g)] = yd + _dot(c, st) * jnp.exp(e)

    z = cols_of_p(0, ds)
    y = (y_scr[...] + dsk_ref[...] * xs_scr[...]) * (z * jax.nn.sigmoid(z))
    o_ref[...] = _rms(y, ng_ref[...], SSM_NORM_EPS).astype(BF16).reshape(n_seq, rows_seq, ds)


MIX_SEQS = 2


def _mixers(p, batch, rwkv_params, ssd_params, consts, n_grp, d_state):
    t, wp2 = p.shape
    wp = wp2 // 2
    dr = rwkv_params["w0"].shape[1]
    ds = ssd_params["norm_g"].shape[1]
    dc = ds + 2 * n_grp * d_state
    seq = t // batch
    rows = MIX_ROWS // MIX_SEQS
    assert dr % BLOCK == 0 and wp == 3 * dr + 2 * LANE and ds == n_grp * BLOCK and ds + dc + LANE <= wp
    assert t == batch * seq and batch % MIX_SEQS == 0 and seq % rows == 0 and rows % CHUNK == 0
    r_ops = ([rwkv_params[n] for n in ["w0", "w2", "a0", "a2", "g2", "k_k", "k_a", "r_k", "gn_g", "gn_b"]]
             + [consts[n] for n in ["tri2", "bdm", "bdm32", "trm", "eye"]])
    s_ops = ([ssd_params[n] for n in ["dt_bias", "a_log", "d_skip", "norm_g", "expand"]]
             + [consts[n] for n in ["tri2", "ones", "bdm", "tri", "triu"]])
    tl = MIX_ROWS
    r_scr = ([pltpu.VMEM((MIX_SEQS * (dr // BLOCK), BLOCK, BLOCK), F32)]
             + [pltpu.VMEM((tl, dr), F32) for _ in range(7)])
    s_scr = [pltpu.VMEM((MIX_SEQS * n_grp, d_state, BLOCK), F32)] + [pltpu.VMEM((tl, ds), F32) for _ in range(4)]

    def body(pr_ref, ps_ref, *refs):
        r_in, refs = refs[:len(r_ops)], refs[len(r_ops):]
        s_in, refs = refs[:len(s_ops)], refs[len(s_ops):]
        (or_ref, os_ref), scr = refs[:2], refs[2:]
        _rwkv_kernel(pr_ref, *r_in, or_ref, *scr[:len(r_scr)])
        _ssd_kernel(ps_ref, *s_in, os_ref, *scr[len(r_scr):])

    tile = lambda col: pl.BlockSpec((MIX_SEQS, rows, wp), lambda b, l: (b, l, col))
    out = lambda w: pl.BlockSpec((MIX_SEQS, rows, w), lambda b, l: (b, l, 0))
    p3 = p.reshape(batch, seq, wp2)
    y_r, y_s = pl.pallas_call(
        body,
        grid=(batch // MIX_SEQS, seq // rows),
        in_specs=[tile(0), tile(1)] + [_resident(o.shape) for o in r_ops + s_ops],
        out_specs=[out(dr), out(ds)],
        out_shape=[jax.ShapeDtypeStruct((batch, seq, dr), BF16), jax.ShapeDtypeStruct((batch, seq, ds), BF16)],
        scratch_shapes=r_scr + s_scr,
        compiler_params=pltpu.CompilerParams(
            dimension_semantics=("arbitrary", "arbitrary"), vmem_limit_bytes=VMEM_LIMIT),
        name="mixers",
    )(p3, p3, *r_ops, *s_ops)
    return y_r.reshape(t, dr), y_s.reshape(t, ds)


def _mask_consts():
    row = np.arange(BLOCK)[:, None]
    col = np.arange(BLOCK)[None, :]
    t = np.arange(CHUNK)[:, None]
    s = col % CHUNK
    tri64 = (np.arange(CHUNK)[None, :] <= t)
    bdm = row // CHUNK == col // CHUNK
    return {
        "bdm": bdm.astype(BF16),
        "bdm32": bdm.astype(F32),
        "trm": np.concatenate([s < t, s <= t], axis=0).astype(F32),
        "tri": (s <= t).astype(F32),
        "triu": (s >= t).astype(F32),
        "eye": (s == t).astype(F32),
        "tri2": np.concatenate([tri64, tri64], axis=1).astype(BF16),
        "ones": np.ones((CHUNK, 2 * CHUNK), BF16),
    }


def kernel(x, norm_ffn1, ffn1_w_gate, ffn1_w_up, ffn1_w_down, norm_mix, w_in, rwkv_mu, rwkv_w0, rwkv_w2,
           rwkv_a0, rwkv_a2, rwkv_g2, rwkv_k_k, rwkv_k_a, rwkv_r_k, rwkv_gn_g, rwkv_gn_b, ssm_conv_w,
           ssm_conv_b, ssm_dt_bias, ssm_a_log, ssm_d, ssm_norm, w_out, norm_ffn2, ffn2_w_gate, ffn2_w_up,
           ffn2_w_down, norm_final):
    batch, seq, d = x.shape
    depth = norm_ffn1.shape[0]
    dr = rwkv_w0.shape[1]
    ds = ssm_norm.shape[1]
    dc = ssm_conv_w.shape[2]
    n_heads_s = ssm_a_log.shape[1]
    lora_w = rwkv_w2.shape[1]
    lora_a = rwkv_a2.shape[1]
    lora_g = rwkv_g2.shape[1]
    n_grp = ds // BLOCK
    d_state = (dc - ds) // (2 * n_grp)
    wp = 3 * dr + 2 * LANE
    assert ds // n_heads_s == HEAD_DIM and lora_w + lora_a == LANE and lora_g == LANE
    assert n_heads_s <= LANE and ds + dc + LANE <= wp and seq % MIX_ROWS == 0

    consts = _mask_consts()
    lane_head = np.arange(LANE)[:, None] == (np.arange(ds) // HEAD_DIM)[None, :]
    expand = np.concatenate([lane_head, lane_head], axis=0).astype(BF16)
    row = lambda a: a.reshape(1, -1)
    rep = lambda a: jnp.repeat(a, HEAD_DIM).reshape(1, -1)

    xt = x.reshape(batch * seq, d)
    for i in range(depth):
        x1 = _ffn1(xt, row(norm_ffn1[i]), ffn1_w_gate, ffn1_w_up, ffn1_w_down, i)

        p = _inproj(x1, row(norm_mix[i]), jnp.swapaxes(w_in, 1, 2), i, 2 * wp, row(rwkv_mu[i]), ssm_conv_w[i],
                    row(ssm_conv_b[i]), seq, wp, wp + ds)

        rwkv_params = {
            "w0": row(rwkv_w0[i]), "a0": row(rwkv_a0[i]),
            "w2": jnp.pad(rwkv_w2[i], ((0, lora_a), (0, 0))).astype(BF16),
            "a2": jnp.pad(rwkv_a2[i], ((lora_w, 0), (0, 0))).astype(BF16),
            "g2": rwkv_g2[i].astype(BF16),
            "k_k": row(rwkv_k_k[i]), "k_a": row(rwkv_k_a[i]), "r_k": row(rwkv_r_k[i]),
            "gn_g": row(rwkv_gn_g[i]), "gn_b": row(rwkv_gn_b[i]),
        }
        ssd_params = {
            "dt_bias": jnp.pad(ssm_dt_bias[i], (0, LANE - n_heads_s)).reshape(1, -1),
            "a_log": rep(ssm_a_log[i]), "d_skip": rep(ssm_d[i]), "norm_g": row(ssm_norm[i]),
            "expand": expand,
        }
        y_r, y_s = _mixers(p, batch, rwkv_params, ssd_params, consts, n_grp, d_state)

        assert i == depth - 1, "the fused final norm assumes a single layer"
        xt = _ffn2(x1, y_r, y_s, w_out, row(norm_ffn2[i]), ffn2_w_gate, ffn2_w_up, ffn2_w_down,
                   row(norm_final), i)
    return xt.reshape(batch, seq, d)
```

```python
import functools

import jax
import jax.numpy as jnp
import numpy as np
from jax import lax
from jax.experimental import pallas as pl
from jax.experimental.pallas import tpu as pltpu

F32 = jnp.float32
BF16 = jnp.bfloat16

NORM_EPS = 1e-6
RWKV_GN_EPS = 64e-5
SSM_NORM_EPS = 1e-5

HEAD_DIM = 64
CHUNK = 64
BLOCK = 256
LANE = 128
BF16_ROWS = 16
HALO = 8
MIX_ROWS = 512

NT_DIMS = (((1,), (1,)), ((), ()))
TN_DIMS = (((0,), (0,)), ((), ()))

VMEM_LIMIT = 56 * 1024 * 1024


def _dot(a, b):
    return jnp.dot(a, b, preferred_element_type=F32)


def _dot_nt(a, b):
    return lax.dot_general(a, b, NT_DIMS, preferred_element_type=F32)


def _dot_tn(a, b):
    return lax.dot_general(a, b, TN_DIMS, preferred_element_type=F32)


def _rms(x, g, eps):
    return x * lax.rsqrt(jnp.mean(x * x, axis=-1, keepdims=True) + eps) * g


def _split2_rows(x):
    hi = x.astype(BF16)
    lo = (x - hi.astype(F32)).astype(BF16)
    return jnp.concatenate([hi, lo], axis=0)


def _split2_cols(x):
    hi = x.astype(BF16)
    lo = (x - hi.astype(F32)).astype(BF16)
    return jnp.concatenate([hi, lo], axis=1)


def _head_sum(x, bdm_ref):
    xb = x.astype(BF16)
    return jnp.concatenate([_dot(xb[:, c:c + BLOCK], bdm_ref[...]) for c in range(0, x.shape[1], BLOCK)],
                           axis=1)


def _block_diag(xb, bdm):
    return jnp.concatenate([xb, xb, xb, xb], axis=0) * bdm


def _each(f, *lists):
    return [f(*args) for args in zip(*lists)]


def _resident(shape):
    zeros = (0,) * len(shape)
    return pl.BlockSpec(shape, lambda *_: zeros, pipeline_mode=pl.Buffered(1))


FF_CHUNK = 256
FFN_ROWS = 512


def _norm_split(x, g, h_scr, r_scr):
    h_scr[...] = (x * g).astype(BF16)
    r = lax.rsqrt(jnp.mean(x * x, axis=-1, keepdims=True) + NORM_EPS)
    r_scr[...] = jnp.broadcast_to(r, r_scr.shape)


def _swiglu(h_scr, r_scr, wg_ref, wu_ref, wd_ref, act_scr, before_down):
    d_ff = wg_ref.shape[1]
    for c in range(d_ff // FF_CHUNK):
        cs = slice(c * FF_CHUNK, (c + 1) * FF_CHUNK)
        r = jnp.concatenate([r_scr[...]] * (FF_CHUNK // LANE), axis=1)
        gate = _dot(h_scr[...], wg_ref[:, cs]) * r
        up = _dot(h_scr[...], wu_ref[:, cs]) * r
        act_scr[:, cs] = (gate * jax.nn.sigmoid(gate) * up).astype(BF16)
    before_down()
    return _dot(act_scr[...], wd_ref[...])


STAGE_ROWS = 512


def _stage_weights(w_hbms, layer, w_scrs, stage_scr, sems, part="all"):
    step = stage_scr.shape[1]
    pieces = [(k, r0, min(step, w.shape[1] - r0))
              for k, w in enumerate(w_hbms) for r0 in range(0, w.shape[1], step)]

    def view(c):
        k, _, rows = pieces[c]
        return stage_scr.at[c % 2, pl.ds(0, rows), pl.ds(0, w_hbms[k].shape[2])]

    def dma(c):
        k, r0, rows = pieces[c]
        return pltpu.make_async_copy(w_hbms[k].at[layer, pl.ds(r0, rows), :], view(c), sems.at[c % 2])

    if part != "finish":
        for c in range(min(2, len(pieces))):
            dma(c).start()
    if part != "start":
        for c, (k, r0, rows) in enumerate(pieces):
            dma(c).wait()
            w_scrs[k][r0:r0 + rows, 0:w_hbms[k].shape[2]] = view(c)[...].astype(BF16)
            if c + 2 < len(pieces):
                dma(c + 2).start()


def _ffn1_kernel(x_ref, g_ref, wg_hbm, wu_hbm, wd_hbm, o_ref,
                 h_scr, r_scr, act_scr, wg_scr, wu_scr, wd_scr, stage_scr, sems, *, layer):
    @pl.when(pl.program_id(0) == 0)
    def _():
        _stage_weights((wg_hbm, wu_hbm), layer, (wg_scr, wu_scr), stage_scr, sems)
        _stage_weights((wd_hbm,), layer, (wd_scr,), stage_scr, sems, part="start")

    def finish_down():
        pl.when(pl.program_id(0) == 0)(
            lambda: _stage_weights((wd_hbm,), layer, (wd_scr,), stage_scr, sems, part="finish"))

    _norm_split(x_ref[...], g_ref[...], h_scr, r_scr)
    o_ref[...] = x_ref[...] + 0.5 * _swiglu(h_scr, r_scr, wg_scr, wu_scr, wd_scr, act_scr, finish_down)


def _ffn2_kernel(x_ref, yr_ref, ys_ref, wo_hbm, g_ref, wg_hbm, wu_hbm, wd_hbm, gf_ref, o_ref,
                 h_scr, r_scr, act_scr, x_scr, wo_scr, wg_scr, wu_scr, wd_scr, stage_scr, sems, *, layer):
    @pl.when(pl.program_id(0) == 0)
    def _():
        _stage_weights((wo_hbm, wg_hbm, wu_hbm), layer, (wo_scr, wg_scr, wu_scr), stage_scr, sems)
        _stage_weights((wd_hbm,), layer, (wd_scr,), stage_scr, sems, part="start")

    def finish_down():
        pl.when(pl.program_id(0) == 0)(
            lambda: _stage_weights((wd_hbm,), layer, (wd_scr,), stage_scr, sems, part="finish"))

    dr = yr_ref.shape[1]
    x2 = x_ref[...] + _dot(yr_ref[...], wo_scr[0:dr, :]) + _dot(ys_ref[...], wo_scr[dr:, :])
    x_scr[...] = x2
    _norm_split(x2, g_ref[...], h_scr, r_scr)
    y = _swiglu(h_scr, r_scr, wg_scr, wu_scr, wd_scr, act_scr, finish_down)
    o_ref[...] = _rms(x_scr[...] + 0.5 * y, gf_ref[...], NORM_EPS)


def _weight_scratch(*weights):
    widest = max(w.shape[2] for w in weights)
    return ([pltpu.VMEM(w.shape[1:], BF16) for w in weights]
            + [pltpu.VMEM((2, STAGE_ROWS, widest), F32), pltpu.SemaphoreType.DMA((2,))])


def _ffn1(x, g, wg, wu, wd, layer):
    t, d = x.shape
    d_ff = wg.shape[2]
    tm = FFN_ROWS
    assert t % tm == 0 and d_ff % FF_CHUNK == 0
    row = lambda i: (i, 0)
    hbm = pl.BlockSpec(memory_space=pl.ANY)
    return pl.pallas_call(
        functools.partial(_ffn1_kernel, layer=layer),
        grid=(t // tm,),
        in_specs=[pl.BlockSpec((tm, d), row), _resident(g.shape), hbm, hbm, hbm],
        out_specs=pl.BlockSpec((tm, d), row),
        out_shape=jax.ShapeDtypeStruct((t, d), F32),
        scratch_shapes=[pltpu.VMEM((tm, d), BF16), pltpu.VMEM((tm, LANE), F32), pltpu.VMEM((tm, d_ff), BF16)]
        + _weight_scratch(wg, wu, wd),
        compiler_params=pltpu.CompilerParams(
            dimension_semantics=("arbitrary",), vmem_limit_bytes=VMEM_LIMIT),
        name="ffn1",
    )(x, g, wg, wu, wd)


def _ffn2(x, yr, ys, wo, g, wg, wu, wd, gf, layer):
    t, d = x.shape
    d_ff = wg.shape[2]
    dm = yr.shape[1]
    tm = FFN_ROWS
    assert t % tm == 0 and d_ff % FF_CHUNK == 0 and wo.shape[1] == 2 * dm
    row = lambda i: (i, 0)
    hbm = pl.BlockSpec(memory_space=pl.ANY)
    return pl.pallas_call(
        functools.partial(_ffn2_kernel, layer=layer),
        grid=(t // tm,),
        in_specs=[pl.BlockSpec((tm, d), row), pl.BlockSpec((tm, dm), row), pl.BlockSpec((tm, dm), row),
                  hbm, _resident(g.shape), hbm, hbm, hbm, _resident(gf.shape)],
        out_specs=pl.BlockSpec((tm, d), row),
        out_shape=jax.ShapeDtypeStruct((t, d), F32),
        scratch_shapes=[pltpu.VMEM((tm, d), BF16), pltpu.VMEM((tm, LANE), F32), pltpu.VMEM((tm, d_ff), BF16),
                        pltpu.VMEM((tm, d), F32)] + _weight_scratch(wo, wg, wu, wd),
        compiler_params=pltpu.CompilerParams(
            dimension_semantics=("arbitrary",), vmem_limit_bytes=VMEM_LIMIT),
        name="ffn2",
    )(x, yr, ys, wo, g, wg, wu, wd, gf)


IN_CHUNK = 512


def _col_chunks(lo, hi):
    return [(c0, min(IN_CHUNK, hi - c0)) for c0 in range(lo, hi, IN_CHUNK)]


def _inproj_kernel(x_ref, g_ref, wt_hbm, mu_ref, cw_ref, cb_ref, p_ref, h_scr, ext_scr, raw_scr,
                   wt_scr, stage_scr, sems, *, layer, tiles_per_seq, wr, conv_lo):
    tm = x_ref.shape[0]
    dc = cw_ref.shape[1]
    width = cw_ref.shape[0]
    step = pl.program_id(0)
    last = pl.num_programs(0) - 1

    @pl.when(step == 0)
    def _():
        pad_lo = wt_hbm.shape[1] // BF16_ROWS * BF16_ROWS
        wt_scr[pad_lo:, :] = jnp.zeros((wt_scr.shape[0] - pad_lo, wt_scr.shape[1]), BF16)
        _stage_weights((wt_hbm,), layer, (wt_scr,), stage_scr, sems)

    @pl.when(lax.rem(step + tiles_per_seq - 1, tiles_per_seq) == 0)
    def _():
        ext_scr[...] = jnp.zeros_like(ext_scr)

    def behind_halo(cols, pc):
        ext = jnp.concatenate([ext_scr[:, cols], pc], axis=0)
        ext_scr[:, cols] = pc[tm - HALO:, :]
        return ext

    def lerp(c0, n, pc):
        cs = slice(c0, c0 + n)
        shifted = pltpu.roll(behind_halo(cs, pc), 1, axis=0)[HALO:, :]
        p_ref[:, cs] = pc + (shifted - pc) * mu_ref[:, cs]

    def conv_silu(c0, n, pc):
        j = c0 - conv_lo
        cw = slice(j, j + n)
        ext = behind_halo(slice(wr + j, wr + j + n), pc)
        tap = lambda i: cw_ref[width - 1 - i:width - i, cw]
        if width == 4:
            ext1 = pltpu.roll(ext, 1, axis=0)
            far = pltpu.roll(ext * tap(2) + ext1 * tap(3), 2, axis=0)[HALO:, :]
            conv = pc * tap(0) + ext1[HALO:, :] * tap(1) + far + cb_ref[:, cw]
        else:
            conv = pc * tap(0) + cb_ref[:, cw]
            for i in range(1, width):
                conv = conv + pltpu.roll(ext, i, axis=0)[HALO:, :] * tap(i)
        p_ref[:, c0:c0 + n] = conv * jax.nn.sigmoid(conv)

    def plain(c0, n, pc):
        p_ref[:, c0:c0 + n] = pc

    convs = [(conv_silu, c) for c in _col_chunks(conv_lo, conv_lo + dc)]
    light = ([(lerp, c) for c in _col_chunks(0, wr)]
             + [(plain, c) for c in _col_chunks(wr, conv_lo) + _col_chunks(conv_lo + dc, p_ref.shape[1])])
    per_conv = len(light) // max(len(convs), 1)
    order = []
    for n, cv in enumerate(convs):
        order += [cv] + light[n * per_conv:(n + 1) * per_conv]
    order += light[len(convs) * per_conv:]

    def emit(project, finish_previous):
        if project:
            h_scr[...] = _rms(x_ref[...], g_ref[...], NORM_EPS).astype(BF16)
        for finish, (c0, n) in order:
            if finish_previous:
                before = raw_scr[:, c0:c0 + n]
            if project:
                raw_scr[:, c0:c0 + n] = _dot_nt(h_scr[...], wt_scr[c0:c0 + n, :])
            if finish_previous:
                finish(c0, n, before)

    pl.when(step == 0)(lambda: emit(True, False))
    pl.when(jnp.logical_and(step > 0, step < last))(lambda: emit(True, True))
    pl.when(step == last)(lambda: emit(False, True))


IN_STAGE_ROWS = 1024


def _inproj(x, g, wt_all, layer, width_out, mu, conv_w, conv_b, seq, wr, conv_lo):
    t, d = x.shape
    tm = 512
    dc = conv_w.shape[1]
    n_cols = wt_all.shape[1]
    assert t % tm == 0 and seq % tm == 0 and conv_w.shape[0] <= HALO
    assert all(v % LANE == 0 for v in (wr, conv_lo, dc, width_out)) and wr <= conv_lo <= n_cols - dc
    assert n_cols <= width_out
    body = functools.partial(_inproj_kernel, layer=layer, tiles_per_seq=seq // tm, wr=wr, conv_lo=conv_lo)
    n = t // tm
    return pl.pallas_call(
        body,
        grid=(n + 1,),
        in_specs=[pl.BlockSpec((tm, d), lambda i: (jnp.minimum(i, n - 1), 0)), _resident(g.shape),
                  pl.BlockSpec(memory_space=pl.ANY), _resident(mu.shape), _resident(conv_w.shape),
                  _resident(conv_b.shape)],
        out_specs=pl.BlockSpec((tm, width_out), lambda i: (jnp.maximum(i - 1, 0), 0)),
        out_shape=jax.ShapeDtypeStruct((t, width_out), F32),
        scratch_shapes=[pltpu.VMEM((tm, d), BF16), pltpu.VMEM((HALO, wr + dc), F32),
                        pltpu.VMEM((tm, width_out), F32), pltpu.VMEM((width_out, d), BF16),
                        pltpu.VMEM((2, IN_STAGE_ROWS, d), F32), pltpu.SemaphoreType.DMA((2,))],
        compiler_params=pltpu.CompilerParams(
            dimension_semantics=("arbitrary",), vmem_limit_bytes=VMEM_LIMIT),
        name="in_proj",
    )(x, g, wt_all, mu, conv_w, conv_b)


def _rwkv_kernel(p_ref, w0_ref, w2_ref, a0_ref, a2_ref, g2_ref, kk_ref, ka_ref, rk_ref,
                 gng_ref, gnb_ref, tri2_ref, bdm_ref, bdm32_ref, trm_ref, eye_ref,
                 o_ref,
                 zt_scr, k_scr, a_scr, b_scr, ld_scr, y_scr, bonus_scr, gate_scr):
    n_seq, rows_seq, dr = o_ref.shape
    tl = n_seq * rows_seq

    @pl.when(pl.program_id(1) == 0)
    def _():
        zt_scr[...] = jnp.zeros_like(zt_scr)

    cols_of_p = lambda lo, hi: p_ref[:, :, lo:hi].reshape(tl, hi - lo)
    r = cols_of_p(0, dr)
    k = cols_of_p(dr, 2 * dr)
    v = cols_of_p(2 * dr, 3 * dr)
    wa = cols_of_p(3 * dr, 3 * dr + LANE)
    gd = cols_of_p(3 * dr + LANE, 3 * dr + 2 * LANE)

    wpre = w0_ref[...] + _dot(jnp.tanh(wa).astype(BF16), w2_ref[...])
    ld_scr[...] = jax.nn.sigmoid(wpre) * (-jnp.exp(-0.5))
    iclr = jax.nn.sigmoid(a0_ref[...] + _dot(wa.astype(BF16), a2_ref[...]))
    gate_scr[...] = _dot(jax.nn.sigmoid(gd).astype(BF16), g2_ref[...])

    kkv = k * kk_ref[...]
    kkn = kkv * lax.rsqrt(jnp.maximum(_head_sum(kkv * kkv, bdm_ref), 1e-24))
    kmod = k * (1.0 + (iclr - 1.0) * ka_ref[...])
    k_scr[...] = kmod
    a_scr[...] = -kkn
    b_scr[...] = kkn * iclr
    bonus_scr[...] = _head_sum(r * kmod * rk_ref[...], bdm_ref) * v

    bdmask32 = bdm32_ref[...] > 0.5
    trmask = trm_ref[...] > 0.5
    bd = lambda xb: _block_diag(xb, bdm_ref[...])
    n_q = dr // BLOCK

    probs = [(slice(s * rows_seq + c * CHUNK, s * rows_seq + (c + 1) * CHUNK),
              slice(q * BLOCK, (q + 1) * BLOCK), s * n_q + q)
             for c in range(rows_seq // CHUNK) for s in range(n_seq) for q in range(n_q)]
    tile = lambda scr: [scr[rows, cols] for rows, cols, _ in probs]
    from_p = lambda col0: [p_ref[s, c * CHUNK:(c + 1) * CHUNK, col0 + q * BLOCK:col0 + (q + 1) * BLOCK]
                           for c in range(rows_seq // CHUNK) for s in range(n_seq) for q in range(n_q)]

    ld = tile(ld_scr)
    cum = _each(lambda x: _dot(tri2_ref[...], _split2_rows(x)), ld)
    b_c = tile(b_scr)
    k_c = tile(k_scr)
    vb = _each(lambda x: x.astype(BF16), from_p(2 * dr))
    atb = _each(lambda a, cm, l: (a * jnp.exp(cm - l)).astype(BF16), tile(a_scr), cum, ld)
    rtb = _each(lambda x, cm: (x * jnp.exp(cm)).astype(BF16), from_p(0), cum)
    inv = _each(lambda cm: jnp.exp(-cm), cum)
    g_end = _each(lambda cm: jnp.exp(cm[CHUNK - 1:CHUNK, :]), cum)
    to_end = _each(lambda cm: jnp.exp(cm[CHUNK - 1:CHUNK, :] - cm), cum)
    ar = _each(lambda a, x: jnp.concatenate([a, x], axis=0), atb, rtb)
    masked = lambda x: jnp.where(trmask, x, 0.0)
    sb32 = _each(lambda a, b, i: masked(_dot_nt(a, bd((b * i).astype(BF16)))), ar, b_c, inv)
    sk = _each(lambda a, x, i: masked(_dot_nt(a, bd((x * i).astype(BF16)))).astype(BF16), ar, k_c, inv)
    kv = _each(lambda x, y: _dot(x, bd(y)), sk, vb)
    t_inv = _each(lambda x: eye_ref[...] + x[0:CHUNK], sb32)
    pw = _each(lambda x: x[0:CHUNK].astype(BF16), sb32)
    pw = _each(lambda x: _dot(x, bd(x)).astype(BF16), pw)
    for i in range(1, 5):
        res = _each(lambda t, x: _dot(jnp.concatenate([t.astype(BF16), x], axis=0), bd(x)), t_inv, pw)
        t_inv = _each(lambda t, x: t + x[0:CHUNK], t_inv, res)
        pw = _each(lambda x: x[CHUNK:].astype(BF16), res)
    t_inv = _each(lambda t, x: t + _dot(t.astype(BF16), bd(x)), t_inv, pw)
    xx = _each(lambda t, a, x: _dot(t.astype(BF16),
                                    jnp.concatenate([bd(a), bd(x[0:CHUNK].astype(BF16))], axis=1)),
               t_inv, atb, kv)
    x1b = _each(lambda x: x[:, 0:BLOCK].astype(BF16), xx)
    x2 = _each(lambda x: x[:, BLOCK:], xx)
    bgb = _each(lambda b, e: (b * e).astype(BF16), b_c, to_end)
    kgb = _each(lambda x, e: (x * e).astype(BF16), k_c, to_end)
    p_t = _each(lambda x, b: _dot_tn(x, b).astype(BF16), x1b, bgb)
    c_t = _each(lambda x, y, b, kg: _dot_tn(jnp.concatenate([x.astype(BF16), y], axis=0),
                                            jnp.concatenate([b, kg], axis=0)), x2, vb, bgb, kgb)

    zs = [zt_scr[j] for j in range(n_seq * n_q)]
    zb = []
    for (_, _, j), g, pt, ct in zip(probs, g_end, p_t, c_t):
        zb.append(zs[j].astype(BF16))
        zs[j] = zs[j] * g + jnp.where(bdmask32, _dot(zb[-1], pt) + ct, 0.0)
    for j in range(n_seq * n_q):
        zt_scr[j] = zs[j]
    xz = _each(lambda x, rt, z: _dot_nt(jnp.concatenate([x, rt], axis=0), z), x1b, rtb, zb)
    ubd = _each(lambda m, x: bd((m[0:CHUNK] + x).astype(BF16)), xz, x2)
    for (rows, cols, _), m, n_rb, u, o in zip(probs, xz, sb32, ubd, kv):
        y_scr[rows, cols] = m[CHUNK:] + _dot(n_rb[CHUNK:].astype(BF16), u) + o[CHUNK:]

    y = y_scr[...]
    inv_n = 1.0 / HEAD_DIM
    mean = _head_sum(y, bdm_ref) * inv_n
    yc = y - mean
    var = _head_sum(yc * yc, bdm_ref) * inv_n
    yn = yc * lax.rsqrt(var + RWKV_GN_EPS) * gng_ref[...] + gnb_ref[...]
    o_ref[...] = ((yn + bonus_scr[...]) * gate_scr[...]).astype(BF16).reshape(n_seq, rows_seq, dr)


def _ssd_kernel(p_ref, dtb_ref, alog_ref, dsk_ref, ng_ref, exp_ref,
                tri2_ref, ones_ref, bdm_ref, tri_ref, triu_ref,
                o_ref,
                st_scr, xs_scr, xdt_scr, a_scr, y_scr):
    n_seq, rows_seq, ds = o_ref.shape
    tl = n_seq * rows_seq
    _, d_state, _ = st_scr.shape
    n_grp = ds // BLOCK
    dc = ds + 2 * n_grp * d_state

    @pl.when(pl.program_id(1) == 0)
    def _():
        st_scr[...] = jnp.zeros_like(st_scr)

    cols_of_p = lambda lo, hi: p_ref[:, :, lo:hi].reshape(tl, hi - lo)
    xs = cols_of_p(ds, 2 * ds)
    dtr = cols_of_p(ds + dc, ds + dc + LANE) + dtb_ref[...]
    dt_c = jnp.maximum(dtr, 0.0) + jnp.log1p(jnp.exp(-jnp.abs(dtr)))
    dt = _dot(_split2_cols(dt_c), exp_ref[...])
    xs_scr[...] = xs
    xdt_scr[...] = xs * dt
    a_scr[...] = dt * (-jnp.exp(alog_ref[...]))
    bc0 = 2 * ds

    incl = tri_ref[...] > 0.5
    upper = triu_ref[...] > 0.5

    order = [(s, c, g) for c in range(rows_seq // CHUNK) for s in range(n_seq) for g in range(n_grp)]
    probs = [(slice(s * rows_seq + c * CHUNK, s * rows_seq + (c + 1) * CHUNK), g, s * n_grp + g)
             for s, c, g in order]
    gcols = lambda g: slice(g * BLOCK, (g + 1) * BLOCK)
    from_p = lambda col0: [p_ref[s, c * CHUNK:(c + 1) * CHUNK, col0 + g * d_state:col0 + (g + 1) * d_state]
                           .astype(BF16) for s, c, g in order]

    a_c = [a_scr[rows, gcols(g)] for rows, g, _ in probs]
    e1 = _each(lambda a: _dot(tri2_ref[...], _split2_rows(a)), a_c)
    e2 = _each(lambda a: _dot(ones_ref[...], _split2_rows(jnp.where(upper, a, 0.0))), a_c)
    l_w = _each(lambda x, y: jnp.exp(jnp.where(incl, x - y, -jnp.inf)), e1, e2)
    b_g = from_p(bc0)
    c_g = from_p(bc0 + n_grp * d_state)
    scores = _each(lambda c, b: _dot_nt(c, jnp.concatenate([b, b, b, b], axis=0)), c_g, b_g)
    xdt_c = [xdt_scr[rows, gcols(g)] for rows, g, _ in probs]
    y_diag = _each(lambda x, l, y: _dot((x * l).astype(BF16), _block_diag(y.astype(BF16), bdm_ref[...])),
                   scores, l_w, xdt_c)
    new = _each(lambda b, x, e: _dot_tn(b, (x * jnp.exp(e[CHUNK - 1:CHUNK, :] - e)).astype(BF16)),
                b_g, xdt_c, e1)

    sts = [st_scr[j] for j in range(n_seq * n_grp)]
    stb = []
    for (_, _, j), e, n in zip(probs, e1, new):
        stb.append(sts[j].astype(BF16))
        sts[j] = sts[j] * jnp.exp(e[CHUNK - 1:CHUNK, :]) + n
    for j in range(n_seq * n_grp):
        st_scr[j] = sts[j]
    for (rows, g, _), c, st, yd, e in zip(probs, c_g, stb, y_diag, e1):
        y_scr[rows, gcols(g)] = yd + _dot(c, st) * jnp.exp(e)

    z = cols_of_p(0, ds)
    y = (y_scr[...] + dsk_ref[...] * xs_scr[...]) * (z * jax.nn.sigmoid(z))
    o_ref[...] = _rms(y, ng_ref[...], SSM_NORM_EPS).astype(BF16).reshape(n_seq, rows_seq, ds)


MIX_SEQS = 2


def _mixers(p, batch, rwkv_params, ssd_params, consts, n_grp, d_state):
    t, wp2 = p.shape
    wp = wp2 // 2
    dr = rwkv_params["w0"].shape[1]
    ds = ssd_params["norm_g"].shape[1]
    dc = ds + 2 * n_grp * d_state
    seq = t // batch
    rows = MIX_ROWS // MIX_SEQS
    assert dr % BLOCK == 0 and wp == 3 * dr + 2 * LANE and ds == n_grp * BLOCK and ds + dc + LANE <= wp
    assert t == batch * seq and batch % MIX_SEQS == 0 and seq % rows == 0 and rows % CHUNK == 0
    r_ops = ([rwkv_params[n] for n in ["w0", "w2", "a0", "a2", "g2", "k_k", "k_a", "r_k", "gn_g", "gn_b"]]
             + [consts[n] for n in ["tri2", "bdm", "bdm32", "trm", "eye"]])
    s_ops = ([ssd_params[n] for n in ["dt_bias", "a_log", "d_skip", "norm_g", "expand"]]
             + [consts[n] for n in ["tri2", "ones", "bdm", "tri", "triu"]])
    tl = MIX_ROWS
    r_scr = ([pltpu.VMEM((MIX_SEQS * (dr // BLOCK), BLOCK, BLOCK), F32)]
             + [pltpu.VMEM((tl, dr), F32) for _ in range(7)])
    s_scr = [pltpu.VMEM((MIX_SEQS * n_grp, d_state, BLOCK), F32)] + [pltpu.VMEM((tl, ds), F32) for _ in range(4)]

    def body(pr_ref, ps_ref, *refs):
        r_in, refs = refs[:len(r_ops)], refs[len(r_ops):]
        s_in, refs = refs[:len(s_ops)], refs[len(s_ops):]
        (or_ref, os_ref), scr = refs[:2], refs[2:]
        _rwkv_kernel(pr_ref, *r_in, or_ref, *scr[:len(r_scr)])
        _ssd_kernel(ps_ref, *s_in, os_ref, *scr[len(r_scr):])

    tile = lambda col: pl.BlockSpec((MIX_SEQS, rows, wp), lambda b, l: (b, l, col))
    out = lambda w: pl.BlockSpec((MIX_SEQS, rows, w), lambda b, l: (b, l, 0))
    p3 = p.reshape(batch, seq, wp2)
    y_r, y_s = pl.pallas_call(
        body,
        grid=(batch // MIX_SEQS, seq // rows),
        in_specs=[tile(0), tile(1)] + [_resident(o.shape) for o in r_ops + s_ops],
        out_specs=[out(dr), out(ds)],
        out_shape=[jax.ShapeDtypeStruct((batch, seq, dr), BF16), jax.ShapeDtypeStruct((batch, seq, ds), BF16)],
        scratch_shapes=r_scr + s_scr,
        compiler_params=pltpu.CompilerParams(
            dimension_semantics=("arbitrary", "arbitrary"), vmem_limit_bytes=VMEM_LIMIT),
        name="mixers",
    )(p3, p3, *r_ops, *s_ops)
    return y_r.reshape(t, dr), y_s.reshape(t, ds)


def _mask_consts():
    row = np.arange(BLOCK)[:, None]
    col = np.arange(BLOCK)[None, :]
    t = np.arange(CHUNK)[:, None]
    s = col % CHUNK
    tri64 = (np.arange(CHUNK)[None, :] <= t)
    bdm = row // CHUNK == col // CHUNK
    return {
        "bdm": bdm.astype(BF16),
        "bdm32": bdm.astype(F32),
        "trm": np.concatenate([s < t, s <= t], axis=0).astype(F32),
        "tri": (s <= t).astype(F32),
        "triu": (s >= t).astype(F32),
        "eye": (s == t).astype(F32),
        "tri2": np.concatenate([tri64, tri64], axis=1).astype(BF16),
        "ones": np.ones((CHUNK, 2 * CHUNK), BF16),
    }


def kernel(x, norm_ffn1, ffn1_w_gate, ffn1_w_up, ffn1_w_down, norm_mix, w_in, rwkv_mu, rwkv_w0, rwkv_w2,
           rwkv_a0, rwkv_a2, rwkv_g2, rwkv_k_k, rwkv_k_a, rwkv_r_k, rwkv_gn_g, rwkv_gn_b, ssm_conv_w,
           ssm_conv_b, ssm_dt_bias, ssm_a_log, ssm_d, ssm_norm, w_out, norm_ffn2, ffn2_w_gate, ffn2_w_up,
           ffn2_w_down, norm_final):
    batch, seq, d = x.shape
    depth = norm_ffn1.shape[0]
    dr = rwkv_w0.shape[1]
    ds = ssm_norm.shape[1]
    dc = ssm_conv_w.shape[2]
    n_heads_s = ssm_a_log.shape[1]
    lora_w = rwkv_w2.shape[1]
    lora_a = rwkv_a2.shape[1]
    lora_g = rwkv_g2.shape[1]
    n_grp = ds // BLOCK
    d_state = (dc - ds) // (2 * n_grp)
    wp = 3 * dr + 2 * LANE
    assert ds // n_heads_s == HEAD_DIM and lora_w + lora_a == LANE and lora_g == LANE
    assert n_heads_s <= LANE and ds + dc + LANE <= wp and seq % MIX_ROWS == 0

    consts = _mask_consts()
    lane_head = np.arange(LANE)[:, None] == (np.arange(ds) // HEAD_DIM)[None, :]
    expand = np.concatenate([lane_head, lane_head], axis=0).astype(BF16)
    row = lambda a: a.reshape(1, -1)
    rep = lambda a: jnp.repeat(a, HEAD_DIM).reshape(1, -1)

    xt = x.reshape(batch * seq, d)
    for i in range(depth):
        x1 = _ffn1(xt, row(norm_ffn1[i]), ffn1_w_gate, ffn1_w_up, ffn1_w_down, i)

        p = _inproj(x1, row(norm_mix[i]), jnp.swapaxes(w_in, 1, 2), i, 2 * wp, row(rwkv_mu[i]), ssm_conv_w[i],
                    row(ssm_conv_b[i]), seq, wp, wp + ds)

        rwkv_params = {
            "w0": row(rwkv_w0[i]), "a0": row(rwkv_a0[i]),
            "w2": jnp.pad(rwkv_w2[i], ((0, lora_a), (0, 0))).astype(BF16),
            "a2": jnp.pad(rwkv_a2[i], ((lora_w, 0), (0, 0))).astype(BF16),
            "g2": rwkv_g2[i].astype(BF16),
            "k_k": row(rwkv_k_k[i]), "k_a": row(rwkv_k_a[i]), "r_k": row(rwkv_r_k[i]),
            "gn_g": row(rwkv_gn_g[i]), "gn_b": row(rwkv_gn_b[i]),
        }
        ssd_params = {
            "dt_bias": jnp.pad(ssm_dt_bias[i], (0, LANE - n_heads_s)).reshape(1, -1),
            "a_log": rep(ssm_a_log[i]), "d_skip": rep(ssm_d[i]), "norm_g": row(ssm_norm[i]),
            "expand": expand,
        }
        y_r, y_s = _mixers(p, batch, rwkv_params, ssd_params, consts, n_grp, d_state)

        assert i == depth - 1, "the fused final norm assumes a single layer"
        xt = _ffn2(x1, y_r, y_s, w_out, row(norm_ffn2[i]), ffn2_w_gate, ffn2_w_up, ffn2_w_down,
                   row(norm_final), i)
    return xt.reshape(batch, seq, d)
```
